```python
import functools
import jax, jax.numpy as jnp
from jax import lax
import numpy as np

D_MODEL = 1024
BATCH = 8
SEQ = 2048
DEPTH = 1
DEC_BATCH = 128
DEC_SEQ = 8
PAST_LEN = 2048
PAGE_SIZE = 128

N_HEADS = 16
HEAD_DIM = D_MODEL // N_HEADS
D_ATTN = N_HEADS * HEAD_DIM
D_RNN = D_MODEL
N_RNN_BLOCKS = 16
RNN_BLOCK = D_RNN // N_RNN_BLOCKS
CONV_W = 4
LRU_C = 8.0
N_EXPERTS = 32
TOP_K = 4
D_EXPERT = D_MODEL
SWIGLU_LIMIT = 7.0
SWIGLU_ALPHA = 1.702
Q_BLOCK = 128
ROW_BLOCK = 128
EPS = 1e-6
D_IN = D_RNN + 3 * D_ATTN + N_HEADS + 2 * D_MODEL

kernel_name = "hybrid_rglru_fox_moe_step"


def rms_norm(x, g):
    xf = x.astype(jnp.float32)
    y = xf * lax.rsqrt(jnp.mean(xf * xf, axis=-1, keepdims=True) + EPS)
    return (y * g.astype(jnp.float32)).astype(x.dtype)


def causal_conv(x, prev, w, b):
    s = x.shape[1]
    xp = jnp.concatenate([prev.astype(x.dtype), x], axis=1)
    y = b
    for j in range(CONV_W):
        y = y + w[j] * xp[:, j:j + s]
    return y, xp[:, -(CONV_W - 1):]


def block_diag_linear(x, w, b):
    bsz, s, _ = x.shape
    xb = x.reshape(bsz, s, N_RNN_BLOCKS, RNN_BLOCK)
    return jnp.einsum('bsnc,ncd->bsnd', xb, w).reshape(bsz, s, D_RNN) + b


def rg_lru(x, h0, lam, w_a, b_a, w_x, b_x, reset_first):
    r = jax.nn.sigmoid(block_diag_linear(x, w_a, b_a).astype(jnp.float32))
    i = jax.nn.sigmoid(block_diag_linear(x, w_x, b_x).astype(jnp.float32))
    log_a = -LRU_C * r * jax.nn.softplus(-lam.astype(jnp.float32))
    a = jnp.exp(log_a)
    mult = jnp.sqrt(-jnp.expm1(2.0 * log_a))
    if reset_first:
        mult = mult.at[:, 0].set(1.0)
    u = mult * i * x.astype(jnp.float32)
    u = u.at[:, 0].add(a[:, 0] * h0.astype(jnp.float32))

    def combine(left, right):
        a_l, b_l = left
        a_r, b_r = right
        return a_l * a_r, a_r * b_l + b_r

    _, h = lax.associative_scan(combine, (a, u), axis=1)
    return h.astype(x.dtype), h[:, -1]


def fox_prompt(q, k, v, logf):
    s_len = q.shape[1]
    scale = HEAD_DIM ** -0.5
    ft = jnp.swapaxes(jnp.cumsum(logf, axis=1), 1, 2)
    outs = []
    for blk in range(s_len // Q_BLOCK):
        q0 = blk * Q_BLOCK
        kend = q0 + Q_BLOCK
        sc = jnp.einsum('bqhd,bkhd->bhqk', q[:, q0:kend], k[:, :kend]).astype(jnp.float32) * scale
        bias = ft[:, :, q0:kend, None] - ft[:, :, None, :kend]
        mask = jnp.arange(kend)[None, :] <= (q0 + jnp.arange(Q_BLOCK))[:, None]
        p = jax.nn.softmax(jnp.where(mask, sc + bias, -jnp.inf), axis=-1)
        outs.append(jnp.einsum('bhqk,bkhd->bqhd', p.astype(v.dtype), v[:, :kend]))
    return jnp.concatenate(outs, axis=1)


def fox_sample(q, k, v, logf, cache_k, cache_v, cache_logf, page_table):
    db, ds = q.shape[:2]
    past = page_table.shape[1] * PAGE_SIZE
    scale = HEAD_DIM ** -0.5
    k_all = jnp.concatenate([cache_k[page_table].reshape(db, past, N_HEADS, HEAD_DIM).astype(k.dtype), k], axis=1)
    v_all = jnp.concatenate([cache_v[page_table].reshape(db, past, N_HEADS, HEAD_DIM).astype(v.dtype), v], axis=1)
    lf_all = jnp.concatenate([cache_logf[page_table].reshape(db, past, N_HEADS).astype(jnp.float32), logf], axis=1)
    ft = jnp.swapaxes(jnp.cumsum(lf_all, axis=1), 1, 2)
    sc = jnp.einsum('bqhd,bkhd->bhqk', q, k_all).astype(jnp.float32) * scale
    bias = ft[:, :, past:, None] - ft[:, :, None, :]
    mask = jnp.arange(past + ds)[None, :] <= (past + jnp.arange(ds))[:, None]
    p = jax.nn.softmax(jnp.where(mask, sc + bias, -jnp.inf), axis=-1)
    return jnp.einsum('bhqk,bkhd->bqhd', p.astype(v_all.dtype), v_all)


def mixer(h, prev_conv, h0, reset_first, attn_fn, lw):
    bsz, s, _ = h.shape
    z = h @ lw['w_in']
    cuts = [D_RNN, D_RNN + D_ATTN, D_RNN + 2 * D_ATTN, D_RNN + 3 * D_ATTN,
            D_RNN + 3 * D_ATTN + N_HEADS, D_RNN + 3 * D_ATTN + N_HEADS + D_MODEL]
    x_rnn, q, k, v, f_logit, g_a, g_b = jnp.split(z, cuts, axis=-1)
    xc, conv_state = causal_conv(x_rnn, prev_conv, lw['conv_w'], lw['conv_b'])
    y_rnn, h_last = rg_lru(xc, h0, lw['rg_lambda'], lw['rg_w_a'], lw['rg_b_a'], lw['rg_w_x'], lw['rg_b_x'], reset_first)
    q = q.reshape(bsz, s, N_HEADS, HEAD_DIM)
    k = k.reshape(bsz, s, N_HEADS, HEAD_DIM)
    v = v.reshape(bsz, s, N_HEADS, HEAD_DIM)
    logf = jax.nn.log_sigmoid((f_logit + lw['b_forget']).astype(jnp.float32))
    y_attn = attn_fn(q, k, v, logf).reshape(bsz, s, D_ATTN)
    merged = jax.nn.sigmoid(g_a) * y_rnn + jax.nn.sigmoid(g_b) * y_attn
    return merged @ lw['w_out'], (k, v, logf, conv_state, h_last)


def moe(h, w_router, b_router, w_gate_up, b_gate_up, w_down, b_down):
    bsz, s, d = h.shape
    t = h.reshape(-1, d)
    n_tok = t.shape[0]
    logits = (t @ w_router + b_router).astype(jnp.float32)
    top_v, top_e = lax.top_k(logits, TOP_K)
    gate = jax.nn.softmax(top_v, axis=-1)
    n_assign = n_tok * TOP_K
    flat_e = top_e.reshape(-1)
    flat_tok = jnp.repeat(jnp.arange(n_tok, dtype=jnp.int32), TOP_K)
    flat_w = gate.reshape(-1)
    order = jnp.argsort(flat_e)
    se, stok, sw = flat_e[order], flat_tok[order], flat_w[order]
    counts = jnp.zeros((N_EXPERTS,), jnp.int32).at[flat_e].add(1)
    padded = (counts + ROW_BLOCK - 1) // ROW_BLOCK * ROW_BLOCK
    start = jnp.cumsum(counts) - counts
    pend = jnp.cumsum(padded)
    pstart = pend - padded
    dest = pstart[se] + jnp.arange(n_assign, dtype=jnp.int32) - start[se]
    n_blocks = -(-n_assign // ROW_BLOCK) + N_EXPERTS
    n_rows = n_blocks * ROW_BLOCK
    row_tok = jnp.zeros((n_rows,), jnp.int32).at[dest].set(stok)
    row_w = jnp.zeros((n_rows,), jnp.float32).at[dest].set(sw)
    block_e = jnp.minimum(jnp.searchsorted(pend, jnp.arange(n_blocks, dtype=jnp.int32) * ROW_BLOCK, side='right'), N_EXPERTS - 1)
    xr = t[row_tok].reshape(n_blocks, ROW_BLOCK, d)

    def expert_block(args):
        xb, e = args
        gu = xb @ w_gate_up[e] + b_gate_up[e]
        x_glu = jnp.minimum(gu[:, :D_EXPERT], SWIGLU_LIMIT)
        x_lin = jnp.clip(gu[:, D_EXPERT:], -SWIGLU_LIMIT, SWIGLU_LIMIT)
        act = x_glu * jax.nn.sigmoid(SWIGLU_ALPHA * x_glu) * (x_lin + 1.0)
        return act @ w_down[e] + b_down[e]

    yr = lax.map(expert_block, (xr, block_e)).reshape(n_rows, d)
    y = jax.ops.segment_sum(yr * row_w[:, None].astype(yr.dtype), row_tok, num_segments=n_tok)
    return y.reshape(bsz, s, d)


def trunk_layer(x, c, lw, prev_conv, h0, reset_first, attn_fn):
    ada = jax.nn.silu(c) @ lw['w_ada'] + lw['b_ada']
    sh1, sc1, gt1, sh2, sc2, gt2 = [a[:, None, :] for a in jnp.split(ada, 6, axis=-1)]
    h = rms_norm(x, lw['g_mix_pre']) * (1.0 + sc1) + sh1
    out, st = mixer(h, prev_conv, h0, reset_first, attn_fn, lw)
    x = x + gt1 * rms_norm(out, lw['g_mix_post'])
    h = rms_norm(x, lw['g_ffn_pre']) * (1.0 + sc2) + sh2
    f = moe(h, lw['w_router'], lw['b_router'], lw['w_gate_up'], lw['b_gate_up'], lw['w_down'], lw['b_down'])
    x = x + gt2 * rms_norm(f, lw['g_ffn_post'])
    return x, st


def setup_inputs(seed: int = 0) -> dict:
    key = jax.random.key(seed)
    ks = iter(jax.random.split(key, 40))

    def nrm(shape, scale):
        return scale * jax.random.normal(next(ks), shape, jnp.float32)

    n_pages = PAST_LEN // PAGE_SIZE
    n_pool = (DEC_BATCH * n_pages * 5 + 3) // 4
    perm = jax.random.permutation(next(ks), n_pool)
    page_table = perm[:DEC_BATCH * n_pages].reshape(DEC_BATCH, n_pages).astype(jnp.int32)
    a0 = jax.random.uniform(next(ks), (DEPTH, D_RNN), jnp.float32, 0.9, 0.999)
    p = a0 ** (1.0 / LRU_C)
    rg_lambda = jnp.log(p) - jnp.log1p(-p)
    b_forget = jax.random.uniform(next(ks), (DEPTH, N_HEADS), jnp.float32, 1.0, 4.0)
    cache_logf = jax.nn.log_sigmoid(3.0 + nrm((DEPTH, n_pool, PAGE_SIZE, N_HEADS), 1.0))
    dm = D_MODEL ** -0.5
    return {
        'x_prompt': nrm((BATCH, SEQ, D_MODEL), 1.0),
        'x_sample': nrm((DEC_BATCH, DEC_SEQ, D_MODEL), 1.0),
        'cache_k': nrm((DEPTH, n_pool, PAGE_SIZE, N_HEADS, HEAD_DIM), 1.0),
        'cache_v': nrm((DEPTH, n_pool, PAGE_SIZE, N_HEADS, HEAD_DIM), 1.0),
        'cache_logf': cache_logf,
        'state_conv': nrm((DEPTH, DEC_BATCH, CONV_W - 1, D_RNN), 1.0),
        'state_h': nrm((DEPTH, DEC_BATCH, D_RNN), 0.5),
        'page_table': page_table,
        'c_prompt': nrm((BATCH, D_MODEL), 1.0),
        'c_sample': nrm((DEC_BATCH, D_MODEL), 1.0),
        'w_ada': nrm((DEPTH, D_MODEL, 6 * D_MODEL), 0.3 * dm),
        'b_ada': nrm((DEPTH, 6 * D_MODEL), 0.1),
        'g_mix_pre': 1.0 + nrm((DEPTH, D_MODEL), 0.05),
        'g_mix_post': 1.0 + nrm((DEPTH, D_MODEL), 0.05),
        'w_in': nrm((DEPTH, D_MODEL, D_IN), dm),
        'b_forget': b_forget,
        'conv_w': nrm((DEPTH, CONV_W, D_RNN), CONV_W ** -0.5),
        'conv_b': nrm((DEPTH, D_RNN), 0.02),
        'rg_w_a': nrm((DEPTH, N_RNN_BLOCKS, RNN_BLOCK, RNN_BLOCK), RNN_BLOCK ** -0.5),
        'rg_b_a': nrm((DEPTH, D_RNN), 0.02),
        'rg_w_x': nrm((DEPTH, N_RNN_BLOCKS, RNN_BLOCK, RNN_BLOCK), RNN_BLOCK ** -0.5),
        'rg_b_x': nrm((DEPTH, D_RNN), 0.02),
        'rg_lambda': rg_lambda,
        'w_out': nrm((DEPTH, D_MODEL, D_MODEL), dm),
        'g_ffn_pre': 1.0 + nrm((DEPTH, D_MODEL), 0.05),
        'g_ffn_post': 1.0 + nrm((DEPTH, D_MODEL), 0.05),
        'w_router': nrm((DEPTH, D_MODEL, N_EXPERTS), dm),
        'b_router': nrm((DEPTH, N_EXPERTS), 0.01),
        'w_gate_up': nrm((DEPTH, N_EXPERTS, D_MODEL, 2 * D_EXPERT), dm),
        'b_gate_up': nrm((DEPTH, N_EXPERTS, 2 * D_EXPERT), 0.01),
        'w_down': nrm((DEPTH, N_EXPERTS, D_EXPERT, D_MODEL), D_EXPERT ** -0.5),
        'b_down': nrm((DEPTH, N_EXPERTS, D_MODEL), 0.01),
    }


def reference(x_prompt, x_sample, cache_k, cache_v, cache_logf, state_conv, state_h, page_table,
              c_prompt, c_sample, w_ada, b_ada, g_mix_pre, g_mix_post, w_in, b_forget,
              conv_w, conv_b, rg_w_a, rg_b_a, rg_w_x, rg_b_x, rg_lambda, w_out,
              g_ffn_pre, g_ffn_post, w_router, b_router, w_gate_up, b_gate_up, w_down, b_down):
    yp, ys = x_prompt, x_sample
    kp_l, vp_l, lfp_l, cvp_l, hp_l = [], [], [], [], []
    ks_l, vs_l, lfs_l, cvs_l, hs_l = [], [], [], [], []
    for l in range(DEPTH):
        lw = {
            'w_ada': w_ada[l], 'b_ada': b_ada[l], 'g_mix_pre': g_mix_pre[l], 'g_mix_post': g_mix_post[l],
            'w_in': w_in[l], 'b_forget': b_forget[l], 'conv_w': conv_w[l], 'conv_b': conv_b[l],
            'rg_w_a': rg_w_a[l], 'rg_b_a': rg_b_a[l], 'rg_w_x': rg_w_x[l], 'rg_b_x': rg_b_x[l],
            'rg_lambda': rg_lambda[l], 'w_out': w_out[l], 'g_ffn_pre': g_ffn_pre[l], 'g_ffn_post': g_ffn_post[l],
            'w_router': w_router[l], 'b_router': b_router[l], 'w_gate_up': w_gate_up[l],
            'b_gate_up': b_gate_up[l], 'w_down': w_down[l], 'b_down': b_down[l],
        }
        zero_conv = jnp.zeros((yp.shape[0], CONV_W - 1, D_RNN), yp.dtype)
        zero_h = jnp.zeros((yp.shape[0], D_RNN), jnp.float32)
        yp, (kp, vp, lfp, cvp, hp) = trunk_layer(yp, c_prompt, lw, zero_conv, zero_h, True, fox_prompt)
        attn_sample = functools.partial(fox_sample, cache_k=cache_k[l], cache_v=cache_v[l],
                                        cache_logf=cache_logf[l], page_table=page_table)
        ys, (ks, vs, lfs, cvs, hs) = trunk_layer(ys, c_sample, lw, state_conv[l], state_h[l], False, attn_sample)
        kp_l.append(kp); vp_l.append(vp); lfp_l.append(lfp); cvp_l.append(cvp); hp_l.append(hp)
        ks_l.append(ks); vs_l.append(vs); lfs_l.append(lfs); cvs_l.append(cvs); hs_l.append(hs)
    return (yp, ys,
            jnp.stack(kp_l), jnp.stack(vp_l), jnp.stack(lfp_l), jnp.stack(cvp_l), jnp.stack(hp_l),
            jnp.stack(ks_l), jnp.stack(vs_l), jnp.stack(lfs_l), jnp.stack(cvs_l), jnp.stack(hs_l))
```

```python
import functools

import jax
import jax.numpy as jnp
from jax import lax
from jax.experimental import pallas as pl
from jax.experimental.pallas import tpu as pltpu

F32 = jnp.float32
BF16 = jnp.bfloat16
I32 = jnp.int32

EPS = 1e-6
LRU_C = 8.0
TOP_K = 4
SWIGLU_LIMIT = 7.0
SWIGLU_ALPHA = 1.702
NEG_BIG = -1e30

LANES = 128
SUBLANES = 8
VMEM_LIMIT = 56 * 1024 * 1024


def _params(n_grid_dims):
    return pltpu.CompilerParams(
        dimension_semantics=("arbitrary",) * n_grid_dims, vmem_limit_bytes=VMEM_LIMIT)


def _dot(a, b):
    return jnp.dot(a, b, preferred_element_type=F32)


def _dot_nt(a, b):
    return lax.dot_general(a, b, (((1,), (1,)), ((), ())), preferred_element_type=F32)


def _split2(x):
    hi = x.astype(BF16)
    lo = (x - hi.astype(F32)).astype(BF16)
    return hi, lo


def _split3(x):
    p1 = x.astype(BF16)
    r1 = x - p1.astype(F32)
    p2 = r1.astype(BF16)
    p3 = (r1 - p2.astype(F32)).astype(BF16)
    return p1, p2, p3


def _dot3(a, b):
    ah, al = _split2(a)
    bh, bl = _split2(b)
    return _dot(ah, bh) + _dot(ah, bl) + _dot(al, bh)


def _dot_exact01(x, u01):
    p1, p2, p3 = _split3(x)
    return _dot(p1, u01) + _dot(p2, u01) + _dot(p3, u01)


def _rms(x, g):
    ms = jnp.mean(x * x, axis=-1, keepdims=True)
    return x * lax.rsqrt(ms + EPS) * g


def _softplus(x):
    return jnp.maximum(x, 0.0) + jnp.log1p(jnp.exp(-jnp.abs(x)))


def _log_sigmoid(x):
    return jnp.minimum(x, 0.0) - jnp.log1p(jnp.exp(-jnp.abs(x)))


def _ada_kernel(c_ref, w_ref, b_ref, o_ref):
    c = c_ref[...]
    o_ref[...] = _dot3(c * jax.nn.sigmoid(c), w_ref[...]) + b_ref[...]


def _ada(c, w, b):
    n, d = c.shape
    nout = w.shape[1]
    tn = min(nout, 1536)
    assert nout % tn == 0
    return pl.pallas_call(
        _ada_kernel,
        grid=(nout // tn,),
        in_specs=[pl.BlockSpec((n, d), lambda j: (0, 0)),
                  pl.BlockSpec((d, tn), lambda j: (0, j)),
                  pl.BlockSpec((1, tn), lambda j: (0, j))],
        out_specs=pl.BlockSpec((n, tn), lambda j: (0, j)),
        out_shape=jax.ShapeDtypeStruct((n, nout), F32),
        compiler_params=_params(1),
        name="ada",
    )(c, w, b.reshape(1, nout))


def _inproj_kernel(x_ref, sc_ref, sh_ref, g_ref, w_ref, wft_ref, bf_ref, z_ref, lft_ref, h_scr):
    j = pl.program_id(1)

    @pl.when(j == 0)
    def _():
        h = _rms(x_ref[...], g_ref[...]) * (1.0 + sc_ref[...]) + sh_ref[...]
        hb = h.astype(BF16)
        h_scr[...] = hb
        lft_ref[...] = _log_sigmoid(_dot_nt(wft_ref[...], hb) + bf_ref[...])

    z_ref[...] = _dot(h_scr[...], w_ref[j])


def _mod_spec(mod, tm, rows_per_batch):
    if mod.ndim == 3:
        tiles_per_batch = rows_per_batch // tm
        return pl.BlockSpec((None, 1, mod.shape[-1]), lambda i, *_: (i // tiles_per_batch, 0, 0))
    return pl.BlockSpec((tm, mod.shape[-1]), lambda i, *_: (i, 0))


def _inproj(x, sc, sh, g, w6, wft, bfo, rows_per_batch):
    t, d = x.shape
    nh = wft.shape[0]
    nz = w6.shape[0]
    tm = min(512, rows_per_batch if sc.ndim == 3 else t)
    assert t % tm == 0 and (sc.ndim == 2 or rows_per_batch % tm == 0)
    return pl.pallas_call(
        _inproj_kernel,
        grid=(t // tm, nz),
        in_specs=[pl.BlockSpec((tm, d), lambda i, j: (i, 0)),
                  _mod_spec(sc, tm, rows_per_batch), _mod_spec(sh, tm, rows_per_batch),
                  pl.BlockSpec((1, d), lambda i, j: (0, 0)),
                  pl.BlockSpec((nz, d, d), lambda i, j: (0, 0, 0), pipeline_mode=pl.Buffered(1)),
                  pl.BlockSpec((nh, d), lambda i, j: (0, 0)),
                  pl.BlockSpec((nh, 1), lambda i, j: (0, 0))],
        out_specs=[pl.BlockSpec((None, tm, d), lambda i, j: (j, i, 0)),
                   pl.BlockSpec((nh, tm), lambda i, j: (0, i))],
        out_shape=[jax.ShapeDtypeStruct((nz, t, d), F32), jax.ShapeDtypeStruct((nh, t), F32)],
        scratch_shapes=[pltpu.VMEM((tm, d), BF16)],
        compiler_params=_params(2),
        name="inproj",
    )(x, sc, sh, g, w6, wft, bfo)


def _cumsum_kernel(lf_ref, o_ref, *, cw, seg):
    n = lf_ref.shape[1]
    r = lax.broadcasted_iota(I32, (cw, cw), 0)
    c = lax.broadcasted_iota(I32, (cw, cw), 1)
    keep = r <= c
    if seg is not None:
        keep = jnp.logical_and(keep, (r // seg) == (c // seg))
    u01 = jnp.where(keep, 1.0, 0.0).astype(BF16)
    carry = jnp.zeros((lf_ref.shape[0], 1), F32)
    for i in range(n // cw):
        f = _dot_exact01(lf_ref[:, i * cw:(i + 1) * cw], u01)
        if seg is None:
            f = f + carry
            carry = f[:, cw - 1:cw]
        o_ref[:, i * cw:(i + 1) * cw] = f


def _cumsum_lanes(lft, block, seg=None):
    nh, t = lft.shape
    cw = min(256, block)
    assert t % block == 0 and block % cw == 0 and (seg is None or cw % seg == 0)
    return pl.pallas_call(
        functools.partial(_cumsum_kernel, cw=cw, seg=seg),
        grid=(t // block,),
        in_specs=[pl.BlockSpec((nh, block), lambda b: (0, b))],
        out_specs=pl.BlockSpec((nh, block), lambda b: (0, b)),
        out_shape=jax.ShapeDtypeStruct((nh, t), F32),
        compiler_params=_params(1),
        name="cumsum_logf",
    )(lft)


def _rglru_coeffs(xc, wbd_ref, ba, bx, lam):
    ng, gw = wbd_ref.shape[0], wbd_ref.shape[1]
    xb = xc.astype(BF16)
    ra, ri = [], []
    for g in range(ng):
        o = _dot(xb[:, g * gw:(g + 1) * gw], wbd_ref[g])
        ra.append(o[:, :gw])
        ri.append(o[:, gw:])
    r = jax.nn.sigmoid((ra[0] if ng == 1 else jnp.concatenate(ra, axis=1)) + ba)
    i = jax.nn.sigmoid((ri[0] if ng == 1 else jnp.concatenate(ri, axis=1)) + bx)
    log_a = -LRU_C * r * _softplus(-lam)
    a = jnp.exp(log_a)
    th = jnp.tanh(log_a)
    mult = jnp.sqrt(-2.0 * th / (1.0 - th))
    return a, mult, i


def _scan8(a8, u8, row):
    for s in (1, 2, 4):
        a_sh = pltpu.roll(a8, s, axis=0)
        u_sh = pltpu.roll(u8, s, axis=0)
        m = row >= s
        u8 = u8 + a8 * jnp.where(m, u_sh, 0.0)
        a8 = a8 * jnp.where(m, a_sh, 1.0)
    return a8, u8


def _rnn_prompt_kernel(x_ref, prev_ref, h0_ref, cw_ref, cb_ref, wbd_ref, ba_ref, bx_ref, lam_ref,
                       y_ref, cs_ref, hl_ref, ext_scr, a_scr, u_scr, h_scr, *, reset_first):
    c = pl.program_id(1)
    tc, d = x_ref.shape
    kw = cw_ref.shape[0]

    @pl.when(c == 0)
    def _():
        ext_scr[0:SUBLANES, :] = prev_ref[...]
        h_scr[...] = h0_ref[...]

    ext_scr[SUBLANES:SUBLANES + tc, :] = x_ref[...]
    w = cw_ref[...]
    xc = cb_ref[...]
    for j in range(kw):
        off = SUBLANES - (kw - 1) + j
        xc = xc + w[j:j + 1, :] * ext_scr[off:off + tc, :]
    ext_scr[0:SUBLANES, :] = ext_scr[tc:tc + SUBLANES, :]

    a, mult, i = _rglru_coeffs(xc, wbd_ref, ba_ref[...], bx_ref[...], lam_ref[...])
    if reset_first:
        row = lax.broadcasted_iota(I32, (tc, 1), 0)
        mult = jnp.where(row == jnp.where(c == 0, 0, -1), 1.0, mult)
    a_scr[...] = a
    u_scr[...] = mult * i * xc

    row8 = lax.broadcasted_iota(I32, (SUBLANES, d), 0)

    def body(g, h):
        r0 = pl.multiple_of(g * SUBLANES, SUBLANES)
        a8, u8 = _scan8(a_scr[pl.ds(r0, SUBLANES), :], u_scr[pl.ds(r0, SUBLANES), :], row8)
        h8 = u8 + a8 * h
        y_ref[pl.ds(r0, SUBLANES), :] = h8
        return h8[SUBLANES - 1:SUBLANES, :]

    h = lax.fori_loop(0, tc // SUBLANES, body, h_scr[...], unroll=4)
    h_scr[...] = h

    @pl.when(c == pl.num_programs(1) - 1)
    def _():
        hl_ref[...] = h
        cs_ref[...] = x_ref[tc - (kw - 1):tc, :]


def _rnn_prompt(x3, prev8, h0, cw, cb, wbd, ba, bx, lam, reset_first):
    b = prev8.shape[0]
    t, d = x3.shape[1], x3.shape[2]
    s = t // b
    kw = cw.shape[0]
    tc = min(256, s)
    nc = s // tc
    assert s % tc == 0 and kw - 1 <= SUBLANES and tc % SUBLANES == 0
    vec = pl.BlockSpec((1, d), lambda bi, ci: (0, 0))
    return pl.pallas_call(
        functools.partial(_rnn_prompt_kernel, reset_first=reset_first),
        grid=(b, nc),
        in_specs=[pl.BlockSpec((None, tc, d), lambda bi, ci: (0, bi * nc + ci, 0)),
                  pl.BlockSpec((None, SUBLANES, d), lambda bi, ci: (bi, 0, 0)),
                  pl.BlockSpec((None, 1, d), lambda bi, ci: (bi, 0, 0)),
                  pl.BlockSpec((kw, d), lambda bi, ci: (0, 0)), vec,
                  pl.BlockSpec(wbd.shape, lambda bi, ci: (0, 0, 0)), vec, vec, vec],
        out_specs=[pl.BlockSpec((tc, d), lambda bi, ci: (bi * nc + ci, 0)),
                   pl.BlockSpec((None, kw - 1, d), lambda bi, ci: (bi, 0, 0)),
                   pl.BlockSpec((None, 1, d), lambda bi, ci: (bi, 0, 0))],
        out_shape=[jax.ShapeDtypeStruct((t, d), F32),
                   jax.ShapeDtypeStruct((b, kw - 1, d), F32),
                   jax.ShapeDtypeStruct((b, 1, d), F32)],
        scratch_shapes=[pltpu.VMEM((tc + SUBLANES, d), F32), pltpu.VMEM((tc, d), F32),
                        pltpu.VMEM((tc, d), F32), pltpu.VMEM((1, d), F32)],
        compiler_params=_params(2),
        name="rnn_prompt",
    )(x3, prev8, h0, cw, cb, wbd, ba, bx, lam)


def _rnn_sample_kernel(x_ref, prev_ref, h0_ref, cw_ref, cb_ref, wbd_ref, ba_ref, bx_ref, lam_ref,
                       y_ref):
    r, d = x_ref.shape
    kw = cw_ref.shape[0]
    x = x_ref[...]
    p = prev_ref[...]
    step = lax.broadcasted_iota(I32, (r, 1), 0) & (SUBLANES - 1)
    w = cw_ref[...]
    xc = cb_ref[...]
    for j in range(kw):
        back = kw - 1 - j
        if back == 0:
            xs = x
        else:
            xs = jnp.where(step >= back, pltpu.roll(x, back, axis=0),
                           pltpu.roll(p, (r - SUBLANES + back) % r, axis=0))
        xc = xc + w[j:j + 1, :] * xs
    a, mult, i = _rglru_coeffs(xc, wbd_ref, ba_ref[...], bx_ref[...], lam_ref[...])
    u = mult * i * xc + a * h0_ref[...]
    for s in (1, 2, 4):
        m = step >= s
        a_sh = pltpu.roll(a, s, axis=0)
        u_sh = pltpu.roll(u, s, axis=0)
        u = u + a * jnp.where(m, u_sh, 0.0)
        a = a * jnp.where(m, a_sh, 1.0)
    y_ref[...] = u


def _rnn_sample(x3, prev8, h08, cw, cb, wbd, ba, bx, lam):
    t, d = x3.shape[1], x3.shape[2]
    nb_total = t // SUBLANES
    kw = cw.shape[0]
    nb = min(16, nb_total)
    r = nb * SUBLANES
    assert nb_total % nb == 0 and nb % SUBLANES == 0
    vec = pl.BlockSpec((1, d), lambda i: (0, 0))
    return pl.pallas_call(
        _rnn_sample_kernel,
        grid=(nb_total // nb,),
        in_specs=[pl.BlockSpec((None, r, d), lambda i: (0, i, 0)),
                  pl.BlockSpec((r, d), lambda i: (i, 0)),
                  pl.BlockSpec((r, d), lambda i: (i, 0)),
                  pl.BlockSpec((kw, d), lambda i: (0, 0)), vec,
                  pl.BlockSpec(wbd.shape, lambda i: (0, 0, 0)), vec, vec, vec],
        out_specs=pl.BlockSpec((r, d), lambda i: (i, 0)),
        out_shape=jax.ShapeDtypeStruct((t, d), F32),
        compiler_params=_params(1),
        name="rnn_sample",
    )(x3, prev8, h08, cw, cb, wbd, ba, bx, lam)


def _softmax_update(s, m, l, acc, v):
    m_new = jnp.maximum(m, jnp.max(s, axis=-1, keepdims=True))
    p = jnp.exp(s - m_new)
    alpha = jnp.exp(m - m_new)
    l_new = alpha * l + jnp.sum(p, axis=-1, keepdims=True)
    acc_new = alpha * acc + _dot(p.astype(BF16), v)
    return m_new, l_new, acc_new


def _fox_prompt_kernel(q_ref, k_ref, v_ref, f_ref, o_ref, *, hd, scale):
    i = pl.program_id(2)
    tq = q_ref.shape[0]
    lane = lax.broadcasted_iota(I32, (1, 2 * hd), 1)
    q = q_ref[...] * scale
    qs = [jnp.where(lane < hd, q, 0.0).astype(BF16), jnp.where(lane >= hd, q, 0.0).astype(BF16)]

    def block(k0, carry, diagonal):
        kb = k_ref[pl.ds(k0, tq), :].astype(BF16)
        vb = v_ref[pl.ds(k0, tq), :].astype(BF16)
        fb = f_ref[:, pl.ds(k0, tq)]
        out = []
        for h in range(2):
            m, l, acc = carry[3 * h:3 * h + 3]
            s = _dot_nt(qs[h], kb) - fb[h:h + 1, :]
            if diagonal:
                r = lax.broadcasted_iota(I32, (tq, tq), 0)
                c = lax.broadcasted_iota(I32, (tq, tq), 1)
                s = jnp.where(c <= r, s, NEG_BIG)
            out.extend(_softmax_update(s, m, l, acc, vb))
        return tuple(out)

    init = []
    for _ in range(2):
        init.extend([jnp.full((tq, 1), NEG_BIG, F32), jnp.zeros((tq, 1), F32), jnp.zeros((tq, 2 * hd), F32)])
    carry = lax.fori_loop(0, i, lambda kb, cr: block(pl.multiple_of(kb * tq, tq), cr, False), tuple(init))
    m0, l0, a0, m1, l1, a1 = block(pl.multiple_of(i * tq, tq), carry, True)
    o_ref[...] = jnp.where(lane < hd, a0 / l0, a1 / l1)


def _fox_prompt(z4, ft3, hd):
    _, b, s, d = z4.shape
    assert 2 * hd == LANES and d % LANES == 0
    npair = d // LANES
    tq = min(256, s)
    assert s % tq == 0
    return pl.pallas_call(
        functools.partial(_fox_prompt_kernel, hd=hd, scale=hd ** -0.5),
        grid=(b, npair, s // tq),
        in_specs=[pl.BlockSpec((None, None, tq, LANES), lambda bi, j, i: (1, bi, i, j)),
                  pl.BlockSpec((None, None, s, LANES), lambda bi, j, i: (2, bi, 0, j)),
                  pl.BlockSpec((None, None, s, LANES), lambda bi, j, i: (3, bi, 0, j)),
                  pl.BlockSpec((None, 2, s), lambda bi, j, i: (j, 0, bi))],
        out_specs=pl.BlockSpec((None, tq, LANES), lambda bi, j, i: (bi, i, j)),
        out_shape=jax.ShapeDtypeStruct((b, s, d), F32),
        compiler_params=_params(3),
        name="fox_prompt",
    )(z4, z4, z4, ft3)


def _fox_sample_kernel(pt_ref, q_ref, kn_ref, vn_ref, cn_ref, ck_ref, cv_ref, clf_ref, o_ref,
                       q_scr, m_scr, l_scr, acc_scr, fcar_scr, *, nh, hd, scale):
    del pt_ref
    j = pl.program_id(1)
    page = clf_ref.shape[0]
    ds_ = q_ref.shape[0]

    @pl.when(j == 0)
    def _():
        q = q_ref[...] * scale
        for h in range(nh):
            q_scr[h] = q[:, h * hd:(h + 1) * hd].astype(BF16)
        m_scr[...] = jnp.full(m_scr.shape, NEG_BIG, F32)
        l_scr[...] = jnp.zeros(l_scr.shape, F32)
        acc_scr[...] = jnp.zeros(acc_scr.shape, F32)
        fcar_scr[...] = jnp.zeros(fcar_scr.shape, F32)

    r = lax.broadcasted_iota(I32, (page, page), 0)
    c = lax.broadcasted_iota(I32, (page, page), 1)
    u01 = jnp.where(r <= c, 1.0, 0.0).astype(BF16)
    p1, p2, p3 = _split3(clf_ref[...])
    tdot = lambda a: lax.dot_general(a, u01, (((0,), (0,)), ((), ())), preferred_element_type=F32)
    ft = tdot(p1) + tdot(p2) + tdot(p3) + fcar_scr[...]
    fcar_scr[...] = ft[:, page - 1:page]

    for h in range(nh):
        kh = ck_ref[pl.ds(h, page, stride=nh), :].astype(BF16)
        vh = cv_ref[pl.ds(h, page, stride=nh), :].astype(BF16)
        s = _dot_nt(q_scr[h], kh) - ft[h:h + 1, :]
        m, l, acc = _softmax_update(s, m_scr[h], l_scr[h], acc_scr[h], vh)
        m_scr[h] = m
        l_scr[h] = l
        acc_scr[h] = acc

    @pl.when(j == pl.num_programs(1) - 1)
    def _():
        fn = cn_ref[...] + fcar_scr[...]
        kn = kn_ref[...]
        vn = vn_ref[...]
        rr = lax.broadcasted_iota(I32, (ds_, ds_), 0)
        cc = lax.broadcasted_iota(I32, (ds_, ds_), 1)
        outs = []
        for h in range(nh):
            kh = kn[:, h * hd:(h + 1) * hd].astype(BF16)
            vh = vn[:, h * hd:(h + 1) * hd].astype(BF16)
            s = _dot_nt(q_scr[h], kh) - fn[h:h + 1, :]
            s = jnp.where(cc <= rr, s, NEG_BIG)
            m, l, acc = _softmax_update(s, m_scr[h], l_scr[h], acc_scr[h], vh)
            outs.append(acc / l)
        o_ref[...] = jnp.concatenate(outs, axis=1)


def _fox_sample(page_table, z3, cnt, ck, cv, clf, nh, hd):
    b, npages = page_table.shape
    t, d = z3.shape[1], z3.shape[2]
    ds_ = t // b
    page = clf.shape[1]
    assert ds_ == SUBLANES and ck.shape[1] == page * nh
    tok = lambda slab: pl.BlockSpec((None, ds_, d), lambda bi, j, pt: (slab, bi, 0))
    pg = lambda shape: pl.BlockSpec((None,) + shape, lambda bi, j, pt: (pt[bi * npages + j], 0, 0))
    grid_spec = pltpu.PrefetchScalarGridSpec(
        num_scalar_prefetch=1,
        grid=(b, npages),
        in_specs=[tok(1), tok(2), tok(3),
                  pl.BlockSpec((None, nh, ds_), lambda bi, j, pt: (bi, 0, 0)),
                  pg((page * nh, hd)), pg((page * nh, hd)), pg((page, nh))],
        out_specs=pl.BlockSpec((ds_, d), lambda bi, j, pt: (bi, 0)),
        scratch_shapes=[pltpu.VMEM((nh, ds_, hd), BF16), pltpu.VMEM((nh, ds_, 1), F32),
                        pltpu.VMEM((nh, ds_, 1), F32), pltpu.VMEM((nh, ds_, hd), F32),
                        pltpu.VMEM((nh, 1), F32)],
    )
    return pl.pallas_call(
        functools.partial(_fox_sample_kernel, nh=nh, hd=hd, scale=hd ** -0.5),
        grid_spec=grid_spec,
        out_shape=jax.ShapeDtypeStruct((t, d), F32),
        compiler_params=_params(2),
        name="fox_sample",
    )(page_table.reshape(-1), z3, z3, z3, cnt, ck, cv, clf)


def _post_kernel(yr_ref, ya_ref, ga_ref, gb_ref, x_ref, gt1_ref, sc2_ref, sh2_ref, wo_ref, gpost_ref,
                 gpre_ref, wr_ref, br_ref, x1_ref, h2_ref, te_ref, gate_ref):
    merged = jax.nn.sigmoid(ga_ref[...]) * yr_ref[...] + jax.nn.sigmoid(gb_ref[...]) * ya_ref[...]
    out = _dot(merged.astype(BF16), wo_ref[...])
    x1 = x_ref[...] + gt1_ref[...] * _rms(out, gpost_ref[...])
    x1_ref[...] = x1
    h2 = _rms(x1, gpre_ref[...]) * (1.0 + sc2_ref[...]) + sh2_ref[...]
    h2_ref[...] = h2
    logits = _dot3(h2, wr_ref[...]) + br_ref[...]
    ne = logits.shape[1]
    lane = lax.broadcasted_iota(I32, logits.shape, 1)
    vals, idxs = [], []
    for _ in range(TOP_K):
        m = jnp.max(logits, axis=1, keepdims=True)
        idx = jnp.min(jnp.where(logits == m, lane, ne), axis=1, keepdims=True)
        vals.append(m)
        idxs.append(idx)
        logits = jnp.where(lane == idx, -jnp.inf, logits)
    e = jnp.exp(jnp.concatenate(vals, axis=1) - vals[0])
    gate_ref[...] = e / jnp.sum(e, axis=1, keepdims=True)
    te_ref[...] = jnp.concatenate(idxs, axis=1)


def _post(y_rnn, y_attn, z3, x, gt1, sc2, sh2, wo, gpost, gpre, wr, br, rows_per_batch):
    t, d = x.shape
    ne = wr.shape[1]
    tm = min(256, rows_per_batch if gt1.ndim == 3 else t)
    assert t % tm == 0
    row = pl.BlockSpec((tm, d), lambda i: (i, 0))
    vec = pl.BlockSpec((1, d), lambda i: (0, 0))
    mod = lambda m: _mod_spec(m, tm, rows_per_batch)
    return pl.pallas_call(
        _post_kernel,
        grid=(t // tm,),
        in_specs=[row, row,
                  pl.BlockSpec((None, tm, d), lambda i: (4, i, 0)),
                  pl.BlockSpec((None, tm, d), lambda i: (5, i, 0)),
                  row, mod(gt1), mod(sc2), mod(sh2),
                  pl.BlockSpec((d, d), lambda i: (0, 0)), vec, vec,
                  pl.BlockSpec((d, ne), lambda i: (0, 0)),
                  pl.BlockSpec((1, ne), lambda i: (0, 0))],
        out_specs=[row, row,
                   pl.BlockSpec((tm, TOP_K), lambda i: (i, 0)),
                   pl.BlockSpec((tm, TOP_K), lambda i: (i, 0))],
        out_shape=[jax.ShapeDtypeStruct((t, d), F32), jax.ShapeDtypeStruct((t, d), F32),
                   jax.ShapeDtypeStruct((t, TOP_K), I32), jax.ShapeDtypeStruct((t, TOP_K), F32)],
        compiler_params=_params(1),
        name="post_mixer",
    )(y_rnn, y_attn, z3, z3, x, gt1, sc2, sh2, wo, gpost, gpre, wr, br)


def _multi_hot(te, ne):
    lane = lax.broadcasted_iota(I32, (te.shape[0], ne), 1)
    hot = jnp.zeros((te.shape[0], ne), F32)
    for k in range(te.shape[1]):
        hot = hot + jnp.where(lane == te[:, k:k + 1], 1.0, 0.0)
    return hot


def _rank_kernel(te_ref, c_ref, cnt_ref, cnt_scr):
    i = pl.program_id(0)
    tr = te_ref.shape[0]

    @pl.when(i == 0)
    def _():
        cnt_scr[...] = jnp.zeros(cnt_scr.shape, F32)

    hot = _multi_hot(te_ref[...], c_ref.shape[1])
    r = lax.broadcasted_iota(I32, (tr, tr), 0)
    c = lax.broadcasted_iota(I32, (tr, tr), 1)
    below = jnp.where(c < r, 1.0, 0.0).astype(BF16)
    c_ref[...] = _dot(below, hot.astype(BF16)) + cnt_scr[...]
    cnt_scr[...] = cnt_scr[...] + jnp.sum(hot, axis=0, keepdims=True)
    cnt_ref[...] = cnt_scr[...]


def _dest_kernel(cnt_ref, c_ref, te_ref, dest_ref, be_ref, nv_ref, *, bm):
    ne = cnt_ref.shape[1]
    nblk = jnp.ceil(cnt_ref[...] * (1.0 / bm))
    r = lax.broadcasted_iota(I32, (ne, ne), 0)
    c = lax.broadcasted_iota(I32, (ne, ne), 1)
    before = jnp.where(r < c, 1.0, 0.0).astype(BF16)
    start_blk = _dot_exact01(jnp.broadcast_to(nblk, (SUBLANES, ne)), before)[0:1, :]
    end_blk = start_blk + nblk

    te = te_ref[...]
    lane = lax.broadcasted_iota(I32, c_ref.shape, 1)
    pos = c_ref[...] + start_blk * float(bm)
    cols = [jnp.sum(jnp.where(lane == te[:, k:k + 1], pos, 0.0), axis=1, keepdims=True) for k in range(te.shape[1])]
    dest_ref[...] = jnp.concatenate(cols, axis=1).astype(I32)

    @pl.when(pl.program_id(0) == 0)
    def _():
        blk = lax.broadcasted_iota(I32, be_ref.shape, 1).astype(F32)
        be = jnp.zeros(be_ref.shape, F32)
        for e in range(ne):
            be = be + jnp.where(blk >= end_blk[:, e:e + 1], 1.0, 0.0)
        be_ref[...] = jnp.minimum(be, float(ne - 1)).astype(I32)
        nv_ref[...] = jnp.broadcast_to(end_blk[:, ne - 1:ne], nv_ref.shape).astype(I32)


def _route(te, ne, bm, nblocks):
    t, k = te.shape
    tr = 512 if t % 512 == 0 else LANES
    assert t % tr == 0
    c, cnt = pl.pallas_call(
        _rank_kernel,
        grid=(t // tr,),
        in_specs=[pl.BlockSpec((tr, k), lambda i: (i, 0))],
        out_specs=[pl.BlockSpec((tr, ne), lambda i: (i, 0)), pl.BlockSpec((1, ne), lambda i: (0, 0))],
        out_shape=[jax.ShapeDtypeStruct((t, ne), F32), jax.ShapeDtypeStruct((1, ne), F32)],
        scratch_shapes=[pltpu.VMEM((1, ne), F32)],
        compiler_params=_params(1),
        name="moe_rank",
    )(te)
    nbp = -(-nblocks // LANES) * LANES
    return pl.pallas_call(
        functools.partial(_dest_kernel, bm=bm),
        grid=(t // tr,),
        in_specs=[pl.BlockSpec((1, ne), lambda i: (0, 0)),
                  pl.BlockSpec((tr, ne), lambda i: (i, 0)),
                  pl.BlockSpec((tr, k), lambda i: (i, 0))],
        out_specs=[pl.BlockSpec((tr, k), lambda i: (i, 0)),
                   pl.BlockSpec((1, nbp), lambda i: (0, 0)),
                   pl.BlockSpec((1, LANES), lambda i: (0, 0))],
        out_shape=[jax.ShapeDtypeStruct((t, k), I32), jax.ShapeDtypeStruct((1, nbp), I32),
                   jax.ShapeDtypeStruct((1, LANES), I32)],
        compiler_params=_params(1),
        name="moe_dest",
    )(cnt, c, te)


def _dispatch_kernel(dest_ref, src_ref, xr_in_ref, xr_ref, sem, *, td):
    del xr_in_ref
    base = pl.program_id(0) * td

    def row_copy(t, k):
        return pltpu.make_async_copy(src_ref.at[pl.ds(base + t, 1)],
                                     xr_ref.at[pl.ds(dest_ref[t * TOP_K + k], 1)], sem)

    def start(t, carry):
        for k in range(TOP_K):
            row_copy(t, k).start()
        return carry

    def wait(t, carry):
        for k in range(TOP_K):
            row_copy(t, k).wait()
        return carry

    lax.fori_loop(0, td, start, 0)
    lax.fori_loop(0, td, wait, 0)


def _dispatch(dest_flat, tok_offset, src, xr):
    t, d = src.shape
    td = LANES
    assert t % td == 0 and tok_offset % td == 0
    off = tok_offset // td
    return pl.pallas_call(
        functools.partial(_dispatch_kernel, td=td),
        grid=(t // td,),
        in_specs=[pl.BlockSpec((td * TOP_K,), lambda i: (i + off,), memory_space=pltpu.SMEM),
                  pl.BlockSpec(memory_space=pl.ANY), pl.BlockSpec(memory_space=pl.ANY)],
        out_specs=pl.BlockSpec(memory_space=pl.ANY),
        out_shape=jax.ShapeDtypeStruct(xr.shape, xr.dtype),
        scratch_shapes=[pltpu.SemaphoreType.DMA(())],
        input_output_aliases={2: 0},
        compiler_params=_params(1),
        name="moe_dispatch",
    )(dest_flat, src, xr)


def _expert_kernel(be_ref, nv_ref, x_ref, wgu_ref, bgu_ref, wd_ref, bd_ref, o_ref, *, fc):
    del be_ref
    f = wd_ref.shape[0]

    @pl.when(pl.program_id(0) < nv_ref[0])
    def _():
        xb = x_ref[...].astype(BF16)
        acc = jnp.zeros(o_ref.shape, F32)
        for c in range(f // fc):
            glu = _dot(xb, wgu_ref[:, c * fc:(c + 1) * fc]) + bgu_ref[:, c * fc:(c + 1) * fc]
            lin = _dot(xb, wgu_ref[:, f + c * fc:f + (c + 1) * fc]) + bgu_ref[:, f + c * fc:f + (c + 1) * fc]
            glu = jnp.minimum(glu, SWIGLU_LIMIT)
            lin = jnp.clip(lin, -SWIGLU_LIMIT, SWIGLU_LIMIT)
            act = glu * jax.nn.sigmoid(SWIGLU_ALPHA * glu) * (lin + 1.0)
            acc = acc + _dot(act.astype(BF16), wd_ref[c * fc:(c + 1) * fc, :])
        o_ref[...] = acc + bd_ref[...]

    @pl.when(pl.program_id(0) >= nv_ref[0])
    def _():
        o_ref[...] = jnp.zeros(o_ref.shape, F32)


def _experts(be, nv, xr, wgu, bgu, wd, bd, bm):
    nr, d = xr.shape
    ne, _, f2 = wgu.shape
    f = f2 // 2
    fc = min(512, f)
    grid_spec = pltpu.PrefetchScalarGridSpec(
        num_scalar_prefetch=2,
        grid=(nr // bm,),
        in_specs=[pl.BlockSpec((bm, d), lambda i, be, nv: (i, 0)),
                  pl.BlockSpec((None, d, f2), lambda i, be, nv: (be[i], 0, 0)),
                  pl.BlockSpec((None, 1, f2), lambda i, be, nv: (be[i], 0, 0)),
                  pl.BlockSpec((None, f, d), lambda i, be, nv: (be[i], 0, 0)),
                  pl.BlockSpec((None, 1, d), lambda i, be, nv: (be[i], 0, 0))],
        out_specs=pl.BlockSpec((bm, d), lambda i, be, nv: (i, 0)),
    )
    return pl.pallas_call(
        functools.partial(_expert_kernel, fc=fc),
        grid_spec=grid_spec,
        out_shape=jax.ShapeDtypeStruct((nr, d), F32),
        compiler_params=_params(1),
        name="moe_experts",
    )(be, nv, xr, wgu, bgu.reshape(ne, 1, f2), wd, bd.reshape(ne, 1, d))


def _combine_kernel(dest_ref, yr_ref, gate_ref, x1_ref, gt2_ref, g_ref, o_ref, buf, sem, *, tc):
    def row_copy(t, k):
        return pltpu.make_async_copy(yr_ref.at[pl.ds(dest_ref[t * TOP_K + k], 1)],
                                     buf.at[k, pl.ds(t, 1)], sem)

    def start(t, carry):
        for k in range(TOP_K):
            row_copy(t, k).start()
        return carry

    def wait(t, carry):
        for k in range(TOP_K):
            row_copy(t, k).wait()
        return carry

    lax.fori_loop(0, tc, start, 0)
    lax.fori_loop(0, tc, wait, 0)
    gate = gate_ref[...]
    f = gate[:, 0:1] * buf[0]
    for k in range(1, TOP_K):
        f = f + gate[:, k:k + 1] * buf[k]
    o_ref[...] = x1_ref[...] + gt2_ref[...] * _rms(f, g_ref[...])


def _combine(dest_flat, tok_offset, yr, gate, x1, gt2, g, rows_per_batch):
    t, d = x1.shape
    tc = LANES
    assert t % tc == 0 and tok_offset % tc == 0
    off = tok_offset // tc
    row = pl.BlockSpec((tc, d), lambda i: (i, 0))
    return pl.pallas_call(
        functools.partial(_combine_kernel, tc=tc),
        grid=(t // tc,),
        in_specs=[pl.BlockSpec((tc * TOP_K,), lambda i: (i + off,), memory_space=pltpu.SMEM),
                  pl.BlockSpec(memory_space=pl.ANY),
                  pl.BlockSpec((tc, TOP_K), lambda i: (i, 0)),
                  row, _mod_spec(gt2, tc, rows_per_batch),
                  pl.BlockSpec((1, d), lambda i: (0, 0))],
        out_specs=row,
        out_shape=jax.ShapeDtypeStruct((t, d), F32),
        scratch_shapes=[pltpu.VMEM((TOP_K, tc, d), F32), pltpu.SemaphoreType.DMA(())],
        compiler_params=_params(1),
        name="moe_combine",
    )(dest_flat, yr, gate, x1, gt2, g)


def _blockdiag_pairs(wa, wx, gw):
    nb, c, _ = wa.shape
    per = gw // c
    ng = nb // per
    eye = jnp.eye(per, dtype=wa.dtype)

    def bd(w):
        return jnp.einsum("gpcd,pq->gpcqd", w.reshape(ng, per, c, c), eye).reshape(ng, gw, gw)

    return jnp.concatenate([bd(wa), bd(wx)], axis=2).astype(BF16)


def _layer(xp, xs, ck, cv, clf, sconv, sh0, page_table, cp, cs, w):
    bp, s, d = xp.shape
    bs, ds_, _ = xs.shape
    nh = w["b_forget"].shape[0]
    hd = d // nh
    tp, ts = bp * s, bs * ds_
    kw = w["conv_w"].shape[0]
    ne = w["w_router"].shape[1]

    w_in = w["w_in"]
    w6 = jnp.concatenate([w_in[:, :4 * d], w_in[:, 4 * d + nh:]], axis=1).astype(BF16)
    w6 = w6.reshape(d, 6, d).transpose(1, 0, 2)
    wft = w_in[:, 4 * d:4 * d + nh].T.astype(BF16)
    bfo = w["b_forget"].reshape(nh, 1)
    gw = min(2 * LANES, d)
    wbd = _blockdiag_pairs(w["rg_w_a"], w["rg_w_x"], gw)
    vec = lambda v: v.reshape(1, d)
    wo = w["w_out"].astype(BF16)
    wgu = w["w_gate_up"].astype(BF16)
    wdn = w["w_down"].astype(BF16)

    ada = _ada(jnp.concatenate([cp, cs], axis=0), w["w_ada"], w["b_ada"])
    mods_p = [m.reshape(bp, 1, d) for m in jnp.split(ada[:bp], 6, axis=-1)]
    mods_s = [jnp.repeat(m, ds_, axis=0) for m in jnp.split(ada[bp:], 6, axis=-1)]

    xpf, xsf = xp.reshape(tp, d), xs.reshape(ts, d)
    zp, lft_p = _inproj(xpf, mods_p[1], mods_p[0], vec(w["g_mix_pre"]), w6, wft, bfo, s)
    zs, lft_s = _inproj(xsf, mods_s[1], mods_s[0], vec(w["g_mix_pre"]), w6, wft, bfo, ds_)

    rnn_w = (w["conv_w"], vec(w["conv_b"]), wbd, vec(w["rg_b_a"]), vec(w["rg_b_x"]), vec(w["rg_lambda"]))
    zero_prev = jnp.zeros((bp, SUBLANES, d), F32)
    zero_h = jnp.zeros((bp, 1, d), F32)
    yr_p, conv_p, hl_p = _rnn_prompt(zp, zero_prev, zero_h, *rnn_w, reset_first=True)
    prev8 = jnp.pad(sconv, ((0, 0), (SUBLANES - (kw - 1), 0), (0, 0))).reshape(ts, d)
    h08 = jnp.pad(sh0[:, None, :], ((0, 0), (0, SUBLANES - 1), (0, 0))).reshape(ts, d)
    yr_s = _rnn_sample(zs, prev8, h08, *rnn_w)
    hl_s = yr_s.reshape(bs, ds_, d)[:, ds_ - 1]
    conv_s = zs[0].reshape(bs, ds_, d)[:, ds_ - (kw - 1):]

    ft_p = _cumsum_lanes(lft_p, s).reshape(nh // 2, 2, tp)
    ya_p = _fox_prompt(zp.reshape(6, bp, s, d), ft_p, hd).reshape(tp, d)
    cn_s = _cumsum_lanes(lft_s, min(ts, 2 * LANES), seg=ds_).reshape(nh, bs, ds_).transpose(1, 0, 2)
    page = ck.shape[1]
    ya_s = _fox_sample(page_table, zs, cn_s, ck.reshape(-1, page * nh, hd), cv.reshape(-1, page * nh, hd),
                       clf, nh, hd)

    post_w = (wo, vec(w["g_mix_post"]), vec(w["g_ffn_pre"]), w["w_router"], w["b_router"].reshape(1, ne))
    x1_p, h2_p, te_p, gate_p = _post(yr_p, ya_p, zp, xpf, mods_p[2], mods_p[4], mods_p[3], *post_w, s)
    x1_s, h2_s, te_s, gate_s = _post(yr_s, ya_s, zs, xsf, mods_s[2], mods_s[4], mods_s[3], *post_w, ds_)

    bm = 256
    t_all = tp + ts
    nblocks = -(-(t_all * TOP_K) // bm) + ne
    dest, be, nv = _route(jnp.concatenate([te_p, te_s], axis=0), ne, bm, nblocks)
    dest_flat = dest.reshape(-1)
    xr = jnp.zeros((nblocks * bm, d), F32)
    xr = _dispatch(dest_flat, 0, h2_p, xr)
    xr = _dispatch(dest_flat, tp, h2_s, xr)
    yr = _experts(be.reshape(-1), nv.reshape(-1), xr, wgu, w["b_gate_up"], wdn, w["b_down"], bm)
    g_post = vec(w["g_ffn_post"])
    y_p = _combine(dest_flat, 0, yr, gate_p, x1_p, mods_p[5], g_post, s)
    y_s = _combine(dest_flat, tp, yr, gate_s, x1_s, mods_s[5], g_post, ds_)

    heads = lambda z, b_, s_: z.reshape(b_, s_, nh, hd)
    out_p = (y_p.reshape(bp, s, d), heads(zp[2], bp, s), heads(zp[3], bp, s),
             lft_p.T.reshape(bp, s, nh), conv_p, hl_p.reshape(bp, d))
    out_s = (y_s.reshape(bs, ds_, d), heads(zs[2], bs, ds_), heads(zs[3], bs, ds_),
             lft_s.T.reshape(bs, ds_, nh), conv_s, hl_s)
    return out_p, out_s


def kernel(x_prompt, x_sample, cache_k, cache_v, cache_logf, state_conv, state_h, page_table, c_prompt, c_sample, w_ada, b_ada, g_mix_pre, g_mix_post, w_in, b_forget, conv_w, conv_b, rg_w_a, rg_b_a, rg_w_x, rg_b_x, rg_lambda, w_out, g_ffn_pre, g_ffn_post, w_router, b_router, w_gate_up, b_gate_up, w_down, b_down):
    weights = dict(w_ada=w_ada, b_ada=b_ada, g_mix_pre=g_mix_pre, g_mix_post=g_mix_post, w_in=w_in,
                   b_forget=b_forget, conv_w=conv_w, conv_b=conv_b, rg_w_a=rg_w_a, rg_b_a=rg_b_a,
                   rg_w_x=rg_w_x, rg_b_x=rg_b_x, rg_lambda=rg_lambda, w_out=w_out, g_ffn_pre=g_ffn_pre,
                   g_ffn_post=g_ffn_post, w_router=w_router, b_router=b_router, w_gate_up=w_gate_up,
                   b_gate_up=b_gate_up, w_down=w_down, b_down=b_down)
    depth = w_ada.shape[0]
    yp, ys = x_prompt, x_sample
    per_layer = []
    for l in range(depth):
        wl = {k: v[l] for k, v in weights.items()}
        out_p, out_s = _layer(yp, ys, cache_k[l], cache_v[l], cache_logf[l], state_conv[l], state_h[l],
                              page_table, c_prompt, c_sample, wl)
        yp, ys = out_p[0], out_s[0]
        per_layer.append(out_p[1:] + out_s[1:])
    stacked = [jnp.stack(leaf) for leaf in zip(*per_layer)]
    return (yp, ys, *stacked)
```

```python
import functools

import jax
import jax.numpy as jnp
from jax import lax
from jax.experimental import pallas as pl
from jax.experimental.pallas import tpu as pltpu

F32 = jnp.float32
BF16 = jnp.bfloat16
I32 = jnp.int32

EPS = 1e-6
LRU_C = 8.0
TOP_K = 4
SWIGLU_LIMIT = 7.0
SWIGLU_ALPHA = 1.702
NEG_BIG = -1e30

LANES = 128
SUBLANES = 8
VMEM_LIMIT = 56 * 1024 * 1024


def _params(n_grid_dims):
    return pltpu.CompilerParams(
        dimension_semantics=("arbitrary",) * n_grid_dims, vmem_limit_bytes=VMEM_LIMIT)


def _dot(a, b):
    return jnp.dot(a, b, preferred_element_type=F32)


def _dot_nt(a, b):
    return lax.dot_general(a, b, (((1,), (1,)), ((), ())), preferred_element_type=F32)


def _split2(x):
    hi = x.astype(BF16)
    lo = (x - hi.astype(F32)).astype(BF16)
    return hi, lo


def _split3(x):
    p1 = x.astype(BF16)
    r1 = x - p1.astype(F32)
    p2 = r1.astype(BF16)
    p3 = (r1 - p2.astype(F32)).astype(BF16)
    return p1, p2, p3


def _dot3(a, b):
    ah, al = _split2(a)
    bh, bl = _split2(b)
    return _dot(ah, bh) + _dot(ah, bl) + _dot(al, bh)


def _dot_exact01(x, u01):
    p1, p2, p3 = _split3(x)
    return _dot(p1, u01) + _dot(p2, u01) + _dot(p3, u01)


def _rms(x, g):
    ms = jnp.mean(x * x, axis=-1, keepdims=True)
    return x * lax.rsqrt(ms + EPS) * g


def _softplus(x):
    return jnp.maximum(x, 0.0) + jnp.log1p(jnp.exp(-jnp.abs(x)))


def _log_sigmoid(x):
    return jnp.minimum(x, 0.0) - jnp.log1p(jnp.exp(-jnp.abs(x)))


def _ada_kernel(c_ref, w_ref, b_ref, o_ref):
    c = c_ref[...]
    o_ref[...] = _dot3(c * jax.nn.sigmoid(c), w_ref[...]) + b_ref[...]


def _ada(c, w, b):
    n, d = c.shape
    nout = w.shape[1]
    tn = min(nout, 1536)
    assert nout % tn == 0
    return pl.pallas_call(
        _ada_kernel,
        grid=(nout // tn,),
        in_specs=[pl.BlockSpec((n, d), lambda j: (0, 0)),
                  pl.BlockSpec((d, tn), lambda j: (0, j)),
                  pl.BlockSpec((1, tn), lambda j: (0, j))],
        out_specs=pl.BlockSpec((n, tn), lambda j: (0, j)),
        out_shape=jax.ShapeDtypeStruct((n, nout), F32),
        compiler_params=_params(1),
        name="ada",
    )(c, w, b.reshape(1, nout))


def _inproj_kernel(x_ref, sc_ref, sh_ref, g_ref, w_ref, wft_ref, bf_ref, *rest, nz, nt):
    if nt:
        wt_ref, z_ref, zt_ref, lft_ref, h_scr = rest
    else:
        z_ref, lft_ref, h_scr = rest
    j = pl.program_id(1)

    @pl.when(j == 0)
    def _():
        h = _rms(x_ref[...], g_ref[...]) * (1.0 + sc_ref[...]) + sh_ref[...]
        hb = h.astype(BF16)
        h_scr[...] = hb
        lft_ref[...] = _log_sigmoid(_dot_nt(wft_ref[...], hb) + bf_ref[...])

    if nt:
        @pl.when(j < nz)
        def _():
            z_ref[...] = _dot(h_scr[...], w_ref[jnp.minimum(j, nz - 1)])

        @pl.when(j >= nz)
        def _():
            zt_ref[...] = _dot_nt(wt_ref[jnp.maximum(j - nz, 0)], h_scr[...])
    else:
        z_ref[...] = _dot(h_scr[...], w_ref[j])


def _mod_spec(mod, tm, rows_per_batch):
    if mod.ndim == 3:
        tiles_per_batch = rows_per_batch // tm
        return pl.BlockSpec((None, 1, mod.shape[-1]), lambda i, *_: (i // tiles_per_batch, 0, 0))
    return pl.BlockSpec((tm, mod.shape[-1]), lambda i, *_: (i, 0))


def _inproj(x, sc, sh, g, w, wt, wft, bfo, nbatch, rows_per_batch):
    t, d = x.shape
    nh = wft.shape[0]
    nz = w.shape[0]
    nt = 0 if wt is None else wt.shape[0]
    s = t // nbatch
    tm = min(512, s)
    assert s % tm == 0 and (sc.ndim == 2 or rows_per_batch % tm == 0)
    tpb = s // tm
    resident = lambda n: pl.BlockSpec((n, d, d), lambda i, j: (0, 0, 0), pipeline_mode=pl.Buffered(1))
    in_specs = [pl.BlockSpec((tm, d), lambda i, j: (i, 0)),
                _mod_spec(sc, tm, rows_per_batch), _mod_spec(sh, tm, rows_per_batch),
                pl.BlockSpec((1, d), lambda i, j: (0, 0)),
                resident(nz),
                pl.BlockSpec((nh, d), lambda i, j: (0, 0)),
                pl.BlockSpec((nh, 1), lambda i, j: (0, 0))]
    out_specs = [pl.BlockSpec((None, tm, d), lambda i, j: (jnp.minimum(j, nz - 1), i, 0))]
    out_shape = [jax.ShapeDtypeStruct((nz, t, d), F32)]
    args = [x, sc, sh, g, w, wft, bfo]
    if nt:
        in_specs.append(resident(nt))
        args.append(wt)
        out_specs.append(pl.BlockSpec((None, None, d, tm),
                                      lambda i, j: (jnp.maximum(j - nz, 0), i // tpb, 0, i % tpb)))
        out_shape.append(jax.ShapeDtypeStruct((nt, nbatch, d, s), F32))
    out_specs.append(pl.BlockSpec((None, nh, tm), lambda i, j: (i // tpb, 0, i % tpb)))
    out_shape.append(jax.ShapeDtypeStruct((nbatch, nh, s), F32))
    return pl.pallas_call(
        functools.partial(_inproj_kernel, nz=nz, nt=nt),
        grid=(t // tm, nz + nt),
        in_specs=in_specs,
        out_specs=out_specs,
        out_shape=out_shape,
        scratch_shapes=[pltpu.VMEM((tm, d), BF16)],
        compiler_params=_params(2),
        name="inproj",
    )(*args)


def _cumsum_kernel(lf_ref, o_ref, *, cw, seg):
    n = lf_ref.shape[1]
    r = lax.broadcasted_iota(I32, (cw, cw), 0)
    c = lax.broadcasted_iota(I32, (cw, cw), 1)
    keep = r <= c
    if seg is not None:
        keep = jnp.logical_and(keep, (r // seg) == (c // seg))
    u01 = jnp.where(keep, 1.0, 0.0).astype(BF16)
    carry = jnp.zeros((lf_ref.shape[0], 1), F32)
    for i in range(n // cw):
        f = _dot_exact01(lf_ref[:, i * cw:(i + 1) * cw], u01)
        if seg is None:
            f = f + carry
            carry = f[:, cw - 1:cw]
        o_ref[:, i * cw:(i + 1) * cw] = f


def _cumsum_lanes(lft, seg=None):
    nb, nh, s = lft.shape
    block = s if seg is None else min(s, 2 * LANES)
    cw = min(2 * LANES, block)
    assert s % block == 0 and block % cw == 0 and (seg is None or cw % seg == 0)
    spec = pl.BlockSpec((None, nh, block), lambda b, n: (b, 0, n))
    return pl.pallas_call(
        functools.partial(_cumsum_kernel, cw=cw, seg=seg),
        grid=(nb, s // block),
        in_specs=[spec],
        out_specs=spec,
        out_shape=jax.ShapeDtypeStruct((nb, nh, s), F32),
        compiler_params=_params(2),
        name="cumsum_logf",
    )(lft)


def _rglru_coeffs(xc, wbd_ref, ba, bx, lam):
    ng, gw = wbd_ref.shape[0], wbd_ref.shape[1]
    xb = xc.astype(BF16)
    ra, ri = [], []
    for g in range(ng):
        o = _dot(xb[:, g * gw:(g + 1) * gw], wbd_ref[g])
        ra.append(o[:, :gw])
        ri.append(o[:, gw:])
    r = jax.nn.sigmoid((ra[0] if ng == 1 else jnp.concatenate(ra, axis=1)) + ba)
    i = jax.nn.sigmoid((ri[0] if ng == 1 else jnp.concatenate(ri, axis=1)) + bx)
    log_a = -LRU_C * r * _softplus(-lam)
    a = jnp.exp(log_a)
    th = jnp.tanh(log_a)
    mult = jnp.sqrt(-2.0 * th / (1.0 - th))
    return a, mult, i


def _scan8(a8, u8, row):
    for s in (1, 2, 4):
        a_sh = pltpu.roll(a8, s, axis=0)
        u_sh = pltpu.roll(u8, s, axis=0)
        m = row >= s
        u8 = u8 + a8 * jnp.where(m, u_sh, 0.0)
        a8 = a8 * jnp.where(m, a_sh, 1.0)
    return a8, u8


def _rnn_prompt_kernel(x_ref, prev_ref, h0_ref, cw_ref, cb_ref, wbd_ref, ba_ref, bx_ref, lam_ref,
                       y_ref, cs_ref, hl_ref, ext_scr, a_scr, u_scr, h_scr, *, reset_first):
    c = pl.program_id(1)
    tc, d = x_ref.shape
    kw = cw_ref.shape[0]

    @pl.when(c == 0)
    def _():
        ext_scr[0:SUBLANES, :] = prev_ref[...]
        h_scr[...] = h0_ref[...]

    ext_scr[SUBLANES:SUBLANES + tc, :] = x_ref[...]
    w = cw_ref[...]
    xc = cb_ref[...]
    for j in range(kw):
        off = SUBLANES - (kw - 1) + j
        xc = xc + w[j:j + 1, :] * ext_scr[off:off + tc, :]
    ext_scr[0:SUBLANES, :] = ext_scr[tc:tc + SUBLANES, :]

    a, mult, i = _rglru_coeffs(xc, wbd_ref, ba_ref[...], bx_ref[...], lam_ref[...])
    if reset_first:
        row = lax.broadcasted_iota(I32, (tc, 1), 0)
        mult = jnp.where(row == jnp.where(c == 0, 0, -1), 1.0, mult)
    a_scr[...] = a
    u_scr[...] = mult * i * xc

    row8 = lax.broadcasted_iota(I32, (SUBLANES, d), 0)

    def body(g, h):
        r0 = pl.multiple_of(g * SUBLANES, SUBLANES)
        a8, u8 = _scan8(a_scr[pl.ds(r0, SUBLANES), :], u_scr[pl.ds(r0, SUBLANES), :], row8)
        h8 = u8 + a8 * h
        y_ref[pl.ds(r0, SUBLANES), :] = h8
        return h8[SUBLANES - 1:SUBLANES, :]

    h = lax.fori_loop(0, tc // SUBLANES, body, h_scr[...], unroll=4)
    h_scr[...] = h

    @pl.when(c == pl.num_programs(1) - 1)
    def _():
        hl_ref[...] = h
        cs_ref[...] = x_ref[tc - (kw - 1):tc, :]


def _rnn_prompt(x3, prev8, h0, cw, cb, wbd, ba, bx, lam, reset_first):
    b = prev8.shape[0]
    t, d = x3.shape[1], x3.shape[2]
    s = t // b
    kw = cw.shape[0]
    tc = min(256, s)
    nc = s // tc
    assert s % tc == 0 and kw - 1 <= SUBLANES and tc % SUBLANES == 0
    vec = pl.BlockSpec((1, d), lambda bi, ci: (0, 0))
    return pl.pallas_call(
        functools.partial(_rnn_prompt_kernel, reset_first=reset_first),
        grid=(b, nc),
        in_specs=[pl.BlockSpec((None, tc, d), lambda bi, ci: (0, bi * nc + ci, 0)),
                  pl.BlockSpec((None, SUBLANES, d), lambda bi, ci: (bi, 0, 0)),
                  pl.BlockSpec((None, 1, d), lambda bi, ci: (bi, 0, 0)),
                  pl.BlockSpec((kw, d), lambda bi, ci: (0, 0)), vec,
                  pl.BlockSpec(wbd.shape, lambda bi, ci: (0, 0, 0)), vec, vec, vec],
        out_specs=[pl.BlockSpec((tc, d), lambda bi, ci: (bi * nc + ci, 0)),
                   pl.BlockSpec((None, kw - 1, d), lambda bi, ci: (bi, 0, 0)),
                   pl.BlockSpec((None, 1, d), lambda bi, ci: (bi, 0, 0))],
        out_shape=[jax.ShapeDtypeStruct((t, d), F32),
                   jax.ShapeDtypeStruct((b, kw - 1, d), F32),
                   jax.ShapeDtypeStruct((b, 1, d), F32)],
        scratch_shapes=[pltpu.VMEM((tc + SUBLANES, d), F32), pltpu.VMEM((tc, d), F32),
                        pltpu.VMEM((tc, d), F32), pltpu.VMEM((1, d), F32)],
        compiler_params=_params(2),
        name="rnn_prompt",
    )(x3, prev8, h0, cw, cb, wbd, ba, bx, lam)


def _rnn_sample_kernel(x_ref, prev_ref, h0_ref, cw_ref, cb_ref, wbd_ref, ba_ref, bx_ref, lam_ref,
                       y_ref):
    r, d = x_ref.shape
    kw = cw_ref.shape[0]
    x = x_ref[...]
    p = prev_ref[...]
    step = lax.broadcasted_iota(I32, (r, 1), 0) & (SUBLANES - 1)
    w = cw_ref[...]
    xc = cb_ref[...]
    for j in range(kw):
        back = kw - 1 - j
        if back == 0:
            xs = x
        else:
            xs = jnp.where(step >= back, pltpu.roll(x, back, axis=0),
                           pltpu.roll(p, (r - SUBLANES + back) % r, axis=0))
        xc = xc + w[j:j + 1, :] * xs
    a, mult, i = _rglru_coeffs(xc, wbd_ref, ba_ref[...], bx_ref[...], lam_ref[...])
    u = mult * i * xc + a * h0_ref[...]
    for s in (1, 2, 4):
        m = step >= s
        a_sh = pltpu.roll(a, s, axis=0)
        u_sh = pltpu.roll(u, s, axis=0)
        u = u + a * jnp.where(m, u_sh, 0.0)
        a = a * jnp.where(m, a_sh, 1.0)
    y_ref[...] = u


def _rnn_sample(x3, prev8, h08, cw, cb, wbd, ba, bx, lam):
    t, d = x3.shape[1], x3.shape[2]
    nb_total = t // SUBLANES
    kw = cw.shape[0]
    nb = min(16, nb_total)
    r = nb * SUBLANES
    assert nb_total % nb == 0 and nb % SUBLANES == 0
    vec = pl.BlockSpec((1, d), lambda i: (0, 0))
    return pl.pallas_call(
        _rnn_sample_kernel,
        grid=(nb_total // nb,),
        in_specs=[pl.BlockSpec((None, r, d), lambda i: (0, i, 0)),
                  pl.BlockSpec((r, d), lambda i: (i, 0)),
                  pl.BlockSpec((r, d), lambda i: (i, 0)),
                  pl.BlockSpec((kw, d), lambda i: (0, 0)), vec,
                  pl.BlockSpec(wbd.shape, lambda i: (0, 0, 0)), vec, vec, vec],
        out_specs=pl.BlockSpec((r, d), lambda i: (i, 0)),
        out_shape=jax.ShapeDtypeStruct((t, d), F32),
        compiler_params=_params(1),
        name="rnn_sample",
    )(x3, prev8, h08, cw, cb, wbd, ba, bx, lam)


def _fox_prompt_kernel(q_ref, kt_ref, vt_ref, f_ref, o_ref, *, hd, scale):
    i = pl.program_id(2)
    tq = q_ref.shape[0]
    lane = lax.broadcasted_iota(I32, (1, 2 * hd), 1)
    q = q_ref[...] * scale
    qs = jnp.concatenate([jnp.where(lane < hd, q, 0.0), jnp.where(lane >= hd, q, 0.0)], axis=0).astype(BF16)

    def block(k0, carry, diagonal):
        m, l, acc = carry
        kt = kt_ref[:, pl.ds(k0, tq)].astype(BF16)
        vt = vt_ref[:, pl.ds(k0, tq)].astype(BF16)
        fb = f_ref[:, pl.ds(k0, tq)]
        s = _dot(qs, kt)
        s = jnp.concatenate([s[:tq] - fb[0:1, :], s[tq:] - fb[1:2, :]], axis=0)
        if diagonal:
            r = lax.broadcasted_iota(I32, (2 * tq, tq), 0) & (tq - 1)
            c = lax.broadcasted_iota(I32, (2 * tq, tq), 1)
            s = jnp.where(c <= r, s, NEG_BIG)
        m_new = jnp.maximum(m, jnp.max(s, axis=-1, keepdims=True))
        p = jnp.exp(s - m_new)
        alpha = jnp.exp(m - m_new)
        l_new = alpha * l + jnp.sum(p, axis=-1, keepdims=True)
        acc_new = alpha * acc + _dot_nt(p.astype(BF16), vt)
        return m_new, l_new, acc_new

    init = (jnp.full((2 * tq, 1), NEG_BIG, F32), jnp.zeros((2 * tq, 1), F32), jnp.zeros((2 * tq, 2 * hd), F32))
    carry = lax.fori_loop(0, i, lambda kb, cr: block(pl.multiple_of(kb * tq, tq), cr, False), init)
    _, l, acc = block(pl.multiple_of(i * tq, tq), carry, True)
    o = acc / l
    o_ref[...] = jnp.where(lane < hd, o[:tq], o[tq:])


def _fox_prompt(z4, zt, ft4, hd):
    _, b, s, d = z4.shape
    assert 2 * hd == LANES and d % LANES == 0
    npair = d // LANES
    tq = min(256, s)
    assert s % tq == 0 and tq & (tq - 1) == 0
    return pl.pallas_call(
        functools.partial(_fox_prompt_kernel, hd=hd, scale=hd ** -0.5),
        grid=(b, npair, s // tq),
        in_specs=[pl.BlockSpec((None, None, tq, LANES), lambda bi, j, i: (1, bi, i, j)),
                  pl.BlockSpec((None, None, LANES, s), lambda bi, j, i: (0, bi, j, 0)),
                  pl.BlockSpec((None, None, LANES, s), lambda bi, j, i: (1, bi, j, 0)),
                  pl.BlockSpec((None, None, 2, s), lambda bi, j, i: (bi, j, 0, 0))],
        out_specs=pl.BlockSpec((None, tq, LANES), lambda bi, j, i: (bi, i, j)),
        out_shape=jax.ShapeDtypeStruct((b, s, d), F32),
        compiler_params=_params(3),
        name="fox_prompt",
    )(z4, zt, zt, ft4)


def _fox_sample_kernel(pt_ref, q_ref, kn_ref, vn_ref, cn_ref, *rest, nh, hd, scale, pps):
    del pt_ref
    k_refs, v_refs, lf_refs = rest[:pps], rest[pps:2 * pps], rest[2 * pps:3 * pps]
    o_ref, qx_scr, m_scr, l_scr, acc_scr, fcar_scr = rest[3 * pps:]
    j = pl.program_id(1)
    ds_, d = q_ref.shape
    page = lf_refs[0].shape[1]
    nr = nh * ds_

    def expand(x):
        return jnp.broadcast_to(x[:, None, :], (nh, ds_, x.shape[1])).reshape(nr, x.shape[1])

    def own_head():
        rh = lax.broadcasted_iota(I32, (nr, d), 0) // ds_
        ch = lax.broadcasted_iota(I32, (nr, d), 1) // hd
        return rh == ch

    @pl.when(j == 0)
    def _():
        q = q_ref[...] * scale
        qt = jnp.broadcast_to(q[None], (nh, ds_, d)).reshape(nr, d)
        qx_scr[...] = jnp.where(own_head(), qt, 0.0).astype(BF16)
        m_scr[...] = jnp.full(m_scr.shape, NEG_BIG, F32)
        l_scr[...] = jnp.zeros(l_scr.shape, F32)
        acc_scr[...] = jnp.zeros(acc_scr.shape, F32)
        fcar_scr[...] = jnp.zeros(fcar_scr.shape, F32)

    def update(s, pv_fn):
        m_old = m_scr[...]
        m_new = jnp.maximum(m_old, jnp.max(s, axis=-1, keepdims=True))
        p = jnp.exp(s - m_new)
        alpha = jnp.exp(m_old - m_new)
        l_scr[...] = alpha * l_scr[...] + jnp.sum(p, axis=-1, keepdims=True)
        m_scr[...] = m_new
        acc_scr[...] = alpha * acc_scr[...] + pv_fn(p.astype(BF16))

    r = lax.broadcasted_iota(I32, (page, page), 0)
    c = lax.broadcasted_iota(I32, (page, page), 1)
    u01 = jnp.where(r <= c, 1.0, 0.0).astype(BF16)
    qx = qx_scr[...]
    fcar = fcar_scr[...]
    s_parts = []
    for pi in range(pps):
        ft = _dot_exact01(lf_refs[pi][...], u01) + fcar
        fcar = ft[:, page - 1:page]
        s_parts.append(_dot(qx, k_refs[pi][...].astype(BF16)) - expand(ft))
    fcar_scr[...] = fcar
    s = s_parts[0] if pps == 1 else jnp.concatenate(s_parts, axis=1)

    def pv_pages(pb):
        out = _dot_nt(pb[:, 0:page], v_refs[0][...].astype(BF16))
        for pi in range(1, pps):
            out = out + _dot_nt(pb[:, pi * page:(pi + 1) * page], v_refs[pi][...].astype(BF16))
        return out

    update(s, pv_pages)

    @pl.when(j == pl.num_programs(1) - 1)
    def _():
        fn = expand(cn_ref[...] + fcar_scr[...])
        sn = _dot_nt(qx_scr[...], kn_ref[...].astype(BF16)) - fn
        qi = lax.broadcasted_iota(I32, (nr, ds_), 0) & (ds_ - 1)
        key = lax.broadcasted_iota(I32, (nr, ds_), 1)
        sn = jnp.where(key <= qi, sn, NEG_BIG)
        update(sn, lambda pb: _dot(pb, vn_ref[...].astype(BF16)))
        o = jnp.where(own_head(), acc_scr[...] / l_scr[...], 0.0)
        o_ref[...] = o.reshape(nh, ds_, d).sum(axis=0)


def _fox_sample(page_table, z3, slabs, cnt, ckt, cvt, clft, nh, hd):
    b, npages = page_table.shape
    t, d = z3.shape[1], z3.shape[2]
    ds_ = t // b
    page = clft.shape[2]
    pps = 4 if npages % 4 == 0 else (2 if npages % 2 == 0 else 1)
    nr = nh * ds_
    assert ds_ == SUBLANES and ckt.shape[1] == d
    tok = lambda slab: pl.BlockSpec((None, ds_, d), lambda bi, j, pt: (slab, bi, 0))

    def pg(rows, pi):
        return pl.BlockSpec((None, rows, page), lambda bi, j, pt: (pt[bi * npages + j * pps + pi], 0, 0))

    in_specs = [tok(slabs[0]), tok(slabs[1]), tok(slabs[2]),
                pl.BlockSpec((None, nh, ds_), lambda bi, j, pt: (bi, 0, 0))]
    in_specs += [pg(d, pi) for pi in range(pps)] * 2 + [pg(nh, pi) for pi in range(pps)]
    grid_spec = pltpu.PrefetchScalarGridSpec(
        num_scalar_prefetch=1,
        grid=(b, npages // pps),
        in_specs=in_specs,
        out_specs=pl.BlockSpec((ds_, d), lambda bi, j, pt: (bi, 0)),
        scratch_shapes=[pltpu.VMEM((nr, d), BF16), pltpu.VMEM((nr, 1), F32), pltpu.VMEM((nr, 1), F32),
                        pltpu.VMEM((nr, d), F32), pltpu.VMEM((nh, 1), F32)],
    )
    return pl.pallas_call(
        functools.partial(_fox_sample_kernel, nh=nh, hd=hd, scale=hd ** -0.5, pps=pps),
        grid_spec=grid_spec,
        out_shape=jax.ShapeDtypeStruct((t, d), F32),
        compiler_params=_params(2),
        name="fox_sample",
    )(page_table.reshape(-1), z3, z3, z3, cnt, *([ckt] * pps), *([cvt] * pps), *([clft] * pps))


def _post_kernel(yr_ref, ya_ref, ga_ref, gb_ref, x_ref, gt1_ref, sc2_ref, sh2_ref, wo_ref, gpost_ref,
                 gpre_ref, wr_ref, br_ref, x1_ref, h2_ref, te_ref, gate_ref):
    merged = jax.nn.sigmoid(ga_ref[...]) * yr_ref[...] + jax.nn.sigmoid(gb_ref[...]) * ya_ref[...]
    out = _dot(merged.astype(BF16), wo_ref[...])
    x1 = x_ref[...] + gt1_ref[...] * _rms(out, gpost_ref[...])
    x1_ref[...] = x1
    h2 = _rms(x1, gpre_ref[...]) * (1.0 + sc2_ref[...]) + sh2_ref[...]
    h2_ref[...] = h2
    logits = _dot3(h2, wr_ref[...]) + br_ref[...]
    ne = logits.shape[1]
    lane = lax.broadcasted_iota(I32, logits.shape, 1)
    vals, idxs = [], []
    for _ in range(TOP_K):
        m = jnp.max(logits, axis=1, keepdims=True)
        idx = jnp.min(jnp.where(logits == m, lane, ne), axis=1, keepdims=True)
        vals.append(m)
        idxs.append(idx)
        logits = jnp.where(lane == idx, -jnp.inf, logits)
    e = jnp.exp(jnp.concatenate(vals, axis=1) - vals[0])
    gate_ref[...] = e / jnp.sum(e, axis=1, keepdims=True)
    te_ref[...] = jnp.concatenate(idxs, axis=1)


def _post(y_rnn, y_attn, z3, x, gt1, sc2, sh2, wo, gpost, gpre, wr, br, rows_per_batch):
    t, d = x.shape
    ne = wr.shape[1]
    tm = min(256, rows_per_batch if gt1.ndim == 3 else t)
    assert t % tm == 0
    row = pl.BlockSpec((tm, d), lambda i: (i, 0))
    vec = pl.BlockSpec((1, d), lambda i: (0, 0))
    mod = lambda m: _mod_spec(m, tm, rows_per_batch)
    return pl.pallas_call(
        _post_kernel,
        grid=(t // tm,),
        in_specs=[row, row,
                  pl.BlockSpec((None, tm, d), lambda i: (2, i, 0)),
                  pl.BlockSpec((None, tm, d), lambda i: (3, i, 0)),
                  row, mod(gt1), mod(sc2), mod(sh2),
                  pl.BlockSpec((d, d), lambda i: (0, 0)), vec, vec,
                  pl.BlockSpec((d, ne), lambda i: (0, 0)),
                  pl.BlockSpec((1, ne), lambda i: (0, 0))],
        out_specs=[row, row,
                   pl.BlockSpec((tm, TOP_K), lambda i: (i, 0)),
                   pl.BlockSpec((tm, TOP_K), lambda i: (i, 0))],
        out_shape=[jax.ShapeDtypeStruct((t, d), F32), jax.ShapeDtypeStruct((t, d), F32),
                   jax.ShapeDtypeStruct((t, TOP_K), I32), jax.ShapeDtypeStruct((t, TOP_K), F32)],
        compiler_params=_params(1),
        name="post_mixer",
    )(y_rnn, y_attn, z3, z3, x, gt1, sc2, sh2, wo, gpost, gpre, wr, br)


def _multi_hot(te, ne):
    lane = lax.broadcasted_iota(I32, (te.shape[0], ne), 1)
    hot = jnp.zeros((te.shape[0], ne), F32)
    for k in range(te.shape[1]):
        hot = hot + jnp.where(lane == te[:, k:k + 1], 1.0, 0.0)
    return hot


def _rank_kernel(te_ref, c_ref, cnt_ref, cnt_scr):
    i = pl.program_id(0)
    tr = te_ref.shape[0]

    @pl.when(i == 0)
    def _():
        cnt_scr[...] = jnp.zeros(cnt_scr.shape, F32)

    hot = _multi_hot(te_ref[...], c_ref.shape[1])
    r = lax.broadcasted_iota(I32, (tr, tr), 0)
    c = lax.broadcasted_iota(I32, (tr, tr), 1)
    below = jnp.where(c < r, 1.0, 0.0).astype(BF16)
    c_ref[...] = _dot(below, hot.astype(BF16)) + cnt_scr[...]
    cnt_scr[...] = cnt_scr[...] + jnp.sum(hot, axis=0, keepdims=True)
    cnt_ref[...] = cnt_scr[...]


def _dest_kernel(cnt_ref, c_ref, te_ref, dest_ref, be_ref, nv_ref, *, bm):
    ne = cnt_ref.shape[1]
    nblk = jnp.ceil(cnt_ref[...] * (1.0 / bm))
    r = lax.broadcasted_iota(I32, (ne, ne), 0)
    c = lax.broadcasted_iota(I32, (ne, ne), 1)
    before = jnp.where(r < c, 1.0, 0.0).astype(BF16)
    start_blk = _dot_exact01(jnp.broadcast_to(nblk, (SUBLANES, ne)), before)[0:1, :]
    end_blk = start_blk + nblk

    te = te_ref[...]
    lane = lax.broadcasted_iota(I32, c_ref.shape, 1)
    pos = c_ref[...] + start_blk * float(bm)
    cols = [jnp.sum(jnp.where(lane == te[:, k:k + 1], pos, 0.0), axis=1, keepdims=True) for k in range(te.shape[1])]
    dest_ref[...] = jnp.concatenate(cols, axis=1).astype(I32)

    @pl.when(pl.program_id(0) == 0)
    def _():
        blk = lax.broadcasted_iota(I32, be_ref.shape, 1).astype(F32)
        be = jnp.zeros(be_ref.shape, F32)
        for e in range(ne):
            be = be + jnp.where(blk >= end_blk[:, e:e + 1], 1.0, 0.0)
        be_ref[...] = jnp.minimum(be, float(ne - 1)).astype(I32)
        nv_ref[...] = jnp.broadcast_to(end_blk[:, ne - 1:ne], nv_ref.shape).astype(I32)


def _route(te, ne, bm, nblocks):
    t, k = te.shape
    tr = 512 if t % 512 == 0 else LANES
    assert t % tr == 0
    c, cnt = pl.pallas_call(
        _rank_kernel,
        grid=(t // tr,),
        in_specs=[pl.BlockSpec((tr, k), lambda i: (i, 0))],
        out_specs=[pl.BlockSpec((tr, ne), lambda i: (i, 0)), pl.BlockSpec((1, ne), lambda i: (0, 0))],
        out_shape=[jax.ShapeDtypeStruct((t, ne), F32), jax.ShapeDtypeStruct((1, ne), F32)],
        scratch_shapes=[pltpu.VMEM((1, ne), F32)],
        compiler_params=_params(1),
        name="moe_rank",
    )(te)
    nbp = -(-nblocks // LANES) * LANES
    return pl.pallas_call(
        functools.partial(_dest_kernel, bm=bm),
        grid=(t // tr,),
        in_specs=[pl.BlockSpec((1, ne), lambda i: (0, 0)),
                  pl.BlockSpec((tr, ne), lambda i: (i, 0)),
                  pl.BlockSpec((tr, k), lambda i: (i, 0))],
        out_specs=[pl.BlockSpec((tr, k), lambda i: (i, 0)),
                   pl.BlockSpec((1, nbp), lambda i: (0, 0)),
                   pl.BlockSpec((1, LANES), lambda i: (0, 0))],
        out_shape=[jax.ShapeDtypeStruct((t, k), I32), jax.ShapeDtypeStruct((1, nbp), I32),
                   jax.ShapeDtypeStruct((1, LANES), I32)],
        compiler_params=_params(1),
        name="moe_dest",
    )(cnt, c, te)


def _dispatch_kernel(dest_ref, src_ref, xr_in_ref, xr_ref, sem, *, td):
    del xr_in_ref

    def row_copy(t, k):
        return pltpu.make_async_copy(src_ref.at[pl.ds(t, 1)],
                                     xr_ref.at[pl.ds(dest_ref[t * TOP_K + k], 1)], sem)

    def start(t, carry):
        for k in range(TOP_K):
            row_copy(t, k).start()
        return carry

    def wait(t, carry):
        for k in range(TOP_K):
            row_copy(t, k).wait()
        return carry

    lax.fori_loop(0, td, start, 0)
    lax.fori_loop(0, td, wait, 0)


def _dispatch(dest_flat, tok_offset, src, xr):
    t, d = src.shape
    td = LANES
    assert t % td == 0 and tok_offset % td == 0
    off = tok_offset // td
    return pl.pallas_call(
        functools.partial(_dispatch_kernel, td=td),
        grid=(t // td,),
        in_specs=[pl.BlockSpec((td * TOP_K,), lambda i: (i + off,), memory_space=pltpu.SMEM),
                  pl.BlockSpec((td, d), lambda i: (i, 0)), pl.BlockSpec(memory_space=pl.ANY)],
        out_specs=pl.BlockSpec(memory_space=pl.ANY),
        out_shape=jax.ShapeDtypeStruct(xr.shape, xr.dtype),
        scratch_shapes=[pltpu.SemaphoreType.DMA(())],
        input_output_aliases={2: 0},
        compiler_params=_params(1),
        name="moe_dispatch",
    )(dest_flat, src, xr)


def _expert_kernel(be_ref, nv_ref, x_ref, wgu_ref, bgu_ref, wd_ref, bd_ref, o_ref, *, fc):
    del be_ref
    f = wd_ref.shape[0]

    @pl.when(pl.program_id(0) < nv_ref[0])
    def _():
        xb = x_ref[...].astype(BF16)
        acc = jnp.zeros(o_ref.shape, F32)
        for c in range(f // fc):
            glu = _dot(xb, wgu_ref[:, c * fc:(c + 1) * fc]) + bgu_ref[:, c * fc:(c + 1) * fc]
            lin = _dot(xb, wgu_ref[:, f + c * fc:f + (c + 1) * fc]) + bgu_ref[:, f + c * fc:f + (c + 1) * fc]
            glu = jnp.minimum(glu, SWIGLU_LIMIT)
            lin = jnp.clip(lin, -SWIGLU_LIMIT, SWIGLU_LIMIT)
            act = glu * jax.nn.sigmoid(SWIGLU_ALPHA * glu) * (lin + 1.0)
            acc = acc + _dot(act.astype(BF16), wd_ref[c * fc:(c + 1) * fc, :])
        o_ref[...] = acc + bd_ref[...]

    @pl.when(pl.program_id(0) >= nv_ref[0])
    def _():
        o_ref[...] = jnp.zeros(o_ref.shape, F32)


def _experts(be, nv, xr, wgu, bgu, wd, bd, bm):
    nr, d = xr.shape
    ne, _, f2 = wgu.shape
    f = f2 // 2
    fc = min(512, f)
    grid_spec = pltpu.PrefetchScalarGridSpec(
        num_scalar_prefetch=2,
        grid=(nr // bm,),
        in_specs=[pl.BlockSpec((bm, d), lambda i, be, nv: (i, 0)),
                  pl.BlockSpec((None, d, f2), lambda i, be, nv: (be[i], 0, 0)),
                  pl.BlockSpec((None, 1, f2), lambda i, be, nv: (be[i], 0, 0)),
                  pl.BlockSpec((None, f, d), lambda i, be, nv: (be[i], 0, 0)),
                  pl.BlockSpec((None, 1, d), lambda i, be, nv: (be[i], 0, 0))],
        out_specs=pl.BlockSpec((bm, d), lambda i, be, nv: (i, 0)),
    )
    return pl.pallas_call(
        functools.partial(_expert_kernel, fc=fc),
        grid_spec=grid_spec,
        out_shape=jax.ShapeDtypeStruct((nr, d), F32),
        compiler_params=_params(1),
        name="moe_experts",
    )(be, nv, xr, wgu, bgu.reshape(ne, 1, f2), wd, bd.reshape(ne, 1, d))


def _combine_kernel(dest_ref, yr_ref, gate_ref, x1_ref, gt2_ref, g_ref, o_ref, buf, sem, *, tc):
    def row_copy(t, k):
        return pltpu.make_async_copy(yr_ref.at[pl.ds(dest_ref[t * TOP_K + k], 1)],
                                     buf.at[k, pl.ds(t, 1)], sem)

    def start(t, carry):
        for k in range(TOP_K):
            row_copy(t, k).start()
        return carry

    def wait(t, carry):
        for k in range(TOP_K):
            row_copy(t, k).wait()
        return carry

    lax.fori_loop(0, tc, start, 0)
    lax.fori_loop(0, tc, wait, 0)
    gate = gate_ref[...]
    f = gate[:, 0:1] * buf[0]
    for k in range(1, TOP_K):
        f = f + gate[:, k:k + 1] * buf[k]
    o_ref[...] = x1_ref[...] + gt2_ref[...] * _rms(f, g_ref[...])


def _combine(dest_flat, tok_offset, yr, gate, x1, gt2, g, rows_per_batch):
    t, d = x1.shape
    tc = LANES
    assert t % tc == 0 and tok_offset % tc == 0
    off = tok_offset // tc
    row = pl.BlockSpec((tc, d), lambda i: (i, 0))
    return pl.pallas_call(
        functools.partial(_combine_kernel, tc=tc),
        grid=(t // tc,),
        in_specs=[pl.BlockSpec((tc * TOP_K,), lambda i: (i + off,), memory_space=pltpu.SMEM),
                  pl.BlockSpec(memory_space=pl.ANY),
                  pl.BlockSpec((tc, TOP_K), lambda i: (i, 0)),
                  row, _mod_spec(gt2, tc, rows_per_batch),
                  pl.BlockSpec((1, d), lambda i: (0, 0))],
        out_specs=row,
        out_shape=jax.ShapeDtypeStruct((t, d), F32),
        scratch_shapes=[pltpu.VMEM((TOP_K, tc, d), F32), pltpu.SemaphoreType.DMA(())],
        compiler_params=_params(1),
        name="moe_combine",
    )(dest_flat, yr, gate, x1, gt2, g)


def _blockdiag_pairs(wa, wx, gw):
    nb, c, _ = wa.shape
    per = gw // c
    ng = nb // per
    eye = jnp.eye(per, dtype=wa.dtype)

    def bd(w):
        return jnp.einsum("gpcd,pq->gpcqd", w.reshape(ng, per, c, c), eye).reshape(ng, gw, gw)

    return jnp.concatenate([bd(wa), bd(wx)], axis=2).astype(BF16)


def _layer(xp, xs, ck, cv, clf, sconv, sh0, page_table, cp, cs, w):
    bp, s, d = xp.shape
    bs, ds_, _ = xs.shape
    nh = w["b_forget"].shape[0]
    hd = d // nh
    tp, ts = bp * s, bs * ds_
    kw = w["conv_w"].shape[0]
    ne = w["w_router"].shape[1]

    w_in = w["w_in"].astype(BF16)
    w_x, w_q, w_k, w_v = (w_in[:, n * d:(n + 1) * d] for n in range(4))
    w_ga, w_gb = w_in[:, 4 * d + nh:5 * d + nh], w_in[:, 5 * d + nh:]
    w4 = jnp.stack([w_x, w_q, w_ga, w_gb])
    wkvt = jnp.stack([w_k.T, w_v.T])
    w6 = jnp.stack([w_x, w_q, w_ga, w_gb, w_k, w_v])
    wft = w_in[:, 4 * d:4 * d + nh].T
    bfo = w["b_forget"].reshape(nh, 1)
    gw = min(2 * LANES, d)
    wbd = _blockdiag_pairs(w["rg_w_a"], w["rg_w_x"], gw)
    vec = lambda v: v.reshape(1, d)
    wo = w["w_out"].astype(BF16)
    wgu = w["w_gate_up"].astype(BF16)
    wdn = w["w_down"].astype(BF16)

    ada = _ada(jnp.concatenate([cp, cs], axis=0), w["w_ada"], w["b_ada"])
    mods_p = [m.reshape(bp, 1, d) for m in jnp.split(ada[:bp], 6, axis=-1)]
    mods_s = [jnp.repeat(m, ds_, axis=0) for m in jnp.split(ada[bp:], 6, axis=-1)]

    xpf, xsf = xp.reshape(tp, d), xs.reshape(ts, d)
    g_pre = vec(w["g_mix_pre"])
    zp, ztp, lft_p = _inproj(xpf, mods_p[1], mods_p[0], g_pre, w4, wkvt, wft, bfo, bp, s)
    zs, lft_s = _inproj(xsf, mods_s[1], mods_s[0], g_pre, w6, None, wft, bfo, 1, ds_)

    rnn_w = (w["conv_w"], vec(w["conv_b"]), wbd, vec(w["rg_b_a"]), vec(w["rg_b_x"]), vec(w["rg_lambda"]))
    zero_prev = jnp.zeros((bp, SUBLANES, d), F32)
    zero_h = jnp.zeros((bp, 1, d), F32)
    yr_p, conv_p, hl_p = _rnn_prompt(zp, zero_prev, zero_h, *rnn_w, reset_first=True)
    prev8 = jnp.pad(sconv, ((0, 0), (SUBLANES - (kw - 1), 0), (0, 0))).reshape(ts, d)
    h08 = jnp.pad(sh0[:, None, :], ((0, 0), (0, SUBLANES - 1), (0, 0))).reshape(ts, d)
    yr_s = _rnn_sample(zs, prev8, h08, *rnn_w)
    hl_s = yr_s.reshape(bs, ds_, d)[:, ds_ - 1]
    conv_s = zs[0].reshape(bs, ds_, d)[:, ds_ - (kw - 1):]

    ft_p = _cumsum_lanes(lft_p).reshape(bp, nh // 2, 2, s)
    ya_p = _fox_prompt(zp.reshape(4, bp, s, d), ztp, ft_p, hd).reshape(tp, d)
    cn_s = _cumsum_lanes(lft_s, seg=ds_).reshape(nh, bs, ds_).transpose(1, 0, 2)
    npool, page = ck.shape[0], ck.shape[1]
    ckt = ck.transpose(0, 2, 3, 1).reshape(npool, d, page)
    cvt = cv.transpose(0, 2, 3, 1).reshape(npool, d, page)
    ya_s = _fox_sample(page_table, zs, (1, 4, 5), cn_s, ckt, cvt, clf.transpose(0, 2, 1), nh, hd)

    post_w = (wo, vec(w["g_mix_post"]), vec(w["g_ffn_pre"]), w["w_router"], w["b_router"].reshape(1, ne))
    x1_p, h2_p, te_p, gate_p = _post(yr_p, ya_p, zp, xpf, mods_p[2], mods_p[4], mods_p[3], *post_w, s)
    x1_s, h2_s, te_s, gate_s = _post(yr_s, ya_s, zs, xsf, mods_s[2], mods_s[4], mods_s[3], *post_w, ds_)

    bm = 256
    t_all = tp + ts
    nblocks = -(-(t_all * TOP_K) // bm) + ne
    dest, be, nv = _route(jnp.concatenate([te_p, te_s], axis=0), ne, bm, nblocks)
    dest_flat = dest.reshape(-1)
    xr = jnp.zeros((nblocks * bm, d), F32)
    xr = _dispatch(dest_flat, 0, h2_p, xr)
    xr = _dispatch(dest_flat, tp, h2_s, xr)
    yr = _experts(be.reshape(-1), nv.reshape(-1), xr, wgu, w["b_gate_up"], wdn, w["b_down"], bm)
    g_post = vec(w["g_ffn_post"])
    y_p = _combine(dest_flat, 0, yr, gate_p, x1_p, mods_p[5], g_post, s)
    y_s = _combine(dest_flat, tp, yr, gate_s, x1_s, mods_s[5], g_post, ds_)

    heads_t = lambda zt_: zt_.reshape(bp, nh, hd, s).transpose(0, 3, 1, 2)
    out_p = (y_p.reshape(bp, s, d), heads_t(ztp[0]), heads_t(ztp[1]),
             lft_p.transpose(0, 2, 1), conv_p, hl_p.reshape(bp, d))
    out_s = (y_s.reshape(bs, ds_, d), zs[4].reshape(bs, ds_, nh, hd), zs[5].reshape(bs, ds_, nh, hd),
             lft_s[0].T.reshape(bs, ds_, nh), conv_s, hl_s)
    return out_p, out_s


def kernel(x_prompt, x_sample, cache_k, cache_v, cache_logf, state_conv, state_h, page_table, c_prompt, c_sample, w_ada, b_ada, g_mix_pre, g_mix_post, w_in, b_forget, conv_w, conv_b, rg_w_a, rg_b_a, rg_w_x, rg_b_x, rg_lambda, w_out, g_ffn_pre, g_ffn_post, w_router, b_router, w_gate_up, b_gate_up, w_down, b_down):
    weights = dict(w_ada=w_ada, b_ada=b_ada, g_mix_pre=g_mix_pre, g_mix_post=g_mix_post, w_in=w_in,
                   b_forget=b_forget, conv_w=conv_w, conv_b=conv_b, rg_w_a=rg_w_a, rg_b_a=rg_b_a,
                   rg_w_x=rg_w_x, rg_b_x=rg_b_x, rg_lambda=rg_lambda, w_out=w_out, g_ffn_pre=g_ffn_pre,
                   g_ffn_post=g_ffn_post, w_router=w_router, b_router=b_router, w_gate_up=w_gate_up,
                   b_gate_up=b_gate_up, w_down=w_down, b_down=b_down)
    depth = w_ada.shape[0]
    yp, ys = x_prompt, x_sample
    per_layer = []
    for l in range(depth):
        wl = {k: v[l] for k, v in weights.items()}
        out_p, out_s = _layer(yp, ys, cache_k[l], cache_v[l], cache_logf[l], state_conv[l], state_h[l],
                              page_table, c_prompt, c_sample, wl)
        yp, ys = out_p[0], out_s[0]
        per_layer.append(out_p[1:] + out_s[1:])
    stacked = [jnp.stack(leaf) for leaf in zip(*per_layer)]
    return (yp, ys, *stacked)
```

```python
import functools

import jax
import jax.numpy as jnp
from jax import lax
from jax.experimental import pallas as pl
from jax.experimental.pallas import tpu as pltpu

F32 = jnp.float32
BF16 = jnp.bfloat16
I32 = jnp.int32

EPS = 1e-6
LRU_C = 8.0
TOP_K = 4
SWIGLU_LIMIT = 7.0
SWIGLU_ALPHA = 1.702
NEG_BIG = -1e30

LANES = 128
SUBLANES = 8
VMEM_LIMIT = 56 * 1024 * 1024


def _params(n_grid_dims):
    return pltpu.CompilerParams(
        dimension_semantics=("arbitrary",) * n_grid_dims, vmem_limit_bytes=VMEM_LIMIT)


def _dot(a, b):
    return jnp.dot(a, b, preferred_element_type=F32)


def _dot_nt(a, b):
    return lax.dot_general(a, b, (((1,), (1,)), ((), ())), preferred_element_type=F32)


def _split2(x):
    hi = x.astype(BF16)
    lo = (x - hi.astype(F32)).astype(BF16)
    return hi, lo


def _split3(x):
    p1 = x.astype(BF16)
    r1 = x - p1.astype(F32)
    p2 = r1.astype(BF16)
    p3 = (r1 - p2.astype(F32)).astype(BF16)
    return p1, p2, p3


def _dot3(a, b):
    ah, al = _split2(a)
    bh, bl = _split2(b)
    return _dot(ah, bh) + _dot(ah, bl) + _dot(al, bh)


def _dot_exact01(x, u01):
    p1, p2, p3 = _split3(x)
    return _dot(p1, u01) + _dot(p2, u01) + _dot(p3, u01)


def _rms(x, g):
    ms = jnp.mean(x * x, axis=-1, keepdims=True)
    return x * lax.rsqrt(ms + EPS) * g


def _softplus(x):
    return jnp.maximum(x, 0.0) + jnp.log1p(jnp.exp(-jnp.abs(x)))


def _log_sigmoid(x):
    return jnp.minimum(x, 0.0) - jnp.log1p(jnp.exp(-jnp.abs(x)))


def _ada_kernel(c_ref, w_ref, b_ref, o_ref):
    c = c_ref[...]
    o_ref[...] = _dot3(c * jax.nn.sigmoid(c), w_ref[...]) + b_ref[...]


def _ada(c, w, b):
    n, d = c.shape
    nout = w.shape[1]
    tn = min(nout, 1536)
    assert nout % tn == 0
    return pl.pallas_call(
        _ada_kernel,
        grid=(nout // tn,),
        in_specs=[pl.BlockSpec((n, d), lambda j: (0, 0)),
                  pl.BlockSpec((d, tn), lambda j: (0, j)),
                  pl.BlockSpec((1, tn), lambda j: (0, j))],
        out_specs=pl.BlockSpec((n, tn), lambda j: (0, j)),
        out_shape=jax.ShapeDtypeStruct((n, nout), F32),
        compiler_params=_params(1),
        name="ada",
    )(c, w, b.reshape(1, nout))


def _inproj_kernel(x_ref, sc_ref, sh_ref, g_ref, w_ref, wft_ref, bf_ref, *rest, nz, nt):
    if nt:
        wt_ref, z_ref = rest[:2]
        zt_refs = rest[2:2 + nt]
    else:
        z_ref = rest[0]
    lft_ref, h_scr = rest[-2:]
    j = pl.program_id(1)

    @pl.when(j == 0)
    def _():
        h = _rms(x_ref[...], g_ref[...]) * (1.0 + sc_ref[...]) + sh_ref[...]
        hb = h.astype(BF16)
        h_scr[...] = hb
        lft_ref[...] = _log_sigmoid(_dot_nt(wft_ref[...], hb) + bf_ref[...])

    if nt:
        @pl.when(j < nz)
        def _():
            z_ref[...] = _dot(h_scr[...], w_ref[jnp.minimum(j, nz - 1)])

        for n in range(nt):
            @pl.when(j == nz + n)
            def _(n=n):
                zt_refs[n][...] = _dot_nt(wt_ref[n], h_scr[...])
    else:
        z_ref[...] = _dot(h_scr[...], w_ref[j])


def _mod_spec(mod, tm, rows_per_batch):
    if mod.ndim == 3:
        tiles_per_batch = rows_per_batch // tm
        return pl.BlockSpec((None, 1, mod.shape[-1]), lambda i, *_: (i // tiles_per_batch, 0, 0))
    return pl.BlockSpec((tm, mod.shape[-1]), lambda i, *_: (i, 0))


def _inproj(x, sc, sh, g, w, wt, wft, bfo, nbatch, rows_per_batch):
    t, d = x.shape
    nh = wft.shape[0]
    nz = w.shape[0]
    nt = 0 if wt is None else wt.shape[0]
    s = t // nbatch
    tm = min(1024, s)
    assert s % tm == 0 and (sc.ndim == 2 or rows_per_batch % tm == 0)
    tpb = s // tm
    resident = lambda n: pl.BlockSpec((n, d, d), lambda i, j: (0, 0, 0), pipeline_mode=pl.Buffered(1))
    in_specs = [pl.BlockSpec((tm, d), lambda i, j: (i, 0)),
                _mod_spec(sc, tm, rows_per_batch), _mod_spec(sh, tm, rows_per_batch),
                pl.BlockSpec((1, d), lambda i, j: (0, 0)),
                resident(nz),
                pl.BlockSpec((nh, d), lambda i, j: (0, 0)),
                pl.BlockSpec((nh, 1), lambda i, j: (0, 0))]
    out_specs = [pl.BlockSpec((None, tm, d), lambda i, j: (jnp.minimum(j, nz - 1), i, 0))]
    out_shape = [jax.ShapeDtypeStruct((nz, t, d), F32)]
    args = [x, sc, sh, g, w, wft, bfo]
    if nt:
        in_specs.append(resident(nt))
        args.append(wt)
        for _ in range(nt):
            out_specs.append(pl.BlockSpec((None, d, tm), lambda i, j: (i // tpb, 0, i % tpb)))
            out_shape.append(jax.ShapeDtypeStruct((nbatch, d, s), F32))
    out_specs.append(pl.BlockSpec((None, nh, tm), lambda i, j: (i // tpb, 0, i % tpb)))
    out_shape.append(jax.ShapeDtypeStruct((nbatch, nh, s), F32))
    return pl.pallas_call(
        functools.partial(_inproj_kernel, nz=nz, nt=nt),
        grid=(t // tm, nz + nt),
        in_specs=in_specs,
        out_specs=out_specs,
        out_shape=out_shape,
        scratch_shapes=[pltpu.VMEM((tm, d), BF16)],
        compiler_params=_params(2),
        name="inproj",
    )(*args)


def _cumsum_kernel(lf_ref, o_ref, *, cw, seg):
    n = lf_ref.shape[1]
    r = lax.broadcasted_iota(I32, (cw, cw), 0)
    c = lax.broadcasted_iota(I32, (cw, cw), 1)
    keep = r <= c
    if seg is not None:
        keep = jnp.logical_and(keep, (r // seg) == (c // seg))
    u01 = jnp.where(keep, 1.0, 0.0).astype(BF16)
    carry = jnp.zeros((lf_ref.shape[0], 1), F32)
    for i in range(n // cw):
        f = _dot_exact01(lf_ref[:, i * cw:(i + 1) * cw], u01)
        if seg is None:
            f = f + carry
            carry = f[:, cw - 1:cw]
        o_ref[:, i * cw:(i + 1) * cw] = f


def _cumsum_lanes(lft, seg=None):
    nb, nh, s = lft.shape
    block = s if seg is None else min(s, 2 * LANES)
    cw = min(2 * LANES, block)
    assert s % block == 0 and block % cw == 0 and (seg is None or cw % seg == 0)
    spec = pl.BlockSpec((None, nh, block), lambda b, n: (b, 0, n))
    return pl.pallas_call(
        functools.partial(_cumsum_kernel, cw=cw, seg=seg),
        grid=(nb, s // block),
        in_specs=[spec],
        out_specs=spec,
        out_shape=jax.ShapeDtypeStruct((nb, nh, s), F32),
        compiler_params=_params(2),
        name="cumsum_logf",
    )(lft)


def _rglru_coeffs(xc, wbd_ref, ba, bx, lam):
    ng, gw = wbd_ref.shape[0], wbd_ref.shape[1]
    xb = xc.astype(BF16)
    ra, ri = [], []
    for g in range(ng):
        o = _dot(xb[:, g * gw:(g + 1) * gw], wbd_ref[g])
        ra.append(o[:, :gw])
        ri.append(o[:, gw:])
    r = jax.nn.sigmoid((ra[0] if ng == 1 else jnp.concatenate(ra, axis=1)) + ba)
    i = jax.nn.sigmoid((ri[0] if ng == 1 else jnp.concatenate(ri, axis=1)) + bx)
    log_a = -LRU_C * r * _softplus(-lam)
    a = jnp.exp(log_a)
    th = jnp.tanh(log_a)
    mult = jnp.sqrt(-2.0 * th / (1.0 - th))
    return a, mult, i


def _scan8(a8, u8, row):
    for s in (1, 2, 4):
        a_sh = pltpu.roll(a8, s, axis=0)
        u_sh = pltpu.roll(u8, s, axis=0)
        m = row >= s
        u8 = u8 + a8 * jnp.where(m, u_sh, 0.0)
        a8 = a8 * jnp.where(m, a_sh, 1.0)
    return a8, u8


def _rnn_prompt_kernel(x_ref, prev_ref, h0_ref, cw_ref, cb_ref, wbd_ref, ba_ref, bx_ref, lam_ref,
                       y_ref, cs_ref, hl_ref, ext_scr, a_scr, u_scr, h_scr, *, reset_first):
    c = pl.program_id(1)
    tc, d = x_ref.shape
    kw = cw_ref.shape[0]

    @pl.when(c == 0)
    def _():
        ext_scr[0:SUBLANES, :] = prev_ref[...]
        h_scr[...] = h0_ref[...]

    ext_scr[SUBLANES:SUBLANES + tc, :] = x_ref[...]
    w = cw_ref[...]
    xc = cb_ref[...]
    for j in range(kw):
        off = SUBLANES - (kw - 1) + j
        xc = xc + w[j:j + 1, :] * ext_scr[off:off + tc, :]
    ext_scr[0:SUBLANES, :] = ext_scr[tc:tc + SUBLANES, :]

    a, mult, i = _rglru_coeffs(xc, wbd_ref, ba_ref[...], bx_ref[...], lam_ref[...])
    if reset_first:
        row = lax.broadcasted_iota(I32, (tc, 1), 0)
        mult = jnp.where(row == jnp.where(c == 0, 0, -1), 1.0, mult)
    a_scr[...] = a
    u_scr[...] = mult * i * xc

    row8 = lax.broadcasted_iota(I32, (SUBLANES, d), 0)

    def body(g, h):
        r0 = pl.multiple_of(g * SUBLANES, SUBLANES)
        a8, u8 = _scan8(a_scr[pl.ds(r0, SUBLANES), :], u_scr[pl.ds(r0, SUBLANES), :], row8)
        h8 = u8 + a8 * h
        y_ref[pl.ds(r0, SUBLANES), :] = h8
        return h8[SUBLANES - 1:SUBLANES, :]

    h = lax.fori_loop(0, tc // SUBLANES, body, h_scr[...], unroll=4)
    h_scr[...] = h

    @pl.when(c == pl.num_programs(1) - 1)
    def _():
        hl_ref[...] = h
        cs_ref[...] = x_ref[tc - (kw - 1):tc, :]


def _rnn_prompt(x3, prev8, h0, cw, cb, wbd, ba, bx, lam, reset_first):
    b = prev8.shape[0]
    t, d = x3.shape[1], x3.shape[2]
    s = t // b
    kw = cw.shape[0]
    tc = min(256, s)
    nc = s // tc
    assert s % tc == 0 and kw - 1 <= SUBLANES and tc % SUBLANES == 0
    vec = pl.BlockSpec((1, d), lambda bi, ci: (0, 0))
    return pl.pallas_call(
        functools.partial(_rnn_prompt_kernel, reset_first=reset_first),
        grid=(b, nc),
        in_specs=[pl.BlockSpec((None, tc, d), lambda bi, ci: (0, bi * nc + ci, 0)),
                  pl.BlockSpec((None, SUBLANES, d), lambda bi, ci: (bi, 0, 0)),
                  pl.BlockSpec((None, 1, d), lambda bi, ci: (bi, 0, 0)),
                  pl.BlockSpec((kw, d), lambda bi, ci: (0, 0)), vec,
                  pl.BlockSpec(wbd.shape, lambda bi, ci: (0, 0, 0)), vec, vec, vec],
        out_specs=[pl.BlockSpec((tc, d), lambda bi, ci: (bi * nc + ci, 0)),
                   pl.BlockSpec((None, kw - 1, d), lambda bi, ci: (bi, 0, 0)),
                   pl.BlockSpec((None, 1, d), lambda bi, ci: (bi, 0, 0))],
        out_shape=[jax.ShapeDtypeStruct((t, d), F32),
                   jax.ShapeDtypeStruct((b, kw - 1, d), F32),
                   jax.ShapeDtypeStruct((b, 1, d), F32)],
        scratch_shapes=[pltpu.VMEM((tc + SUBLANES, d), F32), pltpu.VMEM((tc, d), F32),
                        pltpu.VMEM((tc, d), F32), pltpu.VMEM((1, d), F32)],
        compiler_params=_params(2),
        name="rnn_prompt",
    )(x3, prev8, h0, cw, cb, wbd, ba, bx, lam)


def _rnn_sample_kernel(x_ref, prev_ref, h0_ref, cw_ref, cb_ref, wbd_ref, ba_ref, bx_ref, lam_ref,
                       y_ref):
    r, d = x_ref.shape
    kw = cw_ref.shape[0]
    x = x_ref[...]
    p = prev_ref[...]
    step = lax.broadcasted_iota(I32, (r, 1), 0) & (SUBLANES - 1)
    w = cw_ref[...]
    xc = cb_ref[...]
    for j in range(kw):
        back = kw - 1 - j
        if back == 0:
            xs = x
        else:
            xs = jnp.where(step >= back, pltpu.roll(x, back, axis=0),
                           pltpu.roll(p, (r - SUBLANES + back) % r, axis=0))
        xc = xc + w[j:j + 1, :] * xs
    a, mult, i = _rglru_coeffs(xc, wbd_ref, ba_ref[...], bx_ref[...], lam_ref[...])
    u = mult * i * xc + a * h0_ref[...]
    for s in (1, 2, 4):
        m = step >= s
        a_sh = pltpu.roll(a, s, axis=0)
        u_sh = pltpu.roll(u, s, axis=0)
        u = u + a * jnp.where(m, u_sh, 0.0)
        a = a * jnp.where(m, a_sh, 1.0)
    y_ref[...] = u


def _rnn_sample(x3, prev8, h08, cw, cb, wbd, ba, bx, lam):
    t, d = x3.shape[1], x3.shape[2]
    nb_total = t // SUBLANES
    kw = cw.shape[0]
    nb = min(16, nb_total)
    r = nb * SUBLANES
    assert nb_total % nb == 0 and nb % SUBLANES == 0
    vec = pl.BlockSpec((1, d), lambda i: (0, 0))
    return pl.pallas_call(
        _rnn_sample_kernel,
        grid=(nb_total // nb,),
        in_specs=[pl.BlockSpec((None, r, d), lambda i: (0, i, 0)),
                  pl.BlockSpec((r, d), lambda i: (i, 0)),
                  pl.BlockSpec((r, d), lambda i: (i, 0)),
                  pl.BlockSpec((kw, d), lambda i: (0, 0)), vec,
                  pl.BlockSpec(wbd.shape, lambda i: (0, 0, 0)), vec, vec, vec],
        out_specs=pl.BlockSpec((r, d), lambda i: (i, 0)),
        out_shape=jax.ShapeDtypeStruct((t, d), F32),
        compiler_params=_params(1),
        name="rnn_sample",
    )(x3, prev8, h08, cw, cb, wbd, ba, bx, lam)


def _fox_prompt_kernel(q_ref, kt_ref, vt_ref, f_ref, o_ref, *, hd, scale):
    i = pl.program_id(2)
    tq = q_ref.shape[0]
    lane = lax.broadcasted_iota(I32, (1, 2 * hd), 1)
    q = q_ref[...] * scale
    qs = jnp.concatenate([jnp.where(lane < hd, q, 0.0), jnp.where(lane >= hd, q, 0.0)], axis=0).astype(BF16)

    def scores(kb):
        k0 = pl.multiple_of(kb * tq, tq)
        kt = kt_ref[:, pl.ds(k0, tq)].astype(BF16)
        fb = f_ref[:, pl.ds(k0, tq)]
        s = _dot(qs, kt)
        return jnp.concatenate([s[:tq] - fb[0:1, :], s[tq:] - fb[1:2, :]], axis=0)

    vrow = lax.broadcasted_iota(I32, (2 * hd, 1), 0)

    def update(kb, s, m, acc):
        k0 = pl.multiple_of(kb * tq, tq)
        vt = vt_ref[:, pl.ds(k0, tq)]
        vt0 = jnp.where(vrow < hd, vt, 1.0).astype(BF16)
        vt1 = jnp.where(vrow >= hd, vt, 1.0).astype(BF16)
        m_new = jnp.maximum(m, jnp.max(s, axis=-1, keepdims=True))
        p = jnp.exp(s - m_new).astype(BF16)
        pv = jnp.concatenate([_dot_nt(p[:tq], vt0), _dot_nt(p[tq:], vt1)], axis=0)
        return m_new, jnp.exp(m - m_new) * acc + pv

    def body(kb, carry):
        s, m, acc = carry
        return (scores(kb + 1),) + update(kb, s, m, acc)

    init = (scores(0), jnp.full((2 * tq, 1), NEG_BIG, F32), jnp.zeros((2 * tq, 2 * hd), F32))
    s, m, acc = lax.fori_loop(0, i, body, init)
    r = lax.broadcasted_iota(I32, (2 * tq, tq), 0) & (tq - 1)
    c = lax.broadcasted_iota(I32, (2 * tq, tq), 1)
    _, acc = update(i, jnp.where(c <= r, s, NEG_BIG), m, acc)
    o = acc / pltpu.roll(acc, hd, axis=1)
    o_ref[...] = jnp.where(lane < hd, o[:tq], o[tq:])


def _fox_prompt(z4, kt, vt, ft4, hd):
    _, b, s, d = z4.shape
    assert 2 * hd == LANES and d % LANES == 0
    npair = d // LANES
    tq = min(256, s)
    assert s % tq == 0 and tq & (tq - 1) == 0
    return pl.pallas_call(
        functools.partial(_fox_prompt_kernel, hd=hd, scale=hd ** -0.5),
        grid=(b, npair, s // tq),
        in_specs=[pl.BlockSpec((None, None, tq, LANES), lambda bi, j, i: (1, bi, i, j)),
                  pl.BlockSpec((None, LANES, s), lambda bi, j, i: (bi, j, 0)),
                  pl.BlockSpec((None, LANES, s), lambda bi, j, i: (bi, j, 0)),
                  pl.BlockSpec((None, None, 2, s), lambda bi, j, i: (bi, j, 0, 0))],
        out_specs=pl.BlockSpec((None, tq, LANES), lambda bi, j, i: (bi, i, j)),
        out_shape=jax.ShapeDtypeStruct((b, s, d), F32),
        compiler_params=_params(3),
        name="fox_prompt",
    )(z4, kt, vt, ft4)


def _fox_sample_kernel(pt_ref, q_ref, kn_ref, vn_ref, cn_ref, *rest, nh, hd, scale, pps, grp):
    del pt_ref
    k_refs, v_refs, lf_refs = rest[:pps], rest[pps:2 * pps], rest[2 * pps:3 * pps]
    o_ref, qx_scr, m_scr, l_scr, acc_scr, fcar_scr = rest[3 * pps:]
    j = pl.program_id(1)
    ds_, d = q_ref.shape
    page = lf_refs[0].shape[1]
    nr = nh * ds_

    def expand(x):
        return jnp.broadcast_to(x[:, None, :], (nh, ds_, x.shape[1])).reshape(nr, x.shape[1])

    def own_head():
        rh = lax.broadcasted_iota(I32, (nr, d), 0) // ds_
        ch = lax.broadcasted_iota(I32, (nr, d), 1) // hd
        return rh == ch

    @pl.when(j == 0)
    def _():
        q = q_ref[...] * scale
        qt = jnp.broadcast_to(q[None], (nh, ds_, d)).reshape(nr, d)
        qx_scr[...] = jnp.where(own_head(), qt, 0.0).astype(BF16)
        m_scr[...] = jnp.full(m_scr.shape, NEG_BIG, F32)
        l_scr[...] = jnp.zeros(l_scr.shape, F32)
        acc_scr[...] = jnp.zeros(acc_scr.shape, F32)
        fcar_scr[...] = jnp.zeros(fcar_scr.shape, F32)

    def update(s, pv_fn):
        m_old = m_scr[...]
        m_new = jnp.maximum(m_old, jnp.max(s, axis=-1, keepdims=True))
        p = jnp.exp(s - m_new)
        alpha = jnp.exp(m_old - m_new)
        l_scr[...] = alpha * l_scr[...] + jnp.sum(p, axis=-1, keepdims=True)
        m_scr[...] = m_new
        acc_scr[...] = alpha * acc_scr[...] + pv_fn(p.astype(BF16))

    r = lax.broadcasted_iota(I32, (page, page), 0)
    c = lax.broadcasted_iota(I32, (page, page), 1)
    u01 = jnp.where(r <= c, 1.0, 0.0).astype(BF16)
    qx = qx_scr[...]
    fcar = fcar_scr[...]
    for g0 in range(0, pps, grp):
        pages = range(g0, min(g0 + grp, pps))
        s_parts = []
        for pi in pages:
            ft = _dot_exact01(lf_refs[pi][...], u01)
            s_parts.append(_dot(qx, k_refs[pi][...].astype(BF16)) - expand(ft + fcar))
            fcar = fcar + ft[:, page - 1:page]
        s = s_parts[0] if len(s_parts) == 1 else jnp.concatenate(s_parts, axis=1)

        def pv_pages(pb, pages=pages):
            out = None
            for n, pi in enumerate(pages):
                term = _dot_nt(pb[:, n * page:(n + 1) * page], v_refs[pi][...].astype(BF16))
                out = term if out is None else out + term
            return out

        update(s, pv_pages)
    fcar_scr[...] = fcar

    @pl.when(j == pl.num_programs(1) - 1)
    def _():
        fn = expand(cn_ref[...] + fcar_scr[...])
        sn = _dot_nt(qx_scr[...], kn_ref[...].astype(BF16)) - fn
        qi = lax.broadcasted_iota(I32, (nr, ds_), 0) & (ds_ - 1)
        key = lax.broadcasted_iota(I32, (nr, ds_), 1)
        sn = jnp.where(key <= qi, sn, NEG_BIG)
        update(sn, lambda pb: _dot(pb, vn_ref[...].astype(BF16)))
        o = jnp.where(own_head(), acc_scr[...] / l_scr[...], 0.0)
        o_ref[...] = o.reshape(nh, ds_, d).sum(axis=0)


def _fox_sample(page_table, z3, slabs, cnt, ckt, cvt, clft, nh, hd):
    b, npages = page_table.shape
    t, d = z3.shape[1], z3.shape[2]
    ds_ = t // b
    page = clft.shape[2]
    pps = next(n for n in (8, 4, 2, 1) if npages % n == 0)
    grp = min(4, pps)
    nr = nh * ds_
    assert ds_ == SUBLANES and ckt.shape[1] == d
    tok = lambda slab: pl.BlockSpec((None, ds_, d), lambda bi, j, pt: (slab, bi, 0))

    def pg(rows, pi):
        return pl.BlockSpec((None, rows, page), lambda bi, j, pt: (pt[bi * npages + j * pps + pi], 0, 0))

    in_specs = [tok(slabs[0]), tok(slabs[1]), tok(slabs[2]),
                pl.BlockSpec((None, nh, ds_), lambda bi, j, pt: (bi, 0, 0))]
    in_specs += [pg(d, pi) for pi in range(pps)] * 2 + [pg(nh, pi) for pi in range(pps)]
    grid_spec = pltpu.PrefetchScalarGridSpec(
        num_scalar_prefetch=1,
        grid=(b, npages // pps),
        in_specs=in_specs,
        out_specs=pl.BlockSpec((ds_, d), lambda bi, j, pt: (bi, 0)),
        scratch_shapes=[pltpu.VMEM((nr, d), BF16), pltpu.VMEM((nr, 1), F32), pltpu.VMEM((nr, 1), F32),
                        pltpu.VMEM((nr, d), F32), pltpu.VMEM((nh, 1), F32)],
    )
    return pl.pallas_call(
        functools.partial(_fox_sample_kernel, nh=nh, hd=hd, scale=hd ** -0.5, pps=pps, grp=grp),
        grid_spec=grid_spec,
        out_shape=jax.ShapeDtypeStruct((t, d), F32),
        compiler_params=_params(2),
        name="fox_sample",
    )(page_table.reshape(-1), z3, z3, z3, cnt, *([ckt] * pps), *([cvt] * pps), *([clft] * pps))


def _post_kernel(yr_ref, ya_ref, ga_ref, gb_ref, x_ref, gt1_ref, sc2_ref, sh2_ref, wo_ref, gpost_ref,
                 gpre_ref, wr_ref, br_ref, x1_ref, h2_ref, te_ref, gate_ref):
    merged = jax.nn.sigmoid(ga_ref[...]) * yr_ref[...] + jax.nn.sigmoid(gb_ref[...]) * ya_ref[...]
    out = _dot(merged.astype(BF16), wo_ref[...])
    x1 = x_ref[...] + gt1_ref[...] * _rms(out, gpost_ref[...])
    x1_ref[...] = x1
    h2 = _rms(x1, gpre_ref[...]) * (1.0 + sc2_ref[...]) + sh2_ref[...]
    h2_ref[...] = h2
    logits = _dot3(h2, wr_ref[...]) + br_ref[...]
    ne = logits.shape[1]
    lane = lax.broadcasted_iota(I32, logits.shape, 1)
    vals, idxs = [], []
    for _ in range(TOP_K):
        m = jnp.max(logits, axis=1, keepdims=True)
        idx = jnp.min(jnp.where(logits == m, lane, ne), axis=1, keepdims=True)
        vals.append(m)
        idxs.append(idx)
        logits = jnp.where(lane == idx, -jnp.inf, logits)
    e = jnp.exp(jnp.concatenate(vals, axis=1) - vals[0])
    gate_ref[...] = e / jnp.sum(e, axis=1, keepdims=True)
    te_ref[...] = jnp.concatenate(idxs, axis=1)


def _post(y_rnn, y_attn, z3, x, gt1, sc2, sh2, wo, gpost, gpre, wr, br, rows_per_batch):
    t, d = x.shape
    ne = wr.shape[1]
    tm = min(256, rows_per_batch if gt1.ndim == 3 else t)
    assert t % tm == 0
    row = pl.BlockSpec((tm, d), lambda i: (i, 0))
    vec = pl.BlockSpec((1, d), lambda i: (0, 0))
    mod = lambda m: _mod_spec(m, tm, rows_per_batch)
    return pl.pallas_call(
        _post_kernel,
        grid=(t // tm,),
        in_specs=[row, row,
                  pl.BlockSpec((None, tm, d), lambda i: (2, i, 0)),
                  pl.BlockSpec((None, tm, d), lambda i: (3, i, 0)),
                  row, mod(gt1), mod(sc2), mod(sh2),
                  pl.BlockSpec((d, d), lambda i: (0, 0)), vec, vec,
                  pl.BlockSpec((d, ne), lambda i: (0, 0)),
                  pl.BlockSpec((1, ne), lambda i: (0, 0))],
        out_specs=[row, row,
                   pl.BlockSpec((tm, TOP_K), lambda i: (i, 0)),
                   pl.BlockSpec((tm, TOP_K), lambda i: (i, 0))],
        out_shape=[jax.ShapeDtypeStruct((t, d), F32), jax.ShapeDtypeStruct((t, d), F32),
                   jax.ShapeDtypeStruct((t, TOP_K), I32), jax.ShapeDtypeStruct((t, TOP_K), F32)],
        compiler_params=_params(1),
        name="post_mixer",
    )(y_rnn, y_attn, z3, z3, x, gt1, sc2, sh2, wo, gpost, gpre, wr, br)


def _multi_hot(te, ne):
    lane = lax.broadcasted_iota(I32, (te.shape[0], ne), 1)
    hot = jnp.zeros((te.shape[0], ne), F32)
    for k in range(te.shape[1]):
        hot = hot + jnp.where(lane == te[:, k:k + 1], 1.0, 0.0)
    return hot


def _rank_kernel(te_ref, c_ref, cnt_ref, cnt_scr):
    i = pl.program_id(0)
    tr = te_ref.shape[0]

    @pl.when(i == 0)
    def _():
        cnt_scr[...] = jnp.zeros(cnt_scr.shape, F32)

    hot = _multi_hot(te_ref[...], c_ref.shape[1])
    r = lax.broadcasted_iota(I32, (tr, tr), 0)
    c = lax.broadcasted_iota(I32, (tr, tr), 1)
    below = jnp.where(c < r, 1.0, 0.0).astype(BF16)
    c_ref[...] = _dot(below, hot.astype(BF16)) + cnt_scr[...]
    cnt_scr[...] = cnt_scr[...] + jnp.sum(hot, axis=0, keepdims=True)
    cnt_ref[...] = cnt_scr[...]


def _dest_kernel(cnt_ref, c_ref, te_ref, dest_ref, be_ref, nv_ref, *, bm):
    ne = cnt_ref.shape[1]
    nblk = jnp.ceil(cnt_ref[...] * (1.0 / bm))
    r = lax.broadcasted_iota(I32, (ne, ne), 0)
    c = lax.broadcasted_iota(I32, (ne, ne), 1)
    before = jnp.where(r < c, 1.0, 0.0).astype(BF16)
    start_blk = _dot_exact01(jnp.broadcast_to(nblk, (SUBLANES, ne)), before)[0:1, :]
    end_blk = start_blk + nblk

    te = te_ref[...]
    lane = lax.broadcasted_iota(I32, c_ref.shape, 1)
    pos = c_ref[...] + start_blk * float(bm)
    cols = [jnp.sum(jnp.where(lane == te[:, k:k + 1], pos, 0.0), axis=1, keepdims=True) for k in range(te.shape[1])]
    dest_ref[...] = jnp.concatenate(cols, axis=1).astype(I32)

    @pl.when(pl.program_id(0) == 0)
    def _():
        blk = lax.broadcasted_iota(I32, be_ref.shape, 1).astype(F32)
        be = jnp.zeros(be_ref.shape, F32)
        for e in range(ne):
            be = be + jnp.where(blk >= end_blk[:, e:e + 1], 1.0, 0.0)
        be_ref[...] = jnp.minimum(be, float(ne - 1)).astype(I32)
        nv_ref[...] = jnp.broadcast_to(end_blk[:, ne - 1:ne], nv_ref.shape).astype(I32)


def _route(te, ne, bm, nblocks):
    t, k = te.shape
    tr = 512 if t % 512 == 0 else LANES
    assert t % tr == 0
    c, cnt = pl.pallas_call(
        _rank_kernel,
        grid=(t // tr,),
        in_specs=[pl.BlockSpec((tr, k), lambda i: (i, 0))],
        out_specs=[pl.BlockSpec((tr, ne), lambda i: (i, 0)), pl.BlockSpec((1, ne), lambda i: (0, 0))],
        out_shape=[jax.ShapeDtypeStruct((t, ne), F32), jax.ShapeDtypeStruct((1, ne), F32)],
        scratch_shapes=[pltpu.VMEM((1, ne), F32)],
        compiler_params=_params(1),
        name="moe_rank",
    )(te)
    nbp = -(-nblocks // LANES) * LANES
    return pl.pallas_call(
        functools.partial(_dest_kernel, bm=bm),
        grid=(t // tr,),
        in_specs=[pl.BlockSpec((1, ne), lambda i: (0, 0)),
                  pl.BlockSpec((tr, ne), lambda i: (i, 0)),
                  pl.BlockSpec((tr, k), lambda i: (i, 0))],
        out_specs=[pl.BlockSpec((tr, k), lambda i: (i, 0)),
                   pl.BlockSpec((1, nbp), lambda i: (0, 0)),
                   pl.BlockSpec((1, LANES), lambda i: (0, 0))],
        out_shape=[jax.ShapeDtypeStruct((t, k), I32), jax.ShapeDtypeStruct((1, nbp), I32),
                   jax.ShapeDtypeStruct((1, LANES), I32)],
        compiler_params=_params(1),
        name="moe_dest",
    )(cnt, c, te)


def _dispatch_kernel(dest_ref, src_ref, xr_in_ref, xr_ref, sem, *, td):
    del xr_in_ref

    def row_copy(t, k):
        return pltpu.make_async_copy(src_ref.at[pl.ds(t, 1)],
                                     xr_ref.at[pl.ds(dest_ref[t * TOP_K + k], 1)], sem)

    def start(t, carry):
        for k in range(TOP_K):
            row_copy(t, k).start(priority=k % 2)
        return carry

    def wait(t, carry):
        for k in range(TOP_K):
            row_copy(t, k).wait()
        return carry

    lax.fori_loop(0, td, start, 0)
    lax.fori_loop(0, td, wait, 0)


def _dispatch(dest_flat, tok_offset, src, xr):
    t, d = src.shape
    td = LANES
    assert t % td == 0 and tok_offset % td == 0
    off = tok_offset // td
    return pl.pallas_call(
        functools.partial(_dispatch_kernel, td=td),
        grid=(t // td,),
        in_specs=[pl.BlockSpec((td * TOP_K,), lambda i: (i + off,), memory_space=pltpu.SMEM),
                  pl.BlockSpec((td, d), lambda i: (i, 0)), pl.BlockSpec(memory_space=pl.ANY)],
        out_specs=pl.BlockSpec(memory_space=pl.ANY),
        out_shape=jax.ShapeDtypeStruct(xr.shape, xr.dtype),
        scratch_shapes=[pltpu.SemaphoreType.DMA(())],
        input_output_aliases={2: 0},
        compiler_params=_params(1),
        name="moe_dispatch",
    )(dest_flat, src, xr)


def _expert_kernel(be_ref, nv_ref, x_ref, wgu_ref, bgu_ref, wd_ref, bd_ref, o_ref, wgu_bf, wd_bf, *, fc):
    i = pl.program_id(0)
    f = wd_ref.shape[0]

    @pl.when(jnp.logical_or(i == 0, be_ref[i] != be_ref[jnp.maximum(i - 1, 0)]))
    def _():
        for c in range(2 * f // fc):
            wgu_bf[:, c * fc:(c + 1) * fc] = wgu_ref[:, c * fc:(c + 1) * fc].astype(BF16)
        for c in range(f // fc):
            wd_bf[c * fc:(c + 1) * fc, :] = wd_ref[c * fc:(c + 1) * fc, :].astype(BF16)

    @pl.when(i < nv_ref[0])
    def _():
        xb = x_ref[...].astype(BF16)
        acc = jnp.zeros(o_ref.shape, F32)
        for c in range(f // fc):
            glu = _dot(xb, wgu_bf[:, c * fc:(c + 1) * fc]) + bgu_ref[:, c * fc:(c + 1) * fc]
            lin = _dot(xb, wgu_bf[:, f + c * fc:f + (c + 1) * fc]) + bgu_ref[:, f + c * fc:f + (c + 1) * fc]
            glu = jnp.minimum(glu, SWIGLU_LIMIT)
            lin = jnp.clip(lin, -SWIGLU_LIMIT, SWIGLU_LIMIT)
            act = glu * jax.nn.sigmoid(SWIGLU_ALPHA * glu) * (lin + 1.0)
            acc = acc + _dot(act.astype(BF16), wd_bf[c * fc:(c + 1) * fc, :])
        o_ref[...] = acc + bd_ref[...]

    @pl.when(i >= nv_ref[0])
    def _():
        o_ref[...] = jnp.zeros(o_ref.shape, F32)


def _experts(be, nv, xr, wgu, bgu, wd, bd, bm):
    nr, d = xr.shape
    ne, _, f2 = wgu.shape
    f = f2 // 2
    fc = min(512, f)
    grid_spec = pltpu.PrefetchScalarGridSpec(
        num_scalar_prefetch=2,
        grid=(nr // bm,),
        in_specs=[pl.BlockSpec((bm, d), lambda i, be, nv: (i, 0)),
                  pl.BlockSpec((None, d, f2), lambda i, be, nv: (be[i], 0, 0)),
                  pl.BlockSpec((None, 1, f2), lambda i, be, nv: (be[i], 0, 0)),
                  pl.BlockSpec((None, f, d), lambda i, be, nv: (be[i], 0, 0)),
                  pl.BlockSpec((None, 1, d), lambda i, be, nv: (be[i], 0, 0))],
        out_specs=pl.BlockSpec((bm, d), lambda i, be, nv: (i, 0)),
        scratch_shapes=[pltpu.VMEM((d, f2), BF16), pltpu.VMEM((f, d), BF16)],
    )
    return pl.pallas_call(
        functools.partial(_expert_kernel, fc=fc),
        grid_spec=grid_spec,
        out_shape=jax.ShapeDtypeStruct((nr, d), F32),
        compiler_params=_params(1),
        name="moe_experts",
    )(be, nv, xr, wgu, bgu.reshape(ne, 1, f2), wd, bd.reshape(ne, 1, d))


def _combine_kernel(dest_ref, yr_ref, gate_ref, x1_ref, gt2_ref, g_ref, o_ref, buf, sem, *, tc):
    def row_copy(t, k):
        return pltpu.make_async_copy(yr_ref.at[pl.ds(dest_ref[t * TOP_K + k], 1)],
                                     buf.at[k, pl.ds(t, 1)], sem)

    def start(t, carry):
        for k in range(TOP_K):
            row_copy(t, k).start(priority=k % 2)
        return carry

    def wait(t, carry):
        for k in range(TOP_K):
            row_copy(t, k).wait()
        return carry

    lax.fori_loop(0, tc, start, 0)
    lax.fori_loop(0, tc, wait, 0)
    gate = gate_ref[...]
    f = gate[:, 0:1] * buf[0]
    for k in range(1, TOP_K):
        f = f + gate[:, k:k + 1] * buf[k]
    o_ref[...] = x1_ref[...] + gt2_ref[...] * _rms(f, g_ref[...])


def _combine(dest_flat, tok_offset, yr, gate, x1, gt2, g, rows_per_batch):
    t, d = x1.shape
    tc = LANES
    assert t % tc == 0 and tok_offset % tc == 0
    off = tok_offset // tc
    row = pl.BlockSpec((tc, d), lambda i: (i, 0))
    return pl.pallas_call(
        functools.partial(_combine_kernel, tc=tc),
        grid=(t // tc,),
        in_specs=[pl.BlockSpec((tc * TOP_K,), lambda i: (i + off,), memory_space=pltpu.SMEM),
                  pl.BlockSpec(memory_space=pl.ANY),
                  pl.BlockSpec((tc, TOP_K), lambda i: (i, 0)),
                  row, _mod_spec(gt2, tc, rows_per_batch),
                  pl.BlockSpec((1, d), lambda i: (0, 0))],
        out_specs=row,
        out_shape=jax.ShapeDtypeStruct((t, d), F32),
        scratch_shapes=[pltpu.VMEM((TOP_K, tc, d), F32), pltpu.SemaphoreType.DMA(())],
        compiler_params=_params(1),
        name="moe_combine",
    )(dest_flat, yr, gate, x1, gt2, g)


def _blockdiag_pairs(wa, wx, gw):
    nb, c, _ = wa.shape
    per = gw // c
    ng = nb // per
    eye = jnp.eye(per, dtype=wa.dtype)

    def bd(w):
        return jnp.einsum("gpcd,pq->gpcqd", w.reshape(ng, per, c, c), eye).reshape(ng, gw, gw)

    return jnp.concatenate([bd(wa), bd(wx)], axis=2).astype(BF16)


def _layer(xp, xs, ck, cv, clf, sconv, sh0, page_table, cp, cs, w):
    bp, s, d = xp.shape
    bs, ds_, _ = xs.shape
    nh = w["b_forget"].shape[0]
    hd = d // nh
    tp, ts = bp * s, bs * ds_
    kw = w["conv_w"].shape[0]
    ne = w["w_router"].shape[1]

    w_in = w["w_in"].astype(BF16)
    w_x, w_q, w_k, w_v = (w_in[:, n * d:(n + 1) * d] for n in range(4))
    w_ga, w_gb = w_in[:, 4 * d + nh:5 * d + nh], w_in[:, 5 * d + nh:]
    w4 = jnp.stack([w_x, w_q, w_ga, w_gb])
    wkvt = jnp.stack([w_k.T, w_v.T])
    w6 = jnp.stack([w_x, w_q, w_ga, w_gb, w_k, w_v])
    wft = w_in[:, 4 * d:4 * d + nh].T
    bfo = w["b_forget"].reshape(nh, 1)
    gw = min(2 * LANES, d)
    wbd = _blockdiag_pairs(w["rg_w_a"], w["rg_w_x"], gw)
    vec = lambda v: v.reshape(1, d)
    wo = w["w_out"].astype(BF16)
    wgu, wdn = w["w_gate_up"], w["w_down"]

    ada = _ada(jnp.concatenate([cp, cs], axis=0), w["w_ada"], w["b_ada"])
    mods_p = [m.reshape(bp, 1, d) for m in jnp.split(ada[:bp], 6, axis=-1)]
    mods_s = [jnp.repeat(m, ds_, axis=0) for m in jnp.split(ada[bp:], 6, axis=-1)]

    xpf, xsf = xp.reshape(tp, d), xs.reshape(ts, d)
    g_pre = vec(w["g_mix_pre"])
    zp, kt_p, vt_p, lft_p = _inproj(xpf, mods_p[1], mods_p[0], g_pre, w4, wkvt, wft, bfo, bp, s)
    zs, lft_s = _inproj(xsf, mods_s[1], mods_s[0], g_pre, w6, None, wft, bfo, 1, ds_)

    rnn_w = (w["conv_w"], vec(w["conv_b"]), wbd, vec(w["rg_b_a"]), vec(w["rg_b_x"]), vec(w["rg_lambda"]))
    zero_prev = jnp.zeros((bp, SUBLANES, d), F32)
    zero_h = jnp.zeros((bp, 1, d), F32)
    yr_p, conv_p, hl_p = _rnn_prompt(zp, zero_prev, zero_h, *rnn_w, reset_first=True)
    prev8 = jnp.pad(sconv, ((0, 0), (SUBLANES - (kw - 1), 0), (0, 0))).reshape(ts, d)
    h08 = jnp.pad(sh0[:, None, :], ((0, 0), (0, SUBLANES - 1), (0, 0))).reshape(ts, d)
    yr_s = _rnn_sample(zs, prev8, h08, *rnn_w)
    hl_s = yr_s.reshape(bs, ds_, d)[:, ds_ - 1]
    conv_s = zs[0].reshape(bs, ds_, d)[:, ds_ - (kw - 1):]

    ft_p = _cumsum_lanes(lft_p).reshape(bp, nh // 2, 2, s)
    ya_p = _fox_prompt(zp.reshape(4, bp, s, d), kt_p, vt_p, ft_p, hd).reshape(tp, d)
    cn_s = _cumsum_lanes(lft_s, seg=ds_).reshape(nh, bs, ds_).transpose(1, 0, 2)
    npool, page = ck.shape[0], ck.shape[1]
    ckt = ck.transpose(0, 2, 3, 1).reshape(npool, d, page)
    cvt = cv.transpose(0, 2, 3, 1).reshape(npool, d, page)
    ya_s = _fox_sample(page_table, zs, (1, 4, 5), cn_s, ckt, cvt, clf.transpose(0, 2, 1), nh, hd)

    post_w = (wo, vec(w["g_mix_post"]), vec(w["g_ffn_pre"]), w["w_router"], w["b_router"].reshape(1, ne))
    x1_p, h2_p, te_p, gate_p = _post(yr_p, ya_p, zp, xpf, mods_p[2], mods_p[4], mods_p[3], *post_w, s)
    x1_s, h2_s, te_s, gate_s = _post(yr_s, ya_s, zs, xsf, mods_s[2], mods_s[4], mods_s[3], *post_w, ds_)

    bm = 256
    t_all = tp + ts
    nblocks = -(-(t_all * TOP_K) // bm) + ne
    dest, be, nv = _route(jnp.concatenate([te_p, te_s], axis=0), ne, bm, nblocks)
    dest_flat = dest.reshape(-1)
    xr = jnp.zeros((nblocks * bm, d), F32)
    xr = _dispatch(dest_flat, 0, h2_p, xr)
    xr = _dispatch(dest_flat, tp, h2_s, xr)
    yr = _experts(be.reshape(-1), nv.reshape(-1), xr, wgu, w["b_gate_up"], wdn, w["b_down"], bm)
    g_post = vec(w["g_ffn_post"])
    y_p = _combine(dest_flat, 0, yr, gate_p, x1_p, mods_p[5], g_post, s)
    y_s = _combine(dest_flat, tp, yr, gate_s, x1_s, mods_s[5], g_post, ds_)

    heads_t = lambda zt_: zt_.reshape(bp, nh, hd, s).transpose(0, 3, 1, 2)
    out_p = (y_p.reshape(bp, s, d), heads_t(kt_p), heads_t(vt_p),
             lft_p.transpose(0, 2, 1), conv_p, hl_p.reshape(bp, d))
    out_s = (y_s.reshape(bs, ds_, d), zs[4].reshape(bs, ds_, nh, hd), zs[5].reshape(bs, ds_, nh, hd),
             lft_s[0].T.reshape(bs, ds_, nh), conv_s, hl_s)
    return out_p, out_s


def kernel(x_prompt, x_sample, cache_k, cache_v, cache_logf, state_conv, state_h, page_table, c_prompt, c_sample, w_ada, b_ada, g_mix_pre, g_mix_post, w_in, b_forget, conv_w, conv_b, rg_w_a, rg_b_a, rg_w_x, rg_b_x, rg_lambda, w_out, g_ffn_pre, g_ffn_post, w_router, b_router, w_gate_up, b_gate_up, w_down, b_down):
    weights = dict(w_ada=w_ada, b_ada=b_ada, g_mix_pre=g_mix_pre, g_mix_post=g_mix_post, w_in=w_in,
                   b_forget=b_forget, conv_w=conv_w, conv_b=conv_b, rg_w_a=rg_w_a, rg_b_a=rg_b_a,
                   rg_w_x=rg_w_x, rg_b_x=rg_b_x, rg_lambda=rg_lambda, w_out=w_out, g_ffn_pre=g_ffn_pre,
                   g_ffn_post=g_ffn_post, w_router=w_router, b_router=b_router, w_gate_up=w_gate_up,
                   b_gate_up=b_gate_up, w_down=w_down, b_down=b_down)
    depth = w_ada.shape[0]
    yp, ys = x_prompt, x_sample
    per_layer = []
    for l in range(depth):
        wl = {k: v[l] for k, v in weights.items()}
        out_p, out_s = _layer(yp, ys, cache_k[l], cache_v[l], cache_logf[l], state_conv[l], state_h[l],
                              page_table, c_prompt, c_sample, wl)
        yp, ys = out_p[0], out_s[0]
        per_layer.append(out_p[1:] + out_s[1:])
    stacked = [jnp.stack(leaf) for leaf in zip(*per_layer)]
    return (yp, ys, *stacked)
```

```python
import functools

import jax
import jax.numpy as jnp
from jax import lax
from jax.experimental import pallas as pl
from jax.experimental.pallas import tpu as pltpu

F32 = jnp.float32
BF16 = jnp.bfloat16
I32 = jnp.int32

EPS = 1e-6
LRU_C = 8.0
TOP_K = 4
SWIGLU_LIMIT = 7.0
SWIGLU_ALPHA = 1.702
NEG_BIG = -1e30

LANES = 128
SUBLANES = 8
VMEM_LIMIT = 56 * 1024 * 1024


def _params(n_grid_dims):
    return pltpu.CompilerParams(
        dimension_semantics=("arbitrary",) * n_grid_dims, vmem_limit_bytes=VMEM_LIMIT)


def _dot(a, b):
    return jnp.dot(a, b, preferred_element_type=F32)


def _dot_nt(a, b):
    return lax.dot_general(a, b, (((1,), (1,)), ((), ())), preferred_element_type=F32)


def _split2(x):
    hi = x.astype(BF16)
    lo = (x - hi.astype(F32)).astype(BF16)
    return hi, lo


def _split3(x):
    p1 = x.astype(BF16)
    r1 = x - p1.astype(F32)
    p2 = r1.astype(BF16)
    p3 = (r1 - p2.astype(F32)).astype(BF16)
    return p1, p2, p3


def _dot3(a, b):
    ah, al = _split2(a)
    bh, bl = _split2(b)
    return _dot(ah, bh) + _dot(ah, bl) + _dot(al, bh)


def _dot_exact01(x, u01):
    p1, p2, p3 = _split3(x)
    return _dot(p1, u01) + _dot(p2, u01) + _dot(p3, u01)


def _rms(x, g):
    ms = jnp.mean(x * x, axis=-1, keepdims=True)
    return x * lax.rsqrt(ms + EPS) * g


def _softplus(x):
    return jnp.maximum(x, 0.0) + jnp.log1p(jnp.exp(-jnp.abs(x)))


def _log_sigmoid(x):
    return jnp.minimum(x, 0.0) - jnp.log1p(jnp.exp(-jnp.abs(x)))


def _ada_kernel(c_ref, w_ref, b_ref, o_ref):
    c = c_ref[...]
    o_ref[...] = _dot3(c * jax.nn.sigmoid(c), w_ref[...]) + b_ref[...]


def _ada(c, w, b):
    n, d = c.shape
    nout = w.shape[1]
    tn = min(nout, 1536)
    assert nout % tn == 0
    return pl.pallas_call(
        _ada_kernel,
        grid=(nout // tn,),
        in_specs=[pl.BlockSpec((n, d), lambda j: (0, 0)),
                  pl.BlockSpec((d, tn), lambda j: (0, j)),
                  pl.BlockSpec((1, tn), lambda j: (0, j))],
        out_specs=pl.BlockSpec((n, tn), lambda j: (0, j)),
        out_shape=jax.ShapeDtypeStruct((n, nout), F32),
        compiler_params=_params(1),
        name="ada",
    )(c, w, b.reshape(1, nout))


def _inproj_kernel(x_ref, sc_ref, sh_ref, g_ref, w_ref, wft_ref, bf_ref, *rest, nz, nt):
    if nt:
        wt_ref, z_ref = rest[:2]
        zt_refs = rest[2:2 + nt]
    else:
        z_ref = rest[0]
    lft_ref, h_scr = rest[-2:]
    j = pl.program_id(1)

    @pl.when(j == 0)
    def _():
        h = _rms(x_ref[...], g_ref[...]) * (1.0 + sc_ref[...]) + sh_ref[...]
        hb = h.astype(BF16)
        h_scr[...] = hb
        lft_ref[...] = _log_sigmoid(_dot_nt(wft_ref[...], hb) + bf_ref[...])

    if nt:
        @pl.when(j < nz)
        def _():
            z_ref[...] = _dot(h_scr[...], w_ref[jnp.minimum(j, nz - 1)])

        for n in range(nt):
            @pl.when(j == nz + n)
            def _(n=n):
                zt_refs[n][...] = _dot_nt(wt_ref[n], h_scr[...])
    else:
        z_ref[...] = _dot(h_scr[...], w_ref[j])


def _mod_spec(mod, tm, rows_per_batch):
    if mod.ndim == 3:
        tiles_per_batch = rows_per_batch // tm
        return pl.BlockSpec((None, 1, mod.shape[-1]), lambda i, *_: (i // tiles_per_batch, 0, 0))
    return pl.BlockSpec((tm, mod.shape[-1]), lambda i, *_: (i, 0))


def _inproj(x, sc, sh, g, w, wt, wft, bfo, nbatch, rows_per_batch):
    t, d = x.shape
    nh = wft.shape[0]
    nz = w.shape[0]
    nt = 0 if wt is None else wt.shape[0]
    s = t // nbatch
    tm = min(1024, s)
    assert s % tm == 0 and (sc.ndim == 2 or rows_per_batch % tm == 0)
    tpb = s // tm
    resident = lambda n: pl.BlockSpec((n, d, d), lambda i, j: (0, 0, 0), pipeline_mode=pl.Buffered(1))
    in_specs = [pl.BlockSpec((tm, d), lambda i, j: (i, 0)),
                _mod_spec(sc, tm, rows_per_batch), _mod_spec(sh, tm, rows_per_batch),
                pl.BlockSpec((1, d), lambda i, j: (0, 0)),
                resident(nz),
                pl.BlockSpec((nh, d), lambda i, j: (0, 0)),
                pl.BlockSpec((nh, 1), lambda i, j: (0, 0))]
    out_specs = [pl.BlockSpec((None, tm, d), lambda i, j: (jnp.minimum(j, nz - 1), i, 0))]
    out_shape = [jax.ShapeDtypeStruct((nz, t, d), F32)]
    args = [x, sc, sh, g, w, wft, bfo]
    if nt:
        in_specs.append(resident(nt))
        args.append(wt)
        for _ in range(nt):
            out_specs.append(pl.BlockSpec((None, d, tm), lambda i, j: (i // tpb, 0, i % tpb)))
            out_shape.append(jax.ShapeDtypeStruct((nbatch, d, s), F32))
    out_specs.append(pl.BlockSpec((None, nh, tm), lambda i, j: (i // tpb, 0, i % tpb)))
    out_shape.append(jax.ShapeDtypeStruct((nbatch, nh, s), F32))
    return pl.pallas_call(
        functools.partial(_inproj_kernel, nz=nz, nt=nt),
        grid=(t // tm, nz + nt),
        in_specs=in_specs,
        out_specs=out_specs,
        out_shape=out_shape,
        scratch_shapes=[pltpu.VMEM((tm, d), BF16)],
        compiler_params=_params(2),
        name="inproj",
    )(*args)


def _cumsum_kernel(lf_ref, o_ref, *, cw, seg):
    n = lf_ref.shape[1]
    r = lax.broadcasted_iota(I32, (cw, cw), 0)
    c = lax.broadcasted_iota(I32, (cw, cw), 1)
    keep = r <= c
    if seg is not None:
        keep = jnp.logical_and(keep, (r // seg) == (c // seg))
    u01 = jnp.where(keep, 1.0, 0.0).astype(BF16)
    carry = jnp.zeros((lf_ref.shape[0], 1), F32)
    for i in range(n // cw):
        f = _dot_exact01(lf_ref[:, i * cw:(i + 1) * cw], u01)
        if seg is None:
            f = f + carry
            carry = f[:, cw - 1:cw]
        o_ref[:, i * cw:(i + 1) * cw] = f


def _cumsum_lanes(lft, seg=None):
    nb, nh, s = lft.shape
    block = s if seg is None else min(s, 2 * LANES)
    cw = min(2 * LANES, block)
    assert s % block == 0 and block % cw == 0 and (seg is None or cw % seg == 0)
    spec = pl.BlockSpec((None, nh, block), lambda b, n: (b, 0, n))
    return pl.pallas_call(
        functools.partial(_cumsum_kernel, cw=cw, seg=seg),
        grid=(nb, s // block),
        in_specs=[spec],
        out_specs=spec,
        out_shape=jax.ShapeDtypeStruct((nb, nh, s), F32),
        compiler_params=_params(2),
        name="cumsum_logf",
    )(lft)


def _rglru_coeffs(xc, wbd_ref, ba, bx, lam):
    ng, gw = wbd_ref.shape[0], wbd_ref.shape[1]
    xb = xc.astype(BF16)
    ra, ri = [], []
    for g in range(ng):
        o = _dot(xb[:, g * gw:(g + 1) * gw], wbd_ref[g])
        ra.append(o[:, :gw])
        ri.append(o[:, gw:])
    r = jax.nn.sigmoid((ra[0] if ng == 1 else jnp.concatenate(ra, axis=1)) + ba)
    i = jax.nn.sigmoid((ri[0] if ng == 1 else jnp.concatenate(ri, axis=1)) + bx)
    log_a = -LRU_C * r * _softplus(-lam)
    a = jnp.exp(log_a)
    th = jnp.tanh(log_a)
    mult = jnp.sqrt(-2.0 * th / (1.0 - th))
    return a, mult, i


def _scan8(a8, u8, row):
    for s in (1, 2, 4):
        a_sh = pltpu.roll(a8, s, axis=0)
        u_sh = pltpu.roll(u8, s, axis=0)
        m = row >= s
        u8 = u8 + a8 * jnp.where(m, u_sh, 0.0)
        a8 = a8 * jnp.where(m, a_sh, 1.0)
    return a8, u8


def _rnn_prompt_kernel(x_ref, prev_ref, h0_ref, cw_ref, cb_ref, wbd_ref, ba_ref, bx_ref, lam_ref,
                       y_ref, cs_ref, hl_ref, ext_scr, a_scr, u_scr, h_scr, *, reset_first):
    c = pl.program_id(1)
    tc, d = x_ref.shape
    kw = cw_ref.shape[0]

    @pl.when(c == 0)
    def _():
        ext_scr[0:SUBLANES, :] = prev_ref[...]
        h_scr[...] = h0_ref[...]

    ext_scr[SUBLANES:SUBLANES + tc, :] = x_ref[...]
    w = cw_ref[...]
    xc = cb_ref[...]
    for j in range(kw):
        off = SUBLANES - (kw - 1) + j
        xc = xc + w[j:j + 1, :] * ext_scr[off:off + tc, :]
    ext_scr[0:SUBLANES, :] = ext_scr[tc:tc + SUBLANES, :]

    a, mult, i = _rglru_coeffs(xc, wbd_ref, ba_ref[...], bx_ref[...], lam_ref[...])
    if reset_first:
        row = lax.broadcasted_iota(I32, (tc, 1), 0)
        mult = jnp.where(row == jnp.where(c == 0, 0, -1), 1.0, mult)
    a_scr[...] = a
    u_scr[...] = mult * i * xc

    row8 = lax.broadcasted_iota(I32, (SUBLANES, d), 0)

    def body(g, h):
        r0 = pl.multiple_of(g * SUBLANES, SUBLANES)
        a8, u8 = _scan8(a_scr[pl.ds(r0, SUBLANES), :], u_scr[pl.ds(r0, SUBLANES), :], row8)
        h8 = u8 + a8 * h
        y_ref[pl.ds(r0, SUBLANES), :] = h8
        return h8[SUBLANES - 1:SUBLANES, :]

    h = lax.fori_loop(0, tc // SUBLANES, body, h_scr[...], unroll=4)
    h_scr[...] = h

    @pl.when(c == pl.num_programs(1) - 1)
    def _():
        hl_ref[...] = h
        cs_ref[...] = x_ref[tc - (kw - 1):tc, :]


def _rnn_prompt(x3, prev8, h0, cw, cb, wbd, ba, bx, lam, reset_first):
    b = prev8.shape[0]
    t, d = x3.shape[1], x3.shape[2]
    s = t // b
    kw = cw.shape[0]
    tc = min(256, s)
    nc = s // tc
    assert s % tc == 0 and kw - 1 <= SUBLANES and tc % SUBLANES == 0
    vec = pl.BlockSpec((1, d), lambda bi, ci: (0, 0))
    return pl.pallas_call(
        functools.partial(_rnn_prompt_kernel, reset_first=reset_first),
        grid=(b, nc),
        in_specs=[pl.BlockSpec((None, tc, d), lambda bi, ci: (0, bi * nc + ci, 0)),
                  pl.BlockSpec((None, SUBLANES, d), lambda bi, ci: (bi, 0, 0)),
                  pl.BlockSpec((None, 1, d), lambda bi, ci: (bi, 0, 0)),
                  pl.BlockSpec((kw, d), lambda bi, ci: (0, 0)), vec,
                  pl.BlockSpec(wbd.shape, lambda bi, ci: (0, 0, 0)), vec, vec, vec],
        out_specs=[pl.BlockSpec((tc, d), lambda bi, ci: (bi * nc + ci, 0)),
                   pl.BlockSpec((None, kw - 1, d), lambda bi, ci: (bi, 0, 0)),
                   pl.BlockSpec((None, 1, d), lambda bi, ci: (bi, 0, 0))],
        out_shape=[jax.ShapeDtypeStruct((t, d), F32),
                   jax.ShapeDtypeStruct((b, kw - 1, d), F32),
                   jax.ShapeDtypeStruct((b, 1, d), F32)],
        scratch_shapes=[pltpu.VMEM((tc + SUBLANES, d), F32), pltpu.VMEM((tc, d), F32),
                        pltpu.VMEM((tc, d), F32), pltpu.VMEM((1, d), F32)],
        compiler_params=_params(2),
        name="rnn_prompt",
    )(x3, prev8, h0, cw, cb, wbd, ba, bx, lam)


def _rnn_sample_kernel(x_ref, prev_ref, h0_ref, cw_ref, cb_ref, wbd_ref, ba_ref, bx_ref, lam_ref,
                       y_ref):
    r, d = x_ref.shape
    kw = cw_ref.shape[0]
    x = x_ref[...]
    p = prev_ref[...]
    step = lax.broadcasted_iota(I32, (r, 1), 0) & (SUBLANES - 1)
    w = cw_ref[...]
    xc = cb_ref[...]
    for j in range(kw):
        back = kw - 1 - j
        if back == 0:
            xs = x
        else:
            xs = jnp.where(step >= back, pltpu.roll(x, back, axis=0),
                           pltpu.roll(p, (r - SUBLANES + back) % r, axis=0))
        xc = xc + w[j:j + 1, :] * xs
    a, mult, i = _rglru_coeffs(xc, wbd_ref, ba_ref[...], bx_ref[...], lam_ref[...])
    u = mult * i * xc + a * h0_ref[...]
    for s in (1, 2, 4):
        m = step >= s
        a_sh = pltpu.roll(a, s, axis=0)
        u_sh = pltpu.roll(u, s, axis=0)
        u = u + a * jnp.where(m, u_sh, 0.0)
        a = a * jnp.where(m, a_sh, 1.0)
    y_ref[...] = u


def _rnn_sample(x3, prev8, h08, cw, cb, wbd, ba, bx, lam):
    t, d = x3.shape[1], x3.shape[2]
    nb_total = t // SUBLANES
    kw = cw.shape[0]
    nb = min(16, nb_total)
    r = nb * SUBLANES
    assert nb_total % nb == 0 and nb % SUBLANES == 0
    vec = pl.BlockSpec((1, d), lambda i: (0, 0))
    return pl.pallas_call(
        _rnn_sample_kernel,
        grid=(nb_total // nb,),
        in_specs=[pl.BlockSpec((None, r, d), lambda i: (0, i, 0)),
                  pl.BlockSpec((r, d), lambda i: (i, 0)),
                  pl.BlockSpec((r, d), lambda i: (i, 0)),
                  pl.BlockSpec((kw, d), lambda i: (0, 0)), vec,
                  pl.BlockSpec(wbd.shape, lambda i: (0, 0, 0)), vec, vec, vec],
        out_specs=pl.BlockSpec((r, d), lambda i: (i, 0)),
        out_shape=jax.ShapeDtypeStruct((t, d), F32),
        compiler_params=_params(1),
        name="rnn_sample",
    )(x3, prev8, h08, cw, cb, wbd, ba, bx, lam)


def _fox_prompt_kernel(q_ref, kt_ref, vt_ref, f_ref, o_ref, *, hd, scale):
    i = pl.program_id(2)
    tq = q_ref.shape[0]
    lane = lax.broadcasted_iota(I32, (1, 2 * hd), 1)
    q = q_ref[...] * scale
    qs = jnp.concatenate([jnp.where(lane < hd, q, 0.0), jnp.where(lane >= hd, q, 0.0)], axis=0).astype(BF16)

    def scores(kb):
        k0 = pl.multiple_of(kb * tq, tq)
        kt = kt_ref[:, pl.ds(k0, tq)].astype(BF16)
        fb = f_ref[:, pl.ds(k0, tq)]
        s = _dot(qs, kt)
        return jnp.concatenate([s[:tq] - fb[0:1, :], s[tq:] - fb[1:2, :]], axis=0)

    vrow = lax.broadcasted_iota(I32, (2 * hd, 1), 0)

    def update(kb, s, m, acc):
        k0 = pl.multiple_of(kb * tq, tq)
        vt = vt_ref[:, pl.ds(k0, tq)]
        vt0 = jnp.where(vrow < hd, vt, 1.0).astype(BF16)
        vt1 = jnp.where(vrow >= hd, vt, 1.0).astype(BF16)
        m_new = jnp.maximum(m, jnp.max(s, axis=-1, keepdims=True))
        p = jnp.exp(s - m_new).astype(BF16)
        pv = jnp.concatenate([_dot_nt(p[:tq], vt0), _dot_nt(p[tq:], vt1)], axis=0)
        return m_new, jnp.exp(m - m_new) * acc + pv

    def body(kb, carry):
        s, m, acc = carry
        return (scores(kb + 1),) + update(kb, s, m, acc)

    init = (scores(0), jnp.full((2 * tq, 1), NEG_BIG, F32), jnp.zeros((2 * tq, 2 * hd), F32))
    s, m, acc = lax.fori_loop(0, i, body, init)
    r = lax.broadcasted_iota(I32, (2 * tq, tq), 0) & (tq - 1)
    c = lax.broadcasted_iota(I32, (2 * tq, tq), 1)
    _, acc = update(i, jnp.where(c <= r, s, NEG_BIG), m, acc)
    o = acc / pltpu.roll(acc, hd, axis=1)
    o_ref[...] = jnp.where(lane < hd, o[:tq], o[tq:])


def _fox_prompt(z4, kt, vt, ft4, hd):
    _, b, s, d = z4.shape
    assert 2 * hd == LANES and d % LANES == 0
    npair = d // LANES
    tq = min(256, s)
    assert s % tq == 0 and tq & (tq - 1) == 0
    return pl.pallas_call(
        functools.partial(_fox_prompt_kernel, hd=hd, scale=hd ** -0.5),
        grid=(b, npair, s // tq),
        in_specs=[pl.BlockSpec((None, None, tq, LANES), lambda bi, j, i: (1, bi, i, j)),
                  pl.BlockSpec((None, LANES, s), lambda bi, j, i: (bi, j, 0)),
                  pl.BlockSpec((None, LANES, s), lambda bi, j, i: (bi, j, 0)),
                  pl.BlockSpec((None, None, 2, s), lambda bi, j, i: (bi, j, 0, 0))],
        out_specs=pl.BlockSpec((None, tq, LANES), lambda bi, j, i: (bi, i, j)),
        out_shape=jax.ShapeDtypeStruct((b, s, d), F32),
        compiler_params=_params(3),
        name="fox_prompt",
    )(z4, kt, vt, ft4)


def _fox_sample_kernel(pt_ref, q_ref, kn_ref, vn_ref, cn_ref, *rest, nh, hd, scale, pps, grp):
    del pt_ref
    k_refs, v_refs, lf_refs = rest[:pps], rest[pps:2 * pps], rest[2 * pps:3 * pps]
    o_ref, qx_scr, m_scr, l_scr, acc_scr, fcar_scr = rest[3 * pps:]
    j = pl.program_id(1)
    ds_, d = q_ref.shape
    page = lf_refs[0].shape[1]
    nr = nh * ds_

    def expand(x):
        return jnp.broadcast_to(x[:, None, :], (nh, ds_, x.shape[1])).reshape(nr, x.shape[1])

    def own_head():
        rh = lax.broadcasted_iota(I32, (nr, d), 0) // ds_
        ch = lax.broadcasted_iota(I32, (nr, d), 1) // hd
        return rh == ch

    @pl.when(j == 0)
    def _():
        q = q_ref[...] * scale
        qt = jnp.broadcast_to(q[None], (nh, ds_, d)).reshape(nr, d)
        qx_scr[...] = jnp.where(own_head(), qt, 0.0).astype(BF16)
        m_scr[...] = jnp.full(m_scr.shape, NEG_BIG, F32)
        l_scr[...] = jnp.zeros(l_scr.shape, F32)
        acc_scr[...] = jnp.zeros(acc_scr.shape, F32)
        fcar_scr[...] = jnp.zeros(fcar_scr.shape, F32)

    def update(s, pv_fn):
        m_old = m_scr[...]
        m_new = jnp.maximum(m_old, jnp.max(s, axis=-1, keepdims=True))
        p = jnp.exp(s - m_new)
        alpha = jnp.exp(m_old - m_new)
        l_scr[...] = alpha * l_scr[...] + jnp.sum(p, axis=-1, keepdims=True)
        m_scr[...] = m_new
        acc_scr[...] = alpha * acc_scr[...] + pv_fn(p.astype(BF16))

    r = lax.broadcasted_iota(I32, (page, page), 0)
    c = lax.broadcasted_iota(I32, (page, page), 1)
    u01 = jnp.where(r <= c, 1.0, 0.0).astype(BF16)
    qx = qx_scr[...]
    fcar = fcar_scr[...]
    for g0 in range(0, pps, grp):
        pages = range(g0, min(g0 + grp, pps))
        s_parts = []
        for pi in pages:
            ft = _dot_exact01(lf_refs[pi][...], u01)
            s_parts.append(_dot(qx, k_refs[pi][...].astype(BF16)) - expand(ft + fcar))
            fcar = fcar + ft[:, page - 1:page]
        s = s_parts[0] if len(s_parts) == 1 else jnp.concatenate(s_parts, axis=1)

        def pv_pages(pb, pages=pages):
            out = None
            for n, pi in enumerate(pages):
                term = _dot_nt(pb[:, n * page:(n + 1) * page], v_refs[pi][...].astype(BF16))
                out = term if out is None else out + term
            return out

        update(s, pv_pages)
    fcar_scr[...] = fcar

    @pl.when(j == pl.num_programs(1) - 1)
    def _():
        fn = expand(cn_ref[...] + fcar_scr[...])
        sn = _dot_nt(qx_scr[...], kn_ref[...].astype(BF16)) - fn
        qi = lax.broadcasted_iota(I32, (nr, ds_), 0) & (ds_ - 1)
        key = lax.broadcasted_iota(I32, (nr, ds_), 1)
        sn = jnp.where(key <= qi, sn, NEG_BIG)
        update(sn, lambda pb: _dot(pb, vn_ref[...].astype(BF16)))
        o = jnp.where(own_head(), acc_scr[...] / l_scr[...], 0.0)
        o_ref[...] = o.reshape(nh, ds_, d).sum(axis=0)


def _fox_sample(page_table, z3, slabs, cnt, ckt, cvt, clft, nh, hd):
    b, npages = page_table.shape
    t, d = z3.shape[1], z3.shape[2]
    ds_ = t // b
    page = clft.shape[2]
    pps = next(n for n in (8, 4, 2, 1) if npages % n == 0)
    grp = min(4, pps)
    nr = nh * ds_
    assert ds_ == SUBLANES and ckt.shape[1] == d
    tok = lambda slab: pl.BlockSpec((None, ds_, d), lambda bi, j, pt: (slab, bi, 0))

    def pg(rows, pi):
        return pl.BlockSpec((None, rows, page), lambda bi, j, pt: (pt[bi * npages + j * pps + pi], 0, 0))

    in_specs = [tok(slabs[0]), tok(slabs[1]), tok(slabs[2]),
                pl.BlockSpec((None, nh, ds_), lambda bi, j, pt: (bi, 0, 0))]
    in_specs += [pg(d, pi) for pi in range(pps)] * 2 + [pg(nh, pi) for pi in range(pps)]
    grid_spec = pltpu.PrefetchScalarGridSpec(
        num_scalar_prefetch=1,
        grid=(b, npages // pps),
        in_specs=in_specs,
        out_specs=pl.BlockSpec((ds_, d), lambda bi, j, pt: (bi, 0)),
        scratch_shapes=[pltpu.VMEM((nr, d), BF16), pltpu.VMEM((nr, 1), F32), pltpu.VMEM((nr, 1), F32),
                        pltpu.VMEM((nr, d), F32), pltpu.VMEM((nh, 1), F32)],
    )
    return pl.pallas_call(
        functools.partial(_fox_sample_kernel, nh=nh, hd=hd, scale=hd ** -0.5, pps=pps, grp=grp),
        grid_spec=grid_spec,
        out_shape=jax.ShapeDtypeStruct((t, d), F32),
        compiler_params=_params(2),
        name="fox_sample",
    )(page_table.reshape(-1), z3, z3, z3, cnt, *([ckt] * pps), *([cvt] * pps), *([clft] * pps))


def _post_kernel(yr_ref, ya_ref, ga_ref, gb_ref, x_ref, gt1_ref, sc2_ref, sh2_ref, wo_ref, gpost_ref,
                 gpre_ref, wr_ref, br_ref, x1_ref, h2_ref, te_ref, gate_ref):
    merged = jax.nn.sigmoid(ga_ref[...]) * yr_ref[...] + jax.nn.sigmoid(gb_ref[...]) * ya_ref[...]
    out = _dot(merged.astype(BF16), wo_ref[...])
    x1 = x_ref[...] + gt1_ref[...] * _rms(out, gpost_ref[...])
    x1_ref[...] = x1
    h2 = _rms(x1, gpre_ref[...]) * (1.0 + sc2_ref[...]) + sh2_ref[...]
    h2_ref[...] = h2
    logits = _dot3(h2, wr_ref[...]) + br_ref[...]
    ne = logits.shape[1]
    lane = lax.broadcasted_iota(I32, logits.shape, 1)
    vals, idxs = [], []
    for _ in range(TOP_K):
        m = jnp.max(logits, axis=1, keepdims=True)
        idx = jnp.min(jnp.where(logits == m, lane, ne), axis=1, keepdims=True)
        vals.append(m)
        idxs.append(idx)
        logits = jnp.where(lane == idx, -jnp.inf, logits)
    e = jnp.exp(jnp.concatenate(vals, axis=1) - vals[0])
    gate_ref[...] = e / jnp.sum(e, axis=1, keepdims=True)
    te_ref[...] = jnp.concatenate(idxs, axis=1)


def _post(y_rnn, y_attn, z3, x, gt1, sc2, sh2, wo, gpost, gpre, wr, br, rows_per_batch):
    t, d = x.shape
    ne = wr.shape[1]
    tm = min(256, rows_per_batch if gt1.ndim == 3 else t)
    assert t % tm == 0
    row = pl.BlockSpec((tm, d), lambda i: (i, 0))
    vec = pl.BlockSpec((1, d), lambda i: (0, 0))
    mod = lambda m: _mod_spec(m, tm, rows_per_batch)
    return pl.pallas_call(
        _post_kernel,
        grid=(t // tm,),
        in_specs=[row, row,
                  pl.BlockSpec((None, tm, d), lambda i: (2, i, 0)),
                  pl.BlockSpec((None, tm, d), lambda i: (3, i, 0)),
                  row, mod(gt1), mod(sc2), mod(sh2),
                  pl.BlockSpec((d, d), lambda i: (0, 0)), vec, vec,
                  pl.BlockSpec((d, ne), lambda i: (0, 0)),
                  pl.BlockSpec((1, ne), lambda i: (0, 0))],
        out_specs=[row, row,
                   pl.BlockSpec((tm, TOP_K), lambda i: (i, 0)),
                   pl.BlockSpec((tm, TOP_K), lambda i: (i, 0))],
        out_shape=[jax.ShapeDtypeStruct((t, d), F32), jax.ShapeDtypeStruct((t, d), F32),
                   jax.ShapeDtypeStruct((t, TOP_K), I32), jax.ShapeDtypeStruct((t, TOP_K), F32)],
        compiler_params=_params(1),
        name="post_mixer",
    )(y_rnn, y_attn, z3, z3, x, gt1, sc2, sh2, wo, gpost, gpre, wr, br)


def _multi_hot(te, ne):
    lane = lax.broadcasted_iota(I32, (te.shape[0], ne), 1)
    hot = jnp.zeros((te.shape[0], ne), F32)
    for k in range(te.shape[1]):
        hot = hot + jnp.where(lane == te[:, k:k + 1], 1.0, 0.0)
    return hot


def _rank_kernel(te_ref, c_ref, cnt_ref, cnt_scr):
    i = pl.program_id(0)
    tr = te_ref.shape[0]

    @pl.when(i == 0)
    def _():
        cnt_scr[...] = jnp.zeros(cnt_scr.shape, F32)

    hot = _multi_hot(te_ref[...], c_ref.shape[1])
    r = lax.broadcasted_iota(I32, (tr, tr), 0)
    c = lax.broadcasted_iota(I32, (tr, tr), 1)
    below = jnp.where(c < r, 1.0, 0.0).astype(BF16)
    c_ref[...] = _dot(below, hot.astype(BF16)) + cnt_scr[...]
    cnt_scr[...] = cnt_scr[...] + jnp.sum(hot, axis=0, keepdims=True)
    cnt_ref[...] = cnt_scr[...]


def _dest_kernel(cnt_ref, c_ref, te_ref, dest_ref, be_ref, nv_ref, *, bm):
    ne = cnt_ref.shape[1]
    nblk = jnp.ceil(cnt_ref[...] * (1.0 / bm))
    r = lax.broadcasted_iota(I32, (ne, ne), 0)
    c = lax.broadcasted_iota(I32, (ne, ne), 1)
    before = jnp.where(r < c, 1.0, 0.0).astype(BF16)
    start_blk = _dot_exact01(jnp.broadcast_to(nblk, (SUBLANES, ne)), before)[0:1, :]
    end_blk = start_blk + nblk

    te = te_ref[...]
    lane = lax.broadcasted_iota(I32, c_ref.shape, 1)
    pos = c_ref[...] + start_blk * float(bm)
    cols = [jnp.sum(jnp.where(lane == te[:, k:k + 1], pos, 0.0), axis=1, keepdims=True) for k in range(te.shape[1])]
    dest_ref[...] = jnp.concatenate(cols, axis=1).astype(I32)

    @pl.when(pl.program_id(0) == 0)
    def _():
        blk = lax.broadcasted_iota(I32, be_ref.shape, 1).astype(F32)
        be = jnp.zeros(be_ref.shape, F32)
        for e in range(ne):
            be = be + jnp.where(blk >= end_blk[:, e:e + 1], 1.0, 0.0)
        be_ref[...] = jnp.minimum(be, float(ne - 1)).astype(I32)
        nv_ref[...] = jnp.broadcast_to(end_blk[:, ne - 1:ne], nv_ref.shape).astype(I32)


def _route(te, ne, bm, nblocks):
    t, k = te.shape
    tr = 512 if t % 512 == 0 else LANES
    assert t % tr == 0
    c, cnt = pl.pallas_call(
        _rank_kernel,
        grid=(t // tr,),
        in_specs=[pl.BlockSpec((tr, k), lambda i: (i, 0))],
        out_specs=[pl.BlockSpec((tr, ne), lambda i: (i, 0)), pl.BlockSpec((1, ne), lambda i: (0, 0))],
        out_shape=[jax.ShapeDtypeStruct((t, ne), F32), jax.ShapeDtypeStruct((1, ne), F32)],
        scratch_shapes=[pltpu.VMEM((1, ne), F32)],
        compiler_params=_params(1),
        name="moe_rank",
    )(te)
    nbp = -(-nblocks // LANES) * LANES
    return pl.pallas_call(
        functools.partial(_dest_kernel, bm=bm),
        grid=(t // tr,),
        in_specs=[pl.BlockSpec((1, ne), lambda i: (0, 0)),
                  pl.BlockSpec((tr, ne), lambda i: (i, 0)),
                  pl.BlockSpec((tr, k), lambda i: (i, 0))],
        out_specs=[pl.BlockSpec((tr, k), lambda i: (i, 0)),
                   pl.BlockSpec((1, nbp), lambda i: (0, 0)),
                   pl.BlockSpec((1, LANES), lambda i: (0, 0))],
        out_shape=[jax.ShapeDtypeStruct((t, k), I32), jax.ShapeDtypeStruct((1, nbp), I32),
                   jax.ShapeDtypeStruct((1, LANES), I32)],
        compiler_params=_params(1),
        name="moe_dest",
    )(cnt, c, te)


def _dispatch_kernel(dest_ref, src_ref, xr_in_ref, xr_ref, sem, *, td):
    del xr_in_ref

    def row_copy(t, k):
        return pltpu.make_async_copy(src_ref.at[pl.ds(t, 1)],
                                     xr_ref.at[pl.ds(dest_ref[t * TOP_K + k], 1)], sem)

    def start(t, carry):
        for k in range(TOP_K):
            row_copy(t, k).start(priority=k % 2)
        return carry

    def wait(t, carry):
        for k in range(TOP_K):
            row_copy(t, k).wait()
        return carry

    lax.fori_loop(0, td, start, 0)
    lax.fori_loop(0, td, wait, 0)


def _dispatch(dest_flat, tok_offset, src, xr):
    t, d = src.shape
    td = LANES
    assert t % td == 0 and tok_offset % td == 0
    off = tok_offset // td
    return pl.pallas_call(
        functools.partial(_dispatch_kernel, td=td),
        grid=(t // td,),
        in_specs=[pl.BlockSpec((td * TOP_K,), lambda i: (i + off,), memory_space=pltpu.SMEM),
                  pl.BlockSpec((td, d), lambda i: (i, 0)), pl.BlockSpec(memory_space=pl.ANY)],
        out_specs=pl.BlockSpec(memory_space=pl.ANY),
        out_shape=jax.ShapeDtypeStruct(xr.shape, xr.dtype),
        scratch_shapes=[pltpu.SemaphoreType.DMA(())],
        input_output_aliases={2: 0},
        compiler_params=_params(1),
        name="moe_dispatch",
    )(dest_flat, src, xr)


def _expert_kernel(be_ref, nv_ref, x_ref, wgu_ref, bgu_ref, wd_ref, bd_ref, o_ref, wgu_bf, wd_bf, *, fc):
    i = pl.program_id(0)
    f = wd_ref.shape[0]

    @pl.when(jnp.logical_or(i == 0, be_ref[i] != be_ref[jnp.maximum(i - 1, 0)]))
    def _():
        for c in range(2 * f // fc):
            wgu_bf[:, c * fc:(c + 1) * fc] = wgu_ref[:, c * fc:(c + 1) * fc].astype(BF16)
        for c in range(f // fc):
            wd_bf[c * fc:(c + 1) * fc, :] = wd_ref[c * fc:(c + 1) * fc, :].astype(BF16)

    @pl.when(i < nv_ref[0])
    def _():
        xb = x_ref[...].astype(BF16)
        acc = jnp.zeros(o_ref.shape, F32)
        for c in range(f // fc):
            glu = _dot(xb, wgu_bf[:, c * fc:(c + 1) * fc]) + bgu_ref[:, c * fc:(c + 1) * fc]
            lin = _dot(xb, wgu_bf[:, f + c * fc:f + (c + 1) * fc]) + bgu_ref[:, f + c * fc:f + (c + 1) * fc]
            glu = jnp.minimum(glu, SWIGLU_LIMIT)
            lin = jnp.clip(lin, -SWIGLU_LIMIT, SWIGLU_LIMIT)
            act = glu * jax.nn.sigmoid(SWIGLU_ALPHA * glu) * (lin + 1.0)
            acc = acc + _dot(act.astype(BF16), wd_bf[c * fc:(c + 1) * fc, :])
        o_ref[...] = acc + bd_ref[...]

    @pl.when(i >= nv_ref[0])
    def _():
        o_ref[...] = jnp.zeros(o_ref.shape, F32)


def _experts(be, nv, xr, wgu, bgu, wd, bd, bm):
    nr, d = xr.shape
    ne, _, f2 = wgu.shape
    f = f2 // 2
    fc = min(512, f)
    grid_spec = pltpu.PrefetchScalarGridSpec(
        num_scalar_prefetch=2,
        grid=(nr // bm,),
        in_specs=[pl.BlockSpec((bm, d), lambda i, be, nv: (i, 0)),
                  pl.BlockSpec((None, d, f2), lambda i, be, nv: (be[i], 0, 0)),
                  pl.BlockSpec((None, 1, f2), lambda i, be, nv: (be[i], 0, 0)),
                  pl.BlockSpec((None, f, d), lambda i, be, nv: (be[i], 0, 0)),
                  pl.BlockSpec((None, 1, d), lambda i, be, nv: (be[i], 0, 0))],
        out_specs=pl.BlockSpec((bm, d), lambda i, be, nv: (i, 0)),
        scratch_shapes=[pltpu.VMEM((d, f2), BF16), pltpu.VMEM((f, d), BF16)],
    )
    return pl.pallas_call(
        functools.partial(_expert_kernel, fc=fc),
        grid_spec=grid_spec,
        out_shape=jax.ShapeDtypeStruct((nr, d), F32),
        compiler_params=_params(1),
        name="moe_experts",
    )(be, nv, xr, wgu, bgu.reshape(ne, 1, f2), wd, bd.reshape(ne, 1, d))


def _combine_kernel(dest_ref, yr_ref, gate_ref, x1_ref, gt2_ref, g_ref, o_ref, buf, sem, *, tc):
    def row_copy(t, k):
        return pltpu.make_async_copy(yr_ref.at[pl.ds(dest_ref[t * TOP_K + k], 1)],
                                     buf.at[k, pl.ds(t, 1)], sem)

    def start(t, carry):
        for k in range(TOP_K):
            row_copy(t, k).start(priority=k % 2)
        return carry

    def wait(t, carry):
        for k in range(TOP_K):
            row_copy(t, k).wait()
        return carry

    lax.fori_loop(0, tc, start, 0)
    lax.fori_loop(0, tc, wait, 0)
    gate = gate_ref[...]
    f = gate[:, 0:1] * buf[0]
    for k in range(1, TOP_K):
        f = f + gate[:, k:k + 1] * buf[k]
    o_ref[...] = x1_ref[...] + gt2_ref[...] * _rms(f, g_ref[...])


def _combine(dest_flat, tok_offset, yr, gate, x1, gt2, g, rows_per_batch):
    t, d = x1.shape
    tc = LANES
    assert t % tc == 0 and tok_offset % tc == 0
    off = tok_offset // tc
    row = pl.BlockSpec((tc, d), lambda i: (i, 0))
    return pl.pallas_call(
        functools.partial(_combine_kernel, tc=tc),
        grid=(t // tc,),
        in_specs=[pl.BlockSpec((tc * TOP_K,), lambda i: (i + off,), memory_space=pltpu.SMEM),
                  pl.BlockSpec(memory_space=pl.ANY),
                  pl.BlockSpec((tc, TOP_K), lambda i: (i, 0)),
                  row, _mod_spec(gt2, tc, rows_per_batch),
                  pl.BlockSpec((1, d), lambda i: (0, 0))],
        out_specs=row,
        out_shape=jax.ShapeDtypeStruct((t, d), F32),
        scratch_shapes=[pltpu.VMEM((TOP_K, tc, d), F32), pltpu.SemaphoreType.DMA(())],
        compiler_params=_params(1),
        name="moe_combine",
    )(dest_flat, yr, gate, x1, gt2, g)


def _run_plan_kernel(te_ref, lp_ref, n8_ref, soff8_ref, dpre8_ref, tot8_ref, carry_scr):
    i = pl.program_id(0)
    tt, ne = te_ref.shape[0], n8_ref.shape[1]

    @pl.when(i == 0)
    def _():
        carry_scr[...] = jnp.zeros(carry_scr.shape, F32)

    te = te_ref[...]
    hot = _multi_hot(te, ne)
    r = lax.broadcasted_iota(I32, (tt, tt), 0)
    c = lax.broadcasted_iota(I32, (tt, tt), 1)
    below = jnp.where(c < r, 1.0, 0.0).astype(BF16)
    rank = _dot(below, hot.astype(BF16))
    n8 = jnp.ceil(jnp.sum(hot, axis=0, keepdims=True) * (1.0 / SUBLANES))
    er = lax.broadcasted_iota(I32, (ne, ne), 0)
    ec = lax.broadcasted_iota(I32, (ne, ne), 1)
    before = jnp.where(er < ec, 1.0, 0.0).astype(BF16)
    soff8 = _dot_exact01(jnp.broadcast_to(n8, (SUBLANES, ne)), before)[0:1, :]
    lane = lax.broadcasted_iota(I32, (tt, ne), 1)
    pos = rank + soff8 * float(SUBLANES)
    cols = [jnp.sum(jnp.where(lane == te[:, k:k + 1], pos, 0.0), axis=1, keepdims=True) for k in range(te.shape[1])]
    lp_ref[...] = jnp.concatenate(cols, axis=1).astype(I32)
    n8_ref[...] = n8.astype(I32)
    soff8_ref[...] = soff8.astype(I32)
    dpre8_ref[...] = carry_scr[...].astype(I32)
    carry_scr[...] = carry_scr[...] + n8
    tot8_ref[...] = carry_scr[...]


def _block_plan_kernel(tot8_ref, dpre8_ref, dbase8_ref, be_ref, nv_ref, *, bm):
    ne = tot8_ref.shape[1]
    per_blk = bm // SUBLANES
    nblk = jnp.ceil(tot8_ref[...] * (1.0 / per_blk))
    er = lax.broadcasted_iota(I32, (ne, ne), 0)
    ec = lax.broadcasted_iota(I32, (ne, ne), 1)
    before = jnp.where(er < ec, 1.0, 0.0).astype(BF16)
    start_blk = _dot_exact01(jnp.broadcast_to(nblk, (SUBLANES, ne)), before)[0:1, :]
    end_blk = start_blk + nblk
    dbase8_ref[...] = dpre8_ref[...] + (start_blk * float(per_blk)).astype(I32)
    blk = lax.broadcasted_iota(I32, be_ref.shape, 1).astype(F32)
    be = jnp.zeros(be_ref.shape, F32)
    for e in range(ne):
        be = be + jnp.where(blk >= end_blk[:, e:e + 1], 1.0, 0.0)
    be_ref[...] = jnp.minimum(be, float(ne - 1)).astype(I32)
    nv_ref[...] = jnp.broadcast_to(end_blk[:, ne - 1:ne], nv_ref.shape).astype(I32)


def _moe_tile(t):
    return 256 if t % 256 == 0 else LANES


def _plan_routes(te, ne, bm, nblocks, tt):
    t, k = te.shape
    assert t % tt == 0 and tt & (tt - 1) == 0
    ntiles = t // tt
    tab = pl.BlockSpec((None, 1, ne), lambda i: (i, 0, 0))
    tab_shape = jax.ShapeDtypeStruct((ntiles, 1, ne), I32)
    lp, n8, soff8, dpre8, tot8 = pl.pallas_call(
        _run_plan_kernel,
        grid=(ntiles,),
        in_specs=[pl.BlockSpec((tt, k), lambda i: (i, 0))],
        out_specs=[pl.BlockSpec((tt, k), lambda i: (i, 0)), tab, tab, tab,
                   pl.BlockSpec((1, ne), lambda i: (0, 0))],
        out_shape=[jax.ShapeDtypeStruct((t, k), I32), tab_shape, tab_shape, tab_shape,
                   jax.ShapeDtypeStruct((1, ne), F32)],
        scratch_shapes=[pltpu.VMEM((1, ne), F32)],
        compiler_params=_params(1),
        name="moe_run_plan",
    )(te)
    nbp = -(-nblocks // LANES) * LANES
    dbase8, be, nv = pl.pallas_call(
        functools.partial(_block_plan_kernel, bm=bm),
        out_shape=[jax.ShapeDtypeStruct((ntiles, ne), I32), jax.ShapeDtypeStruct((1, nbp), I32),
                   jax.ShapeDtypeStruct((1, LANES), I32)],
        name="moe_block_plan",
    )(tot8, dpre8.reshape(ntiles, ne))
    return lp, n8.reshape(-1), soff8.reshape(-1), dbase8.reshape(-1), be.reshape(-1), nv.reshape(-1)


def _for_each_run_chunk(tables, tile, ne, max8, fn):
    n8_ref, soff8_ref, dbase8_ref = tables

    def per_expert(e, carry):
        idx = tile * ne + e
        n, o, d = n8_ref[idx], soff8_ref[idx], dbase8_ref[idx]
        bit = max8
        while bit >= 1:
            @pl.when((n & bit) != 0)
            def _(o=o, d=d, bit=bit):
                fn(pl.multiple_of(o * SUBLANES, SUBLANES), pl.multiple_of(d * SUBLANES, SUBLANES), bit * SUBLANES)
            o = o + (n & bit)
            d = d + (n & bit)
            bit //= 2
        return carry

    lax.fori_loop(0, ne, per_expert, 0)


def _dispatch_runs_kernel(n8_ref, soff8_ref, dbase8_ref, lpt_ref, src_ref, xr_in_ref, xr_ref, xs, sem, *,
                          ne, tile_offset):
    del xr_in_ref
    tt = src_ref.shape[0]
    rows = xs.shape[0]
    r = lax.broadcasted_iota(I32, (rows, tt), 0)
    lpt = lpt_ref[...]
    sel = jnp.zeros((rows, tt), F32)
    for k in range(lpt.shape[0]):
        sel = sel + jnp.where(r == lpt[k:k + 1, :], 1.0, 0.0)
    xs[...] = _dot(sel.astype(BF16), src_ref[...].astype(BF16))

    def copy(o, d, n):
        return pltpu.make_async_copy(xs.at[pl.ds(o, n)], xr_ref.at[pl.ds(d, n)], sem)

    tables = (n8_ref, soff8_ref, dbase8_ref)
    tile = pl.program_id(0) + tile_offset
    _for_each_run_chunk(tables, tile, ne, tt // SUBLANES, lambda o, d, n: copy(o, d, n).start())
    _for_each_run_chunk(tables, tile, ne, tt // SUBLANES, lambda o, d, n: copy(o, d, n).wait())


def _dispatch_runs(tables, tile_offset, lpt, src, xr, ne, tt):
    t, d = src.shape
    assert t % tt == 0
    k = lpt.shape[0]
    grid_spec = pltpu.PrefetchScalarGridSpec(
        num_scalar_prefetch=3,
        grid=(t // tt,),
        in_specs=[pl.BlockSpec((k, tt), lambda i, *_: (0, i + tile_offset)),
                  pl.BlockSpec((tt, d), lambda i, *_: (i, 0)),
                  pl.BlockSpec(memory_space=pl.ANY)],
        out_specs=pl.BlockSpec(memory_space=pl.ANY),
        scratch_shapes=[pltpu.VMEM((k * tt + SUBLANES * ne, d), F32), pltpu.SemaphoreType.DMA(())],
    )
    return pl.pallas_call(
        functools.partial(_dispatch_runs_kernel, ne=ne, tile_offset=tile_offset),
        grid_spec=grid_spec,
        out_shape=jax.ShapeDtypeStruct(xr.shape, xr.dtype),
        input_output_aliases={5: 0},
        compiler_params=_params(1),
        name="moe_dispatch",
    )(*tables, lpt, src, xr)


def _combine_runs_kernel(n8_ref, soff8_ref, dbase8_ref, lp_ref, yr_ref, gate_ref, x1_ref, gt2_ref, g_ref,
                         o_ref, ys, sem, *, ne, tile_offset):
    tt = lp_ref.shape[0]
    rows = ys.shape[0]
    ys[rows - SUBLANES * ne:rows, :] = jnp.zeros((SUBLANES * ne, ys.shape[1]), F32)

    def copy(o, d, n):
        return pltpu.make_async_copy(yr_ref.at[pl.ds(d, n)], ys.at[pl.ds(o, n)], sem)

    tables = (n8_ref, soff8_ref, dbase8_ref)
    tile = pl.program_id(0) + tile_offset
    _for_each_run_chunk(tables, tile, ne, tt // SUBLANES, lambda o, d, n: copy(o, d, n).start())
    c = lax.broadcasted_iota(I32, (tt, rows), 1)
    lp = lp_ref[...]
    gate = gate_ref[...]
    mix = jnp.zeros((tt, rows), F32)
    for k in range(lp.shape[1]):
        mix = mix + jnp.where(c == lp[:, k:k + 1], gate[:, k:k + 1], 0.0)
    mix_hi, mix_lo = _split2(mix)
    _for_each_run_chunk(tables, tile, ne, tt // SUBLANES, lambda o, d, n: copy(o, d, n).wait())
    yb = ys[...].astype(BF16)
    f = _dot(mix_hi, yb) + _dot(mix_lo, yb)
    o_ref[...] = x1_ref[...] + gt2_ref[...] * _rms(f, g_ref[...])


def _combine_runs(tables, tile_offset, lp, yr, gate, x1, gt2, g, rows_per_batch, ne, tt):
    t, d = x1.shape
    k = lp.shape[1]
    assert t % tt == 0
    row = pl.BlockSpec((tt, d), lambda i, *_: (i, 0))
    grid_spec = pltpu.PrefetchScalarGridSpec(
        num_scalar_prefetch=3,
        grid=(t // tt,),
        in_specs=[pl.BlockSpec((tt, k), lambda i, *_: (i + tile_offset, 0)),
                  pl.BlockSpec(memory_space=pl.ANY),
                  pl.BlockSpec((tt, k), lambda i, *_: (i, 0)),
                  row, _mod_spec(gt2, tt, rows_per_batch),
                  pl.BlockSpec((1, d), lambda i, *_: (0, 0))],
        out_specs=row,
        scratch_shapes=[pltpu.VMEM((k * tt + SUBLANES * ne, d), F32), pltpu.SemaphoreType.DMA(())],
    )
    return pl.pallas_call(
        functools.partial(_combine_runs_kernel, ne=ne, tile_offset=tile_offset),
        grid_spec=grid_spec,
        out_shape=jax.ShapeDtypeStruct((t, d), F32),
        compiler_params=_params(1),
        name="moe_combine",
    )(*tables, lp, yr, gate, x1, gt2, g)


def _blockdiag_pairs(wa, wx, gw):
    nb, c, _ = wa.shape
    per = gw // c
    ng = nb // per
    eye = jnp.eye(per, dtype=wa.dtype)

    def bd(w):
        return jnp.einsum("gpcd,pq->gpcqd", w.reshape(ng, per, c, c), eye).reshape(ng, gw, gw)

    return jnp.concatenate([bd(wa), bd(wx)], axis=2).astype(BF16)


def _layer(xp, xs, ck, cv, clf, sconv, sh0, page_table, cp, cs, w):
    bp, s, d = xp.shape
    bs, ds_, _ = xs.shape
    nh = w["b_forget"].shape[0]
    hd = d // nh
    tp, ts = bp * s, bs * ds_
    kw = w["conv_w"].shape[0]
    ne = w["w_router"].shape[1]

    w_in = w["w_in"].astype(BF16)
    w_x, w_q, w_k, w_v = (w_in[:, n * d:(n + 1) * d] for n in range(4))
    w_ga, w_gb = w_in[:, 4 * d + nh:5 * d + nh], w_in[:, 5 * d + nh:]
    w4 = jnp.stack([w_x, w_q, w_ga, w_gb])
    wkvt = jnp.stack([w_k.T, w_v.T])
    w6 = jnp.stack([w_x, w_q, w_ga, w_gb, w_k, w_v])
    wft = w_in[:, 4 * d:4 * d + nh].T
    bfo = w["b_forget"].reshape(nh, 1)
    gw = min(2 * LANES, d)
    wbd = _blockdiag_pairs(w["rg_w_a"], w["rg_w_x"], gw)
    vec = lambda v: v.reshape(1, d)
    wo = w["w_out"].astype(BF16)
    wgu, wdn = w["w_gate_up"], w["w_down"]

    ada = _ada(jnp.concatenate([cp, cs], axis=0), w["w_ada"], w["b_ada"])
    mods_p = [m.reshape(bp, 1, d) for m in jnp.split(ada[:bp], 6, axis=-1)]
    mods_s = [jnp.repeat(m, ds_, axis=0) for m in jnp.split(ada[bp:], 6, axis=-1)]

    xpf, xsf = xp.reshape(tp, d), xs.reshape(ts, d)
    g_pre = vec(w["g_mix_pre"])
    zp, kt_p, vt_p, lft_p = _inproj(xpf, mods_p[1], mods_p[0], g_pre, w4, wkvt, wft, bfo, bp, s)
    zs, lft_s = _inproj(xsf, mods_s[1], mods_s[0], g_pre, w6, None, wft, bfo, 1, ds_)

    rnn_w = (w["conv_w"], vec(w["conv_b"]), wbd, vec(w["rg_b_a"]), vec(w["rg_b_x"]), vec(w["rg_lambda"]))
    zero_prev = jnp.zeros((bp, SUBLANES, d), F32)
    zero_h = jnp.zeros((bp, 1, d), F32)
    yr_p, conv_p, hl_p = _rnn_prompt(zp, zero_prev, zero_h, *rnn_w, reset_first=True)
    prev8 = jnp.pad(sconv, ((0, 0), (SUBLANES - (kw - 1), 0), (0, 0))).reshape(ts, d)
    h08 = jnp.pad(sh0[:, None, :], ((0, 0), (0, SUBLANES - 1), (0, 0))).reshape(ts, d)
    yr_s = _rnn_sample(zs, prev8, h08, *rnn_w)
    hl_s = yr_s.reshape(bs, ds_, d)[:, ds_ - 1]
    conv_s = zs[0].reshape(bs, ds_, d)[:, ds_ - (kw - 1):]

    ft_p = _cumsum_lanes(lft_p).reshape(bp, nh // 2, 2, s)
    ya_p = _fox_prompt(zp.reshape(4, bp, s, d), kt_p, vt_p, ft_p, hd).reshape(tp, d)
    cn_s = _cumsum_lanes(lft_s, seg=ds_).reshape(nh, bs, ds_).transpose(1, 0, 2)
    npool, page = ck.shape[0], ck.shape[1]
    ckt = ck.transpose(0, 2, 3, 1).reshape(npool, d, page)
    cvt = cv.transpose(0, 2, 3, 1).reshape(npool, d, page)
    ya_s = _fox_sample(page_table, zs, (1, 4, 5), cn_s, ckt, cvt, clf.transpose(0, 2, 1), nh, hd)

    post_w = (wo, vec(w["g_mix_post"]), vec(w["g_ffn_pre"]), w["w_router"], w["b_router"].reshape(1, ne))
    x1_p, h2_p, te_p, gate_p = _post(yr_p, ya_p, zp, xpf, mods_p[2], mods_p[4], mods_p[3], *post_w, s)
    x1_s, h2_s, te_s, gate_s = _post(yr_s, ya_s, zs, xsf, mods_s[2], mods_s[4], mods_s[3], *post_w, ds_)

    bm = 256
    t_all = tp + ts
    tt = _moe_tile(t_all)
    assert tp % tt == 0 and ts % tt == 0
    run_pad = (SUBLANES - 1) * (t_all // tt) * ne
    nblocks = -(-(t_all * TOP_K + run_pad) // bm) + ne
    lp, n8, soff8, dbase8, be, nv = _plan_routes(jnp.concatenate([te_p, te_s], axis=0), ne, bm, nblocks, tt)
    tables = (n8, soff8, dbase8)
    lpt = lp.T
    xr = jnp.zeros((nblocks * bm, d), F32)
    xr = _dispatch_runs(tables, 0, lpt, h2_p, xr, ne, tt)
    xr = _dispatch_runs(tables, tp // tt, lpt, h2_s, xr, ne, tt)
    yr = _experts(be, nv, xr, wgu, w["b_gate_up"], wdn, w["b_down"], bm)
    g_post = vec(w["g_ffn_post"])
    y_p = _combine_runs(tables, 0, lp, yr, gate_p, x1_p, mods_p[5], g_post, s, ne, tt)
    y_s = _combine_runs(tables, tp // tt, lp, yr, gate_s, x1_s, mods_s[5], g_post, ds_, ne, tt)

    heads_t = lambda zt_: zt_.reshape(bp, nh, hd, s).transpose(0, 3, 1, 2)
    out_p = (y_p.reshape(bp, s, d), heads_t(kt_p), heads_t(vt_p),
             lft_p.transpose(0, 2, 1), conv_p, hl_p.reshape(bp, d))
    out_s = (y_s.reshape(bs, ds_, d), zs[4].reshape(bs, ds_, nh, hd), zs[5].reshape(bs, ds_, nh, hd),
             lft_s[0].T.reshape(bs, ds_, nh), conv_s, hl_s)
    return out_p, out_s


def kernel(x_prompt, x_sample, cache_k, cache_v, cache_logf, state_conv, state_h, page_table, c_prompt, c_sample, w_ada, b_ada, g_mix_pre, g_mix_post, w_in, b_forget, conv_w, conv_b, rg_w_a, rg_b_a, rg_w_x, rg_b_x, rg_lambda, w_out, g_ffn_pre, g_ffn_post, w_router, b_router, w_gate_up, b_gate_up, w_down, b_down):
    weights = dict(w_ada=w_ada, b_ada=b_ada, g_mix_pre=g_mix_pre, g_mix_post=g_mix_post, w_in=w_in,
                   b_forget=b_forget, conv_w=conv_w, conv_b=conv_b, rg_w_a=rg_w_a, rg_b_a=rg_b_a,
                   rg_w_x=rg_w_x, rg_b_x=rg_b_x, rg_lambda=rg_lambda, w_out=w_out, g_ffn_pre=g_ffn_pre,
                   g_ffn_post=g_ffn_post, w_router=w_router, b_router=b_router, w_gate_up=w_gate_up,
                   b_gate_up=b_gate_up, w_down=w_down, b_down=b_down)
    depth = w_ada.shape[0]
    yp, ys = x_prompt, x_sample
    per_layer = []
    for l in range(depth):
        wl = {k: v[l] for k, v in weights.items()}
        out_p, out_s = _layer(yp, ys, cache_k[l], cache_v[l], cache_logf[l], state_conv[l], state_h[l],
                              page_table, c_prompt, c_sample, wl)
        yp, ys = out_p[0], out_s[0]
        per_layer.append(out_p[1:] + out_s[1:])
    stacked = [jnp.stack(leaf) for leaf in zip(*per_layer)]
    return (yp, ys, *stacked)
```

```python
import functools

import jax
import jax.numpy as jnp
from jax import lax
from jax.experimental import pallas as pl
from jax.experimental.pallas import tpu as pltpu

F32 = jnp.float32
BF16 = jnp.bfloat16
I32 = jnp.int32

EPS = 1e-6
LRU_C = 8.0
TOP_K = 4
SWIGLU_LIMIT = 7.0
SWIGLU_ALPHA = 1.702
NEG_BIG = -1e30

LANES = 128
SUBLANES = 8
VMEM_LIMIT = 56 * 1024 * 1024


def _params(n_grid_dims):
    return pltpu.CompilerParams(
        dimension_semantics=("arbitrary",) * n_grid_dims, vmem_limit_bytes=VMEM_LIMIT)


def _dot(a, b):
    return jnp.dot(a, b, preferred_element_type=F32)


def _dot_nt(a, b):
    return lax.dot_general(a, b, (((1,), (1,)), ((), ())), preferred_element_type=F32)


def _split2(x):
    hi = x.astype(BF16)
    lo = (x - hi.astype(F32)).astype(BF16)
    return hi, lo


def _split3(x):
    p1 = x.astype(BF16)
    r1 = x - p1.astype(F32)
    p2 = r1.astype(BF16)
    p3 = (r1 - p2.astype(F32)).astype(BF16)
    return p1, p2, p3


def _dot3(a, b):
    ah, al = _split2(a)
    bh, bl = _split2(b)
    return _dot(ah, bh) + _dot(ah, bl) + _dot(al, bh)


def _dot_exact01(x, u01):
    p1, p2, p3 = _split3(x)
    return _dot(p1, u01) + _dot(p2, u01) + _dot(p3, u01)


def _rms(x, g):
    ms = jnp.mean(x * x, axis=-1, keepdims=True)
    return x * lax.rsqrt(ms + EPS) * g


def _softplus(x):
    return jnp.maximum(x, 0.0) + jnp.log1p(jnp.exp(-jnp.abs(x)))


def _log_sigmoid(x):
    return jnp.minimum(x, 0.0) - jnp.log1p(jnp.exp(-jnp.abs(x)))


def _ada_kernel(c_ref, w_ref, b_ref, o_ref):
    c = c_ref[...]
    o_ref[...] = _dot3(c * jax.nn.sigmoid(c), w_ref[...]) + b_ref[...]


def _ada(c, w, b):
    n, d = c.shape
    nout = w.shape[1]
    tn = min(nout, 1536)
    assert nout % tn == 0
    return pl.pallas_call(
        _ada_kernel,
        grid=(nout // tn,),
        in_specs=[pl.BlockSpec((n, d), lambda j: (0, 0)),
                  pl.BlockSpec((d, tn), lambda j: (0, j)),
                  pl.BlockSpec((1, tn), lambda j: (0, j))],
        out_specs=pl.BlockSpec((n, tn), lambda j: (0, j)),
        out_shape=jax.ShapeDtypeStruct((n, nout), F32),
        compiler_params=_params(1),
        name="ada",
    )(c, w, b.reshape(1, nout))


def _inproj_kernel(x_ref, sc_ref, sh_ref, g_ref, w_ref, wft_ref, bf_ref, *rest, nz, nt):
    if nt:
        wt_ref, z_ref = rest[:2]
        zt_refs = rest[2:2 + nt]
    else:
        z_ref = rest[0]
    lft_ref, h_scr = rest[-2:]
    j = pl.program_id(1)

    @pl.when(j == 0)
    def _():
        h = _rms(x_ref[...], g_ref[...]) * (1.0 + sc_ref[...]) + sh_ref[...]
        hb = h.astype(BF16)
        h_scr[...] = hb
        lft_ref[...] = _log_sigmoid(_dot_nt(wft_ref[...], hb) + bf_ref[...])

    if nt:
        @pl.when(j < nz)
        def _():
            z_ref[...] = _dot(h_scr[...], w_ref[jnp.minimum(j, nz - 1)])

        for n in range(nt):
            @pl.when(j == nz + n)
            def _(n=n):
                zt_refs[n][...] = _dot_nt(wt_ref[n], h_scr[...])
    else:
        z_ref[...] = _dot(h_scr[...], w_ref[j])


def _mod_spec(mod, tm, rows_per_batch):
    if mod.ndim == 3:
        tiles_per_batch = rows_per_batch // tm
        return pl.BlockSpec((None, 1, mod.shape[-1]), lambda i, *_: (i // tiles_per_batch, 0, 0))
    return pl.BlockSpec((tm, mod.shape[-1]), lambda i, *_: (i, 0))


def _inproj(x, sc, sh, g, w, wt, wft, bfo, nbatch, rows_per_batch):
    t, d = x.shape
    nh = wft.shape[0]
    nz = w.shape[0]
    nt = 0 if wt is None else wt.shape[0]
    s = t // nbatch
    tm = min(1024, s)
    assert s % tm == 0 and (sc.ndim == 2 or rows_per_batch % tm == 0)
    tpb = s // tm
    resident = lambda n: pl.BlockSpec((n, d, d), lambda i, j: (0, 0, 0), pipeline_mode=pl.Buffered(1))
    in_specs = [pl.BlockSpec((tm, d), lambda i, j: (i, 0)),
                _mod_spec(sc, tm, rows_per_batch), _mod_spec(sh, tm, rows_per_batch),
                pl.BlockSpec((1, d), lambda i, j: (0, 0)),
                resident(nz),
                pl.BlockSpec((nh, d), lambda i, j: (0, 0)),
                pl.BlockSpec((nh, 1), lambda i, j: (0, 0))]
    out_specs = [pl.BlockSpec((None, tm, d), lambda i, j: (jnp.minimum(j, nz - 1), i, 0))]
    out_shape = [jax.ShapeDtypeStruct((nz, t, d), F32)]
    args = [x, sc, sh, g, w, wft, bfo]
    if nt:
        in_specs.append(resident(nt))
        args.append(wt)
        for _ in range(nt):
            out_specs.append(pl.BlockSpec((None, d, tm), lambda i, j: (i // tpb, 0, i % tpb)))
            out_shape.append(jax.ShapeDtypeStruct((nbatch, d, s), F32))
    out_specs.append(pl.BlockSpec((None, nh, tm), lambda i, j: (i // tpb, 0, i % tpb)))
    out_shape.append(jax.ShapeDtypeStruct((nbatch, nh, s), F32))
    return pl.pallas_call(
        functools.partial(_inproj_kernel, nz=nz, nt=nt),
        grid=(t // tm, nz + nt),
        in_specs=in_specs,
        out_specs=out_specs,
        out_shape=out_shape,
        scratch_shapes=[pltpu.VMEM((tm, d), BF16)],
        compiler_params=_params(2),
        name="inproj",
    )(*args)


def _cumsum_kernel(lf_ref, o_ref, *, cw, seg):
    n = lf_ref.shape[1]
    r = lax.broadcasted_iota(I32, (cw, cw), 0)
    c = lax.broadcasted_iota(I32, (cw, cw), 1)
    keep = r <= c
    if seg is not None:
        keep = jnp.logical_and(keep, (r // seg) == (c // seg))
    u01 = jnp.where(keep, 1.0, 0.0).astype(BF16)
    carry = jnp.zeros((lf_ref.shape[0], 1), F32)
    for i in range(n // cw):
        f = _dot_exact01(lf_ref[:, i * cw:(i + 1) * cw], u01)
        if seg is None:
            f = f + carry
            carry = f[:, cw - 1:cw]
        o_ref[:, i * cw:(i + 1) * cw] = f


def _cumsum_lanes(lft, seg=None):
    nb, nh, s = lft.shape
    block = s if seg is None else min(s, 2 * LANES)
    cw = min(2 * LANES, block)
    assert s % block == 0 and block % cw == 0 and (seg is None or cw % seg == 0)
    spec = pl.BlockSpec((None, nh, block), lambda b, n: (b, 0, n))
    return pl.pallas_call(
        functools.partial(_cumsum_kernel, cw=cw, seg=seg),
        grid=(nb, s // block),
        in_specs=[spec],
        out_specs=spec,
        out_shape=jax.ShapeDtypeStruct((nb, nh, s), F32),
        compiler_params=_params(2),
        name="cumsum_logf",
    )(lft)


def _rglru_coeffs(xc, wbd_ref, ba, bx, lam):
    ng, gw = wbd_ref.shape[0], wbd_ref.shape[1]
    xb = xc.astype(BF16)
    ra, ri = [], []
    for g in range(ng):
        o = _dot(xb[:, g * gw:(g + 1) * gw], wbd_ref[g])
        ra.append(o[:, :gw])
        ri.append(o[:, gw:])
    r = jax.nn.sigmoid((ra[0] if ng == 1 else jnp.concatenate(ra, axis=1)) + ba)
    i = jax.nn.sigmoid((ri[0] if ng == 1 else jnp.concatenate(ri, axis=1)) + bx)
    log_a = -LRU_C * r * _softplus(-lam)
    a = jnp.exp(log_a)
    th = jnp.tanh(log_a)
    mult = jnp.sqrt(-2.0 * th / (1.0 - th))
    return a, mult, i


def _scan8(a8, u8, row):
    for s in (1, 2, 4):
        a_sh = pltpu.roll(a8, s, axis=0)
        u_sh = pltpu.roll(u8, s, axis=0)
        m = row >= s
        u8 = u8 + a8 * jnp.where(m, u_sh, 0.0)
        a8 = a8 * jnp.where(m, a_sh, 1.0)
    return a8, u8


def _rnn_prompt_kernel(x_ref, prev_ref, h0_ref, cw_ref, cb_ref, wbd_ref, ba_ref, bx_ref, lam_ref,
                       y_ref, cs_ref, hl_ref, ext_scr, a_scr, u_scr, h_scr, *, reset_first):
    c = pl.program_id(1)
    tc, d = x_ref.shape
    kw = cw_ref.shape[0]

    @pl.when(c == 0)
    def _():
        ext_scr[0:SUBLANES, :] = prev_ref[...]
        h_scr[...] = h0_ref[...]

    ext_scr[SUBLANES:SUBLANES + tc, :] = x_ref[...]
    w = cw_ref[...]
    xc = cb_ref[...]
    for j in range(kw):
        off = SUBLANES - (kw - 1) + j
        xc = xc + w[j:j + 1, :] * ext_scr[off:off + tc, :]
    ext_scr[0:SUBLANES, :] = ext_scr[tc:tc + SUBLANES, :]

    a, mult, i = _rglru_coeffs(xc, wbd_ref, ba_ref[...], bx_ref[...], lam_ref[...])
    if reset_first:
        row = lax.broadcasted_iota(I32, (tc, 1), 0)
        mult = jnp.where(row == jnp.where(c == 0, 0, -1), 1.0, mult)
    a_scr[...] = a
    u_scr[...] = mult * i * xc

    row8 = lax.broadcasted_iota(I32, (SUBLANES, d), 0)

    def body(g, h):
        r0 = pl.multiple_of(g * SUBLANES, SUBLANES)
        a8, u8 = _scan8(a_scr[pl.ds(r0, SUBLANES), :], u_scr[pl.ds(r0, SUBLANES), :], row8)
        h8 = u8 + a8 * h
        y_ref[pl.ds(r0, SUBLANES), :] = h8
        return h8[SUBLANES - 1:SUBLANES, :]

    h = lax.fori_loop(0, tc // SUBLANES, body, h_scr[...], unroll=4)
    h_scr[...] = h

    @pl.when(c == pl.num_programs(1) - 1)
    def _():
        hl_ref[...] = h
        cs_ref[...] = x_ref[tc - (kw - 1):tc, :]


def _rnn_prompt(x3, prev8, h0, cw, cb, wbd, ba, bx, lam, reset_first):
    b = prev8.shape[0]
    t, d = x3.shape[1], x3.shape[2]
    s = t // b
    kw = cw.shape[0]
    tc = min(256, s)
    nc = s // tc
    assert s % tc == 0 and kw - 1 <= SUBLANES and tc % SUBLANES == 0
    vec = pl.BlockSpec((1, d), lambda bi, ci: (0, 0))
    return pl.pallas_call(
        functools.partial(_rnn_prompt_kernel, reset_first=reset_first),
        grid=(b, nc),
        in_specs=[pl.BlockSpec((None, tc, d), lambda bi, ci: (0, bi * nc + ci, 0)),
                  pl.BlockSpec((None, SUBLANES, d), lambda bi, ci: (bi, 0, 0)),
                  pl.BlockSpec((None, 1, d), lambda bi, ci: (bi, 0, 0)),
                  pl.BlockSpec((kw, d), lambda bi, ci: (0, 0)), vec,
                  pl.BlockSpec(wbd.shape, lambda bi, ci: (0, 0, 0)), vec, vec, vec],
        out_specs=[pl.BlockSpec((tc, d), lambda bi, ci: (bi * nc + ci, 0)),
                   pl.BlockSpec((None, kw - 1, d), lambda bi, ci: (bi, 0, 0)),
                   pl.BlockSpec((None, 1, d), lambda bi, ci: (bi, 0, 0))],
        out_shape=[jax.ShapeDtypeStruct((t, d), F32),
                   jax.ShapeDtypeStruct((b, kw - 1, d), F32),
                   jax.ShapeDtypeStruct((b, 1, d), F32)],
        scratch_shapes=[pltpu.VMEM((tc + SUBLANES, d), F32), pltpu.VMEM((tc, d), F32),
                        pltpu.VMEM((tc, d), F32), pltpu.VMEM((1, d), F32)],
        compiler_params=_params(2),
        name="rnn_prompt",
    )(x3, prev8, h0, cw, cb, wbd, ba, bx, lam)


def _rnn_sample_kernel(x_ref, prev_ref, h0_ref, cw_ref, cb_ref, wbd_ref, ba_ref, bx_ref, lam_ref,
                       y_ref):
    r, d = x_ref.shape
    kw = cw_ref.shape[0]
    x = x_ref[...]
    p = prev_ref[...]
    step = lax.broadcasted_iota(I32, (r, 1), 0) & (SUBLANES - 1)
    w = cw_ref[...]
    xc = cb_ref[...]
    for j in range(kw):
        back = kw - 1 - j
        if back == 0:
            xs = x
        else:
            xs = jnp.where(step >= back, pltpu.roll(x, back, axis=0),
                           pltpu.roll(p, (r - SUBLANES + back) % r, axis=0))
        xc = xc + w[j:j + 1, :] * xs
    a, mult, i = _rglru_coeffs(xc, wbd_ref, ba_ref[...], bx_ref[...], lam_ref[...])
    u = mult * i * xc + a * h0_ref[...]
    for s in (1, 2, 4):
        m = step >= s
        a_sh = pltpu.roll(a, s, axis=0)
        u_sh = pltpu.roll(u, s, axis=0)
        u = u + a * jnp.where(m, u_sh, 0.0)
        a = a * jnp.where(m, a_sh, 1.0)
    y_ref[...] = u


def _rnn_sample(x3, prev8, h08, cw, cb, wbd, ba, bx, lam):
    t, d = x3.shape[1], x3.shape[2]
    nb_total = t // SUBLANES
    kw = cw.shape[0]
    nb = min(16, nb_total)
    r = nb * SUBLANES
    assert nb_total % nb == 0 and nb % SUBLANES == 0
    vec = pl.BlockSpec((1, d), lambda i: (0, 0))
    return pl.pallas_call(
        _rnn_sample_kernel,
        grid=(nb_total // nb,),
        in_specs=[pl.BlockSpec((None, r, d), lambda i: (0, i, 0)),
                  pl.BlockSpec((r, d), lambda i: (i, 0)),
                  pl.BlockSpec((r, d), lambda i: (i, 0)),
                  pl.BlockSpec((kw, d), lambda i: (0, 0)), vec,
                  pl.BlockSpec(wbd.shape, lambda i: (0, 0, 0)), vec, vec, vec],
        out_specs=pl.BlockSpec((r, d), lambda i: (i, 0)),
        out_shape=jax.ShapeDtypeStruct((t, d), F32),
        compiler_params=_params(1),
        name="rnn_sample",
    )(x3, prev8, h08, cw, cb, wbd, ba, bx, lam)


def _fox_prompt_kernel(q_ref, kt_ref, vt_ref, f_ref, o_ref, *, hd, scale):
    i = pl.program_id(2)
    tq = q_ref.shape[0]
    lane = lax.broadcasted_iota(I32, (1, 2 * hd), 1)
    q = q_ref[...] * scale
    qs = jnp.concatenate([jnp.where(lane < hd, q, 0.0), jnp.where(lane >= hd, q, 0.0)], axis=0).astype(BF16)

    def scores(kb):
        k0 = pl.multiple_of(kb * tq, tq)
        kt = kt_ref[:, pl.ds(k0, tq)].astype(BF16)
        fb = f_ref[:, pl.ds(k0, tq)]
        s = _dot(qs, kt)
        return jnp.concatenate([s[:tq] - fb[0:1, :], s[tq:] - fb[1:2, :]], axis=0)

    vrow = lax.broadcasted_iota(I32, (2 * hd, 1), 0)

    def update(kb, s, m, acc):
        k0 = pl.multiple_of(kb * tq, tq)
        vt = vt_ref[:, pl.ds(k0, tq)]
        vt0 = jnp.where(vrow < hd, vt, 1.0).astype(BF16)
        vt1 = jnp.where(vrow >= hd, vt, 1.0).astype(BF16)
        m_new = jnp.maximum(m, jnp.max(s, axis=-1, keepdims=True))
        p = jnp.exp(s - m_new).astype(BF16)
        pv = jnp.concatenate([_dot_nt(p[:tq], vt0), _dot_nt(p[tq:], vt1)], axis=0)
        return m_new, jnp.exp(m - m_new) * acc + pv

    def body(kb, carry):
        s, m, acc = carry
        return (scores(kb + 1),) + update(kb, s, m, acc)

    init = (scores(0), jnp.full((2 * tq, 1), NEG_BIG, F32), jnp.zeros((2 * tq, 2 * hd), F32))
    s, m, acc = lax.fori_loop(0, i, body, init)
    r = lax.broadcasted_iota(I32, (2 * tq, tq), 0) & (tq - 1)
    c = lax.broadcasted_iota(I32, (2 * tq, tq), 1)
    _, acc = update(i, jnp.where(c <= r, s, NEG_BIG), m, acc)
    o = acc / pltpu.roll(acc, hd, axis=1)
    o_ref[...] = jnp.where(lane < hd, o[:tq], o[tq:])


def _fox_prompt(z4, kt, vt, ft4, hd):
    _, b, s, d = z4.shape
    assert 2 * hd == LANES and d % LANES == 0
    npair = d // LANES
    tq = min(256, s)
    assert s % tq == 0 and tq & (tq - 1) == 0
    return pl.pallas_call(
        functools.partial(_fox_prompt_kernel, hd=hd, scale=hd ** -0.5),
        grid=(b, npair, s // tq),
        in_specs=[pl.BlockSpec((None, None, tq, LANES), lambda bi, j, i: (1, bi, i, j)),
                  pl.BlockSpec((None, LANES, s), lambda bi, j, i: (bi, j, 0)),
                  pl.BlockSpec((None, LANES, s), lambda bi, j, i: (bi, j, 0)),
                  pl.BlockSpec((None, None, 2, s), lambda bi, j, i: (bi, j, 0, 0))],
        out_specs=pl.BlockSpec((None, tq, LANES), lambda bi, j, i: (bi, i, j)),
        out_shape=jax.ShapeDtypeStruct((b, s, d), F32),
        compiler_params=_params(3),
        name="fox_prompt",
    )(z4, kt, vt, ft4)


def _fox_sample_kernel(pt_ref, q_ref, kn_ref, vn_ref, cn_ref, *rest, nh, hd, scale, pps, grp):
    del pt_ref
    k_refs, v_refs, lf_refs = rest[:pps], rest[pps:2 * pps], rest[2 * pps:3 * pps]
    o_ref, qx_scr, m_scr, l_scr, acc_scr, fcar_scr = rest[3 * pps:]
    j = pl.program_id(1)
    ds_, d = q_ref.shape
    page = lf_refs[0].shape[1]
    nr = nh * ds_

    def expand(x):
        return jnp.broadcast_to(x[:, None, :], (nh, ds_, x.shape[1])).reshape(nr, x.shape[1])

    def own_head():
        rh = lax.broadcasted_iota(I32, (nr, d), 0) // ds_
        ch = lax.broadcasted_iota(I32, (nr, d), 1) // hd
        return rh == ch

    @pl.when(j == 0)
    def _():
        q = q_ref[...] * scale
        qt = jnp.broadcast_to(q[None], (nh, ds_, d)).reshape(nr, d)
        qx_scr[...] = jnp.where(own_head(), qt, 0.0).astype(BF16)
        m_scr[...] = jnp.full(m_scr.shape, NEG_BIG, F32)
        l_scr[...] = jnp.zeros(l_scr.shape, F32)
        acc_scr[...] = jnp.zeros(acc_scr.shape, F32)
        fcar_scr[...] = jnp.zeros(fcar_scr.shape, F32)

    def update(s, pv_fn):
        m_old = m_scr[...]
        m_new = jnp.maximum(m_old, jnp.max(s, axis=-1, keepdims=True))
        p = jnp.exp(s - m_new)
        alpha = jnp.exp(m_old - m_new)
        l_scr[...] = alpha * l_scr[...] + jnp.sum(p, axis=-1, keepdims=True)
        m_scr[...] = m_new
        acc_scr[...] = alpha * acc_scr[...] + pv_fn(p.astype(BF16))

    r = lax.broadcasted_iota(I32, (page, page), 0)
    c = lax.broadcasted_iota(I32, (page, page), 1)
    u01 = jnp.where(r <= c, 1.0, 0.0).astype(BF16)
    qx = qx_scr[...]
    fcar = fcar_scr[...]
    for g0 in range(0, pps, grp):
        pages = range(g0, min(g0 + grp, pps))
        s_parts = []
        for pi in pages:
            ft = _dot_exact01(lf_refs[pi][...], u01)
            s_parts.append(_dot(qx, k_refs[pi][...].astype(BF16)) - expand(ft + fcar))
            fcar = fcar + ft[:, page - 1:page]
        s = s_parts[0] if len(s_parts) == 1 else jnp.concatenate(s_parts, axis=1)

        def pv_pages(pb, pages=pages):
            out = None
            for n, pi in enumerate(pages):
                term = _dot_nt(pb[:, n * page:(n + 1) * page], v_refs[pi][...].astype(BF16))
                out = term if out is None else out + term
            return out

        update(s, pv_pages)
    fcar_scr[...] = fcar

    @pl.when(j == pl.num_programs(1) - 1)
    def _():
        fn = expand(cn_ref[...] + fcar_scr[...])
        sn = _dot_nt(qx_scr[...], kn_ref[...].astype(BF16)) - fn
        qi = lax.broadcasted_iota(I32, (nr, ds_), 0) & (ds_ - 1)
        key = lax.broadcasted_iota(I32, (nr, ds_), 1)
        sn = jnp.where(key <= qi, sn, NEG_BIG)
        update(sn, lambda pb: _dot(pb, vn_ref[...].astype(BF16)))
        o = jnp.where(own_head(), acc_scr[...] / l_scr[...], 0.0)
        o_ref[...] = o.reshape(nh, ds_, d).sum(axis=0)


def _fox_sample(page_table, z3, slabs, cnt, ckt, cvt, clft, nh, hd):
    b, npages = page_table.shape
    t, d = z3.shape[1], z3.shape[2]
    ds_ = t // b
    page = clft.shape[2]
    pps = next(n for n in (8, 4, 2, 1) if npages % n == 0)
    grp = min(4, pps)
    nr = nh * ds_
    assert ds_ == SUBLANES and ckt.shape[1] == d
    tok = lambda slab: pl.BlockSpec((None, ds_, d), lambda bi, j, pt: (slab, bi, 0))

    def pg(rows, pi):
        return pl.BlockSpec((None, rows, page), lambda bi, j, pt: (pt[bi * npages + j * pps + pi], 0, 0))

    in_specs = [tok(slabs[0]), tok(slabs[1]), tok(slabs[2]),
                pl.BlockSpec((None, nh, ds_), lambda bi, j, pt: (bi, 0, 0))]
    in_specs += [pg(d, pi) for pi in range(pps)] * 2 + [pg(nh, pi) for pi in range(pps)]
    grid_spec = pltpu.PrefetchScalarGridSpec(
        num_scalar_prefetch=1,
        grid=(b, npages // pps),
        in_specs=in_specs,
        out_specs=pl.BlockSpec((ds_, d), lambda bi, j, pt: (bi, 0)),
        scratch_shapes=[pltpu.VMEM((nr, d), BF16), pltpu.VMEM((nr, 1), F32), pltpu.VMEM((nr, 1), F32),
                        pltpu.VMEM((nr, d), F32), pltpu.VMEM((nh, 1), F32)],
    )
    return pl.pallas_call(
        functools.partial(_fox_sample_kernel, nh=nh, hd=hd, scale=hd ** -0.5, pps=pps, grp=grp),
        grid_spec=grid_spec,
        out_shape=jax.ShapeDtypeStruct((t, d), F32),
        compiler_params=_params(2),
        name="fox_sample",
    )(page_table.reshape(-1), z3, z3, z3, cnt, *([ckt] * pps), *([cvt] * pps), *([clft] * pps))


def _post_kernel(yr_ref, ya_ref, ga_ref, gb_ref, x_ref, gt1_ref, sc2_ref, sh2_ref, wo_ref, gpost_ref,
                 gpre_ref, wr_ref, br_ref, x1_ref, h2_ref, te_ref, gate_ref):
    merged = jax.nn.sigmoid(ga_ref[...]) * yr_ref[...] + jax.nn.sigmoid(gb_ref[...]) * ya_ref[...]
    out = _dot(merged.astype(BF16), wo_ref[...])
    x1 = x_ref[...] + gt1_ref[...] * _rms(out, gpost_ref[...])
    x1_ref[...] = x1
    h2 = _rms(x1, gpre_ref[...]) * (1.0 + sc2_ref[...]) + sh2_ref[...]
    h2_ref[...] = h2
    logits = _dot3(h2, wr_ref[...]) + br_ref[...]
    ne = logits.shape[1]
    lane = lax.broadcasted_iota(I32, logits.shape, 1)
    vals, idxs = [], []
    for _ in range(TOP_K):
        m = jnp.max(logits, axis=1, keepdims=True)
        idx = jnp.min(jnp.where(logits == m, lane, ne), axis=1, keepdims=True)
        vals.append(m)
        idxs.append(idx)
        logits = jnp.where(lane == idx, -jnp.inf, logits)
    e = jnp.exp(jnp.concatenate(vals, axis=1) - vals[0])
    gate_ref[...] = e / jnp.sum(e, axis=1, keepdims=True)
    te_ref[...] = jnp.concatenate(idxs, axis=1)


def _post(y_rnn, y_attn, z3, x, gt1, sc2, sh2, wo, gpost, gpre, wr, br, rows_per_batch):
    t, d = x.shape
    ne = wr.shape[1]
    tm = min(256, rows_per_batch if gt1.ndim == 3 else t)
    assert t % tm == 0
    row = pl.BlockSpec((tm, d), lambda i: (i, 0))
    vec = pl.BlockSpec((1, d), lambda i: (0, 0))
    mod = lambda m: _mod_spec(m, tm, rows_per_batch)
    return pl.pallas_call(
        _post_kernel,
        grid=(t // tm,),
        in_specs=[row, row,
                  pl.BlockSpec((None, tm, d), lambda i: (2, i, 0)),
                  pl.BlockSpec((None, tm, d), lambda i: (3, i, 0)),
                  row, mod(gt1), mod(sc2), mod(sh2),
                  pl.BlockSpec((d, d), lambda i: (0, 0)), vec, vec,
                  pl.BlockSpec((d, ne), lambda i: (0, 0)),
                  pl.BlockSpec((1, ne), lambda i: (0, 0))],
        out_specs=[row, row,
                   pl.BlockSpec((tm, TOP_K), lambda i: (i, 0)),
                   pl.BlockSpec((tm, TOP_K), lambda i: (i, 0))],
        out_shape=[jax.ShapeDtypeStruct((t, d), F32), jax.ShapeDtypeStruct((t, d), F32),
                   jax.ShapeDtypeStruct((t, TOP_K), I32), jax.ShapeDtypeStruct((t, TOP_K), F32)],
        compiler_params=_params(1),
        name="post_mixer",
    )(y_rnn, y_attn, z3, z3, x, gt1, sc2, sh2, wo, gpost, gpre, wr, br)


def _multi_hot(te, ne):
    lane = lax.broadcasted_iota(I32, (te.shape[0], ne), 1)
    hot = jnp.zeros((te.shape[0], ne), F32)
    for k in range(te.shape[1]):
        hot = hot + jnp.where(lane == te[:, k:k + 1], 1.0, 0.0)
    return hot


def _expert_kernel(be_ref, nv_ref, x_ref, wgu_ref, bgu_ref, wd_ref, bd_ref, o_ref, wgu_bf, wd_bf, *, fc):
    i = pl.program_id(0)
    f = wd_ref.shape[0]

    @pl.when(jnp.logical_or(i == 0, be_ref[i] != be_ref[jnp.maximum(i - 1, 0)]))
    def _():
        for c in range(2 * f // fc):
            wgu_bf[:, c * fc:(c + 1) * fc] = wgu_ref[:, c * fc:(c + 1) * fc].astype(BF16)
        for c in range(f // fc):
            wd_bf[c * fc:(c + 1) * fc, :] = wd_ref[c * fc:(c + 1) * fc, :].astype(BF16)

    @pl.when(i < nv_ref[0])
    def _():
        xb = x_ref[...].astype(BF16)
        acc = jnp.zeros(o_ref.shape, F32)
        for c in range(f // fc):
            glu = _dot(xb, wgu_bf[:, c * fc:(c + 1) * fc]) + bgu_ref[:, c * fc:(c + 1) * fc]
            lin = _dot(xb, wgu_bf[:, f + c * fc:f + (c + 1) * fc]) + bgu_ref[:, f + c * fc:f + (c + 1) * fc]
            glu = jnp.minimum(glu, SWIGLU_LIMIT)
            lin = jnp.clip(lin, -SWIGLU_LIMIT, SWIGLU_LIMIT)
            act = glu * jax.nn.sigmoid(SWIGLU_ALPHA * glu) * (lin + 1.0)
            acc = acc + _dot(act.astype(BF16), wd_bf[c * fc:(c + 1) * fc, :])
        o_ref[...] = acc + bd_ref[...]

    @pl.when(i >= nv_ref[0])
    def _():
        o_ref[...] = jnp.zeros(o_ref.shape, F32)


def _experts(be, nv, xr, wgu, bgu, wd, bd, bm):
    nr, d = xr.shape
    ne, _, f2 = wgu.shape
    f = f2 // 2
    fc = min(512, f)
    grid_spec = pltpu.PrefetchScalarGridSpec(
        num_scalar_prefetch=2,
        grid=(nr // bm,),
        in_specs=[pl.BlockSpec((bm, d), lambda i, be, nv: (jnp.minimum(i, nv[0] - 1), 0)),
                  pl.BlockSpec((None, d, f2), lambda i, be, nv: (be[i], 0, 0)),
                  pl.BlockSpec((None, 1, f2), lambda i, be, nv: (be[i], 0, 0)),
                  pl.BlockSpec((None, f, d), lambda i, be, nv: (be[i], 0, 0)),
                  pl.BlockSpec((None, 1, d), lambda i, be, nv: (be[i], 0, 0))],
        out_specs=pl.BlockSpec((bm, d), lambda i, be, nv: (i, 0)),
        scratch_shapes=[pltpu.VMEM((d, f2), BF16), pltpu.VMEM((f, d), BF16)],
    )
    return pl.pallas_call(
        functools.partial(_expert_kernel, fc=fc),
        grid_spec=grid_spec,
        out_shape=jax.ShapeDtypeStruct((nr, d), F32),
        compiler_params=_params(1),
        name="moe_experts",
    )(be, nv, xr, wgu, bgu.reshape(ne, 1, f2), wd, bd.reshape(ne, 1, d))


def _run_plan_kernel(te_ref, lp_ref, n8_ref, soff8_ref, dpre8_ref, tot8_ref, carry_scr):
    i = pl.program_id(0)
    tt, ne = te_ref.shape[0], n8_ref.shape[1]

    @pl.when(i == 0)
    def _():
        carry_scr[...] = jnp.zeros(carry_scr.shape, F32)

    te = te_ref[...]
    hot = _multi_hot(te, ne)
    r = lax.broadcasted_iota(I32, (tt, tt), 0)
    c = lax.broadcasted_iota(I32, (tt, tt), 1)
    below = jnp.where(c < r, 1.0, 0.0).astype(BF16)
    rank = _dot(below, hot.astype(BF16))
    n8 = jnp.ceil(jnp.sum(hot, axis=0, keepdims=True) * (1.0 / SUBLANES))
    er = lax.broadcasted_iota(I32, (ne, ne), 0)
    ec = lax.broadcasted_iota(I32, (ne, ne), 1)
    before = jnp.where(er < ec, 1.0, 0.0).astype(BF16)
    soff8 = _dot_exact01(jnp.broadcast_to(n8, (SUBLANES, ne)), before)[0:1, :]
    lane = lax.broadcasted_iota(I32, (tt, ne), 1)
    pos = rank + soff8 * float(SUBLANES)
    cols = [jnp.sum(jnp.where(lane == te[:, k:k + 1], pos, 0.0), axis=1, keepdims=True) for k in range(te.shape[1])]
    lp_ref[...] = jnp.concatenate(cols, axis=1).astype(I32)
    n8_ref[...] = n8.astype(I32)
    soff8_ref[...] = soff8.astype(I32)
    dpre8_ref[...] = carry_scr[...].astype(I32)
    carry_scr[...] = carry_scr[...] + n8
    tot8_ref[...] = carry_scr[...]


def _block_plan_kernel(tot8_ref, dpre8_ref, dbase8_ref, tail8_ref, ntail8_ref, be_ref, nv_ref, *, bm):
    ne = tot8_ref.shape[1]
    per_blk = bm // SUBLANES
    nblk = jnp.ceil(tot8_ref[...] * (1.0 / per_blk))
    er = lax.broadcasted_iota(I32, (ne, ne), 0)
    ec = lax.broadcasted_iota(I32, (ne, ne), 1)
    before = jnp.where(er < ec, 1.0, 0.0).astype(BF16)
    start_blk = _dot_exact01(jnp.broadcast_to(nblk, (SUBLANES, ne)), before)[0:1, :]
    end_blk = start_blk + nblk
    dbase8_ref[...] = dpre8_ref[...] + (start_blk * float(per_blk)).astype(I32)
    tail8_ref[...] = (start_blk * float(per_blk) + tot8_ref[...]).astype(I32)
    ntail8_ref[...] = (nblk * float(per_blk) - tot8_ref[...]).astype(I32)
    blk = lax.broadcasted_iota(I32, be_ref.shape, 1).astype(F32)
    be = jnp.zeros(be_ref.shape, F32)
    for e in range(ne):
        be = be + jnp.where(blk >= end_blk[:, e:e + 1], 1.0, 0.0)
    be_ref[...] = jnp.minimum(be, float(ne - 1)).astype(I32)
    nv_ref[...] = jnp.broadcast_to(end_blk[:, ne - 1:ne], nv_ref.shape).astype(I32)


def _moe_tile(t):
    return 256 if t % 256 == 0 else LANES


def _plan_routes(te, ne, bm, nblocks, tt):
    t, k = te.shape
    assert t % tt == 0 and tt & (tt - 1) == 0
    ntiles = t // tt
    tab = pl.BlockSpec((None, 1, ne), lambda i: (i, 0, 0))
    tab_shape = jax.ShapeDtypeStruct((ntiles, 1, ne), I32)
    lp, n8, soff8, dpre8, tot8 = pl.pallas_call(
        _run_plan_kernel,
        grid=(ntiles,),
        in_specs=[pl.BlockSpec((tt, k), lambda i: (i, 0))],
        out_specs=[pl.BlockSpec((tt, k), lambda i: (i, 0)), tab, tab, tab,
                   pl.BlockSpec((1, ne), lambda i: (0, 0))],
        out_shape=[jax.ShapeDtypeStruct((t, k), I32), tab_shape, tab_shape, tab_shape,
                   jax.ShapeDtypeStruct((1, ne), F32)],
        scratch_shapes=[pltpu.VMEM((1, ne), F32)],
        compiler_params=_params(1),
        name="moe_run_plan",
    )(te)
    nbp = -(-nblocks // LANES) * LANES
    per_expert = jax.ShapeDtypeStruct((1, ne), I32)
    dbase8, tail8, ntail8, be, nv = pl.pallas_call(
        functools.partial(_block_plan_kernel, bm=bm),
        out_shape=[jax.ShapeDtypeStruct((ntiles, ne), I32), per_expert, per_expert,
                   jax.ShapeDtypeStruct((1, nbp), I32), jax.ShapeDtypeStruct((1, LANES), I32)],
        name="moe_block_plan",
    )(tot8, dpre8.reshape(ntiles, ne))
    runs = (n8.reshape(-1), soff8.reshape(-1), dbase8.reshape(-1))
    be, nv = be.reshape(-1), nv.reshape(-1)
    tails = (tail8.reshape(-1), ntail8.reshape(-1), nv)
    return lp, runs, tails, be, nv


def _for_each_run_chunk(tables, tile, ne, max8, fn):
    n8_ref, soff8_ref, dbase8_ref = tables

    def per_expert(e, carry):
        idx = tile * ne + e
        n, o, d = n8_ref[idx], soff8_ref[idx], dbase8_ref[idx]
        bit = max8
        while bit >= 1:
            @pl.when((n & bit) != 0)
            def _(o=o, d=d, bit=bit):
                fn(pl.multiple_of(o * SUBLANES, SUBLANES), pl.multiple_of(d * SUBLANES, SUBLANES), bit * SUBLANES)
            o = o + (n & bit)
            d = d + (n & bit)
            bit //= 2
        return carry

    lax.fori_loop(0, ne, per_expert, 0)


def _dispatch_runs_kernel(n8_ref, soff8_ref, dbase8_ref, tail8_ref, ntail8_ref, nv_ref, lpt_ref, *rest,
                          ne, group_tiles, max_tail8):
    src_refs = rest[:len(group_tiles)]
    xr_ref, xs, zb, sem = rest[len(group_tiles):]
    tile = pl.program_id(0)

    @pl.when(tile == 0)
    def _():
        zb[...] = jnp.zeros(zb.shape, F32)
        half = zb.shape[0]
        n_half = xr_ref.shape[0] // half

        def zero_unused(issue):
            def per_half(h, carry):
                issue(pltpu.make_async_copy(zb, xr_ref.at[pl.ds(pl.multiple_of(h * half, half), half)], sem))
                return carry
            lax.fori_loop(2 * nv_ref[0], n_half, per_half, 0)

        def zero_tails(issue):
            def per_expert(e, carry):
                n, d = ntail8_ref[e], tail8_ref[e]
                bit = max_tail8
                while bit >= 1:
                    @pl.when((n & bit) != 0)
                    def _(d=d, bit=bit):
                        issue(pltpu.make_async_copy(
                            zb.at[pl.ds(0, bit * SUBLANES)],
                            xr_ref.at[pl.ds(pl.multiple_of(d * SUBLANES, SUBLANES), bit * SUBLANES)], sem))
                    d = d + (n & bit)
                    bit //= 2
                return carry
            lax.fori_loop(0, ne, per_expert, 0)

        for fill in (zero_unused, zero_tails):
            fill(lambda cp: cp.start())
            fill(lambda cp: cp.wait())

    src = src_refs[-1][...]
    first_tile = sum(group_tiles)
    for g in range(len(group_tiles) - 2, -1, -1):
        first_tile -= group_tiles[g + 1]
        src = jnp.where(tile < first_tile, src_refs[g][...], src)
    tt = src.shape[0]
    rows = xs.shape[0]
    r = lax.broadcasted_iota(I32, (rows, tt), 0)
    lpt = lpt_ref[...]
    sel = jnp.zeros((rows, tt), F32)
    for k in range(lpt.shape[0]):
        sel = sel + jnp.where(r == lpt[k:k + 1, :], 1.0, 0.0)
    xs[...] = _dot(sel.astype(BF16), src.astype(BF16))

    def copy(o, d, n):
        return pltpu.make_async_copy(xs.at[pl.ds(o, n)], xr_ref.at[pl.ds(d, n)], sem)

    tables = (n8_ref, soff8_ref, dbase8_ref)
    _for_each_run_chunk(tables, tile, ne, tt // SUBLANES, lambda o, d, n: copy(o, d, n).start())
    _for_each_run_chunk(tables, tile, ne, tt // SUBLANES, lambda o, d, n: copy(o, d, n).wait())


def _dispatch_runs(runs, tails, lpt, srcs, ne, tt, xr_rows, bm):
    d = srcs[0].shape[1]
    k = lpt.shape[0]
    group_tiles = tuple(src.shape[0] // tt for src in srcs)
    assert all(src.shape[0] % tt == 0 for src in srcs) and sum(group_tiles) * tt == lpt.shape[1]
    max_tail8 = bm // SUBLANES // 2
    in_specs = [pl.BlockSpec((k, tt), lambda i, *_: (0, i))]
    first = 0
    for n in group_tiles:
        in_specs.append(pl.BlockSpec((tt, d), lambda i, *_, first=first, n=n: (jnp.clip(i - first, 0, n - 1), 0)))
        first += n
    grid_spec = pltpu.PrefetchScalarGridSpec(
        num_scalar_prefetch=len(runs) + len(tails),
        grid=(sum(group_tiles),),
        in_specs=in_specs,
        out_specs=pl.BlockSpec(memory_space=pl.ANY),
        scratch_shapes=[pltpu.VMEM((k * tt + SUBLANES * ne, d), F32),
                        pltpu.VMEM((max_tail8 * SUBLANES, d), F32), pltpu.SemaphoreType.DMA(())],
    )
    return pl.pallas_call(
        functools.partial(_dispatch_runs_kernel, ne=ne, group_tiles=group_tiles, max_tail8=max_tail8),
        grid_spec=grid_spec,
        out_shape=jax.ShapeDtypeStruct((xr_rows, d), F32),
        compiler_params=_params(1),
        name="moe_dispatch",
    )(*runs, *tails, lpt, *srcs)


def _combine_runs_kernel(n8_ref, soff8_ref, dbase8_ref, lp_ref, yr_ref, gate_ref, x1_ref, gt2_ref, g_ref,
                         o_ref, ys, sem, *, ne, tile_offset):
    tt = lp_ref.shape[0]
    rows = ys.shape[0]
    ys[rows - SUBLANES * ne:rows, :] = jnp.zeros((SUBLANES * ne, ys.shape[1]), F32)

    def copy(o, d, n):
        return pltpu.make_async_copy(yr_ref.at[pl.ds(d, n)], ys.at[pl.ds(o, n)], sem)

    tables = (n8_ref, soff8_ref, dbase8_ref)
    tile = pl.program_id(0) + tile_offset
    _for_each_run_chunk(tables, tile, ne, tt // SUBLANES, lambda o, d, n: copy(o, d, n).start())
    c = lax.broadcasted_iota(I32, (tt, rows), 1)
    lp = lp_ref[...]
    gate = gate_ref[...]
    mix = jnp.zeros((tt, rows), F32)
    for k in range(lp.shape[1]):
        mix = mix + jnp.where(c == lp[:, k:k + 1], gate[:, k:k + 1], 0.0)
    mix_hi, mix_lo = _split2(mix)
    _for_each_run_chunk(tables, tile, ne, tt // SUBLANES, lambda o, d, n: copy(o, d, n).wait())
    yb = ys[...].astype(BF16)
    f = _dot(mix_hi, yb) + _dot(mix_lo, yb)
    o_ref[...] = x1_ref[...] + gt2_ref[...] * _rms(f, g_ref[...])


def _combine_runs(tables, tile_offset, lp, yr, gate, x1, gt2, g, rows_per_batch, ne, tt):
    t, d = x1.shape
    k = lp.shape[1]
    assert t % tt == 0
    row = pl.BlockSpec((tt, d), lambda i, *_: (i, 0))
    grid_spec = pltpu.PrefetchScalarGridSpec(
        num_scalar_prefetch=3,
        grid=(t // tt,),
        in_specs=[pl.BlockSpec((tt, k), lambda i, *_: (i + tile_offset, 0)),
                  pl.BlockSpec(memory_space=pl.ANY),
                  pl.BlockSpec((tt, k), lambda i, *_: (i, 0)),
                  row, _mod_spec(gt2, tt, rows_per_batch),
                  pl.BlockSpec((1, d), lambda i, *_: (0, 0))],
        out_specs=row,
        scratch_shapes=[pltpu.VMEM((k * tt + SUBLANES * ne, d), F32), pltpu.SemaphoreType.DMA(())],
    )
    return pl.pallas_call(
        functools.partial(_combine_runs_kernel, ne=ne, tile_offset=tile_offset),
        grid_spec=grid_spec,
        out_shape=jax.ShapeDtypeStruct((t, d), F32),
        compiler_params=_params(1),
        name="moe_combine",
    )(*tables, lp, yr, gate, x1, gt2, g)


def _blockdiag_pairs(wa, wx, gw):
    nb, c, _ = wa.shape
    per = gw // c
    ng = nb // per
    eye = jnp.eye(per, dtype=wa.dtype)

    def bd(w):
        return jnp.einsum("gpcd,pq->gpcqd", w.reshape(ng, per, c, c), eye).reshape(ng, gw, gw)

    return jnp.concatenate([bd(wa), bd(wx)], axis=2).astype(BF16)


def _layer(xp, xs, ck, cv, clf, sconv, sh0, page_table, cp, cs, w):
    bp, s, d = xp.shape
    bs, ds_, _ = xs.shape
    nh = w["b_forget"].shape[0]
    hd = d // nh
    tp, ts = bp * s, bs * ds_
    kw = w["conv_w"].shape[0]
    ne = w["w_router"].shape[1]

    w_in = w["w_in"].astype(BF16)
    w_x, w_q, w_k, w_v = (w_in[:, n * d:(n + 1) * d] for n in range(4))
    w_ga, w_gb = w_in[:, 4 * d + nh:5 * d + nh], w_in[:, 5 * d + nh:]
    w4 = jnp.stack([w_x, w_q, w_ga, w_gb])
    wkvt = jnp.stack([w_k.T, w_v.T])
    w6 = jnp.stack([w_x, w_q, w_ga, w_gb, w_k, w_v])
    wft = w_in[:, 4 * d:4 * d + nh].T
    bfo = w["b_forget"].reshape(nh, 1)
    gw = min(2 * LANES, d)
    wbd = _blockdiag_pairs(w["rg_w_a"], w["rg_w_x"], gw)
    vec = lambda v: v.reshape(1, d)
    wo = w["w_out"].astype(BF16)
    wgu, wdn = w["w_gate_up"], w["w_down"]

    ada = _ada(jnp.concatenate([cp, cs], axis=0), w["w_ada"], w["b_ada"])
    mods_p = [m.reshape(bp, 1, d) for m in jnp.split(ada[:bp], 6, axis=-1)]
    mods_s = [jnp.repeat(m, ds_, axis=0) for m in jnp.split(ada[bp:], 6, axis=-1)]

    xpf, xsf = xp.reshape(tp, d), xs.reshape(ts, d)
    g_pre = vec(w["g_mix_pre"])
    zp, kt_p, vt_p, lft_p = _inproj(xpf, mods_p[1], mods_p[0], g_pre, w4, wkvt, wft, bfo, bp, s)
    zs, lft_s = _inproj(xsf, mods_s[1], mods_s[0], g_pre, w6, None, wft, bfo, 1, ds_)

    rnn_w = (w["conv_w"], vec(w["conv_b"]), wbd, vec(w["rg_b_a"]), vec(w["rg_b_x"]), vec(w["rg_lambda"]))
    zero_prev = jnp.zeros((bp, SUBLANES, d), F32)
    zero_h = jnp.zeros((bp, 1, d), F32)
    yr_p, conv_p, hl_p = _rnn_prompt(zp, zero_prev, zero_h, *rnn_w, reset_first=True)
    prev8 = jnp.pad(sconv, ((0, 0), (SUBLANES - (kw - 1), 0), (0, 0))).reshape(ts, d)
    h08 = jnp.pad(sh0[:, None, :], ((0, 0), (0, SUBLANES - 1), (0, 0))).reshape(ts, d)
    yr_s = _rnn_sample(zs, prev8, h08, *rnn_w)
    hl_s = yr_s.reshape(bs, ds_, d)[:, ds_ - 1]
    conv_s = zs[0].reshape(bs, ds_, d)[:, ds_ - (kw - 1):]

    ft_p = _cumsum_lanes(lft_p).reshape(bp, nh // 2, 2, s)
    ya_p = _fox_prompt(zp.reshape(4, bp, s, d), kt_p, vt_p, ft_p, hd).reshape(tp, d)
    cn_s = _cumsum_lanes(lft_s, seg=ds_).reshape(nh, bs, ds_).transpose(1, 0, 2)
    npool, page = ck.shape[0], ck.shape[1]
    ckt = ck.transpose(0, 2, 3, 1).reshape(npool, d, page)
    cvt = cv.transpose(0, 2, 3, 1).reshape(npool, d, page)
    ya_s = _fox_sample(page_table, zs, (1, 4, 5), cn_s, ckt, cvt, clf.transpose(0, 2, 1), nh, hd)

    post_w = (wo, vec(w["g_mix_post"]), vec(w["g_ffn_pre"]), w["w_router"], w["b_router"].reshape(1, ne))
    x1_p, h2_p, te_p, gate_p = _post(yr_p, ya_p, zp, xpf, mods_p[2], mods_p[4], mods_p[3], *post_w, s)
    x1_s, h2_s, te_s, gate_s = _post(yr_s, ya_s, zs, xsf, mods_s[2], mods_s[4], mods_s[3], *post_w, ds_)

    bm = 512
    t_all = tp + ts
    tt = _moe_tile(t_all)
    assert tp % tt == 0 and ts % tt == 0
    run_pad = (SUBLANES - 1) * (t_all // tt) * ne
    nblocks = -(-(t_all * TOP_K + run_pad) // bm) + ne
    lp, tables, tails, be, nv = _plan_routes(jnp.concatenate([te_p, te_s], axis=0), ne, bm, nblocks, tt)
    lpt = lp.T
    xr = _dispatch_runs(tables, tails, lpt, (h2_p, h2_s), ne, tt, nblocks * bm, bm)
    yr = _experts(be, nv, xr, wgu, w["b_gate_up"], wdn, w["b_down"], bm)
    g_post = vec(w["g_ffn_post"])
    y_p = _combine_runs(tables, 0, lp, yr, gate_p, x1_p, mods_p[5], g_post, s, ne, tt)
    y_s = _combine_runs(tables, tp // tt, lp, yr, gate_s, x1_s, mods_s[5], g_post, ds_, ne, tt)

    heads_t = lambda zt_: zt_.reshape(bp, nh, hd, s).transpose(0, 3, 1, 2)
    out_p = (y_p.reshape(bp, s, d), heads_t(kt_p), heads_t(vt_p),
             lft_p.transpose(0, 2, 1), conv_p, hl_p.reshape(bp, d))
    out_s = (y_s.reshape(bs, ds_, d), zs[4].reshape(bs, ds_, nh, hd), zs[5].reshape(bs, ds_, nh, hd),
             lft_s[0].T.reshape(bs, ds_, nh), conv_s, hl_s)
    return out_p, out_s


def kernel(x_prompt, x_sample, cache_k, cache_v, cache_logf, state_conv, state_h, page_table, c_prompt, c_sample, w_ada, b_ada, g_mix_pre, g_mix_post, w_in, b_forget, conv_w, conv_b, rg_w_a, rg_b_a, rg_w_x, rg_b_x, rg_lambda, w_out, g_ffn_pre, g_ffn_post, w_router, b_router, w_gate_up, b_gate_up, w_down, b_down):
    weights = dict(w_ada=w_ada, b_ada=b_ada, g_mix_pre=g_mix_pre, g_mix_post=g_mix_post, w_in=w_in,
                   b_forget=b_forget, conv_w=conv_w, conv_b=conv_b, rg_w_a=rg_w_a, rg_b_a=rg_b_a,
                   rg_w_x=rg_w_x, rg_b_x=rg_b_x, rg_lambda=rg_lambda, w_out=w_out, g_ffn_pre=g_ffn_pre,
                   g_ffn_post=g_ffn_post, w_router=w_router, b_router=b_router, w_gate_up=w_gate_up,
                   b_gate_up=b_gate_up, w_down=w_down, b_down=b_down)
    depth = w_ada.shape[0]
    yp, ys = x_prompt, x_sample
    per_layer = []
    for l in range(depth):
        wl = {k: v[l] for k, v in weights.items()}
        out_p, out_s = _layer(yp, ys, cache_k[l], cache_v[l], cache_logf[l], state_conv[l], state_h[l],
                              page_table, c_prompt, c_sample, wl)
        yp, ys = out_p[0], out_s[0]
        per_layer.append(out_p[1:] + out_s[1:])
    stacked = [jnp.stack(leaf) for leaf in zip(*per_layer)]
    return (yp, ys, *stacked)
```

```python
import functools

import jax
import jax.numpy as jnp
from jax import lax
from jax.experimental import pallas as pl
from jax.experimental.pallas import tpu as pltpu

F32 = jnp.float32
BF16 = jnp.bfloat16
I32 = jnp.int32

EPS = 1e-6
LRU_C = 8.0
TOP_K = 4
SWIGLU_LIMIT = 7.0
SWIGLU_ALPHA = 1.702
NEG_BIG = -1e30

LANES = 128
SUBLANES = 8
VMEM_LIMIT = 56 * 1024 * 1024


def _params(n_grid_dims):
    return pltpu.CompilerParams(
        dimension_semantics=("arbitrary",) * n_grid_dims, vmem_limit_bytes=VMEM_LIMIT)


def _dot(a, b):
    return jnp.dot(a, b, preferred_element_type=F32)


def _dot_nt(a, b):
    return lax.dot_general(a, b, (((1,), (1,)), ((), ())), preferred_element_type=F32)


def _split2(x):
    hi = x.astype(BF16)
    lo = (x - hi.astype(F32)).astype(BF16)
    return hi, lo


def _split3(x):
    p1 = x.astype(BF16)
    r1 = x - p1.astype(F32)
    p2 = r1.astype(BF16)
    p3 = (r1 - p2.astype(F32)).astype(BF16)
    return p1, p2, p3


def _dot3(a, b):
    ah, al = _split2(a)
    bh, bl = _split2(b)
    return _dot(ah, bh) + _dot(ah, bl) + _dot(al, bh)


def _dot_exact01(x, u01):
    p1, p2, p3 = _split3(x)
    return _dot(p1, u01) + _dot(p2, u01) + _dot(p3, u01)


def _rms(x, g):
    ms = jnp.mean(x * x, axis=-1, keepdims=True)
    return x * lax.rsqrt(ms + EPS) * g


def _softplus(x):
    return jnp.maximum(x, 0.0) + jnp.log1p(jnp.exp(-jnp.abs(x)))


def _log_sigmoid(x):
    return jnp.minimum(x, 0.0) - jnp.log1p(jnp.exp(-jnp.abs(x)))


def _ada_kernel(c_ref, w_ref, b_ref, o_ref):
    c = c_ref[...]
    o_ref[...] = _dot3(c * jax.nn.sigmoid(c), w_ref[...]) + b_ref[...]


def _ada(c, w, b):
    n, d = c.shape
    nout = w.shape[1]
    tn = min(nout, 1536)
    assert nout % tn == 0
    return pl.pallas_call(
        _ada_kernel,
        grid=(nout // tn,),
        in_specs=[pl.BlockSpec((n, d), lambda j: (0, 0)),
                  pl.BlockSpec((d, tn), lambda j: (0, j)),
                  pl.BlockSpec((1, tn), lambda j: (0, j))],
        out_specs=pl.BlockSpec((n, tn), lambda j: (0, j)),
        out_shape=jax.ShapeDtypeStruct((n, nout), F32),
        compiler_params=_params(1),
        name="ada",
    )(c, w, b.reshape(1, nout))


def _inproj_kernel(x_ref, sc_ref, sh_ref, g_ref, w_ref, wft_ref, bf_ref, *rest, nz, nt):
    if nt:
        wt_ref, z_ref = rest[:2]
        zt_refs = rest[2:2 + nt]
    else:
        z_ref = rest[0]
    lft_ref, h_scr = rest[-2:]
    j = pl.program_id(1)

    @pl.when(j == 0)
    def _():
        h = _rms(x_ref[...], g_ref[...]) * (1.0 + sc_ref[...]) + sh_ref[...]
        hb = h.astype(BF16)
        h_scr[...] = hb
        lft_ref[...] = _log_sigmoid(_dot_nt(wft_ref[...], hb) + bf_ref[...])

    if nt:
        @pl.when(j < nz)
        def _():
            z_ref[...] = _dot(h_scr[...], w_ref[jnp.minimum(j, nz - 1)])

        for n in range(nt):
            @pl.when(j == nz + n)
            def _(n=n):
                zt_refs[n][...] = _dot_nt(wt_ref[n], h_scr[...])
    else:
        z_ref[...] = _dot(h_scr[...], w_ref[j])


def _mod_spec(mod, tm, rows_per_batch):
    if mod.ndim == 3:
        tiles_per_batch = rows_per_batch // tm
        return pl.BlockSpec((None, 1, mod.shape[-1]), lambda i, *_: (i // tiles_per_batch, 0, 0))
    return pl.BlockSpec((tm, mod.shape[-1]), lambda i, *_: (i, 0))


def _inproj(x, sc, sh, g, w, wt, wft, bfo, nbatch, rows_per_batch):
    t, d = x.shape
    nh = wft.shape[0]
    nz = w.shape[0]
    nt = 0 if wt is None else wt.shape[0]
    s = t // nbatch
    tm = min(1024, s)
    assert s % tm == 0 and (sc.ndim == 2 or rows_per_batch % tm == 0)
    tpb = s // tm
    resident = lambda n: pl.BlockSpec((n, d, d), lambda i, j: (0, 0, 0), pipeline_mode=pl.Buffered(1))
    in_specs = [pl.BlockSpec((tm, d), lambda i, j: (i, 0)),
                _mod_spec(sc, tm, rows_per_batch), _mod_spec(sh, tm, rows_per_batch),
                pl.BlockSpec((1, d), lambda i, j: (0, 0)),
                resident(nz),
                pl.BlockSpec((nh, d), lambda i, j: (0, 0)),
                pl.BlockSpec((nh, 1), lambda i, j: (0, 0))]
    out_specs = [pl.BlockSpec((None, tm, d), lambda i, j: (jnp.minimum(j, nz - 1), i, 0))]
    out_shape = [jax.ShapeDtypeStruct((nz, t, d), F32)]
    args = [x, sc, sh, g, w, wft, bfo]
    if nt:
        in_specs.append(resident(nt))
        args.append(wt)
        for _ in range(nt):
            out_specs.append(pl.BlockSpec((None, d, tm), lambda i, j: (i // tpb, 0, i % tpb)))
            out_shape.append(jax.ShapeDtypeStruct((nbatch, d, s), F32))
    out_specs.append(pl.BlockSpec((None, nh, tm), lambda i, j: (i // tpb, 0, i % tpb)))
    out_shape.append(jax.ShapeDtypeStruct((nbatch, nh, s), F32))
    return pl.pallas_call(
        functools.partial(_inproj_kernel, nz=nz, nt=nt),
        grid=(t // tm, nz + nt),
        in_specs=in_specs,
        out_specs=out_specs,
        out_shape=out_shape,
        scratch_shapes=[pltpu.VMEM((tm, d), BF16)],
        compiler_params=_params(2),
        name="inproj",
    )(*args)


def _cumsum_kernel(lf_ref, o_ref, *, cw, seg):
    n = lf_ref.shape[1]
    r = lax.broadcasted_iota(I32, (cw, cw), 0)
    c = lax.broadcasted_iota(I32, (cw, cw), 1)
    keep = r <= c
    if seg is not None:
        keep = jnp.logical_and(keep, (r // seg) == (c // seg))
    u01 = jnp.where(keep, 1.0, 0.0).astype(BF16)
    carry = jnp.zeros((lf_ref.shape[0], 1), F32)
    for i in range(n // cw):
        f = _dot_exact01(lf_ref[:, i * cw:(i + 1) * cw], u01)
        if seg is None:
            f = f + carry
            carry = f[:, cw - 1:cw]
        o_ref[:, i * cw:(i + 1) * cw] = f


def _cumsum_lanes(lft, seg=None):
    nb, nh, s = lft.shape
    block = s if seg is None else min(s, 2 * LANES)
    cw = min(2 * LANES, block)
    assert s % block == 0 and block % cw == 0 and (seg is None or cw % seg == 0)
    spec = pl.BlockSpec((None, nh, block), lambda b, n: (b, 0, n))
    return pl.pallas_call(
        functools.partial(_cumsum_kernel, cw=cw, seg=seg),
        grid=(nb, s // block),
        in_specs=[spec],
        out_specs=spec,
        out_shape=jax.ShapeDtypeStruct((nb, nh, s), F32),
        compiler_params=_params(2),
        name="cumsum_logf",
    )(lft)


def _rglru_coeffs(xc, wbd_ref, ba, bx, lam):
    ng, gw = wbd_ref.shape[0], wbd_ref.shape[1]
    xb = xc.astype(BF16)
    ra, ri = [], []
    for g in range(ng):
        o = _dot(xb[:, g * gw:(g + 1) * gw], wbd_ref[g])
        ra.append(o[:, :gw])
        ri.append(o[:, gw:])
    r = jax.nn.sigmoid((ra[0] if ng == 1 else jnp.concatenate(ra, axis=1)) + ba)
    i = jax.nn.sigmoid((ri[0] if ng == 1 else jnp.concatenate(ri, axis=1)) + bx)
    log_a = -LRU_C * r * _softplus(-lam)
    a = jnp.exp(log_a)
    th = jnp.tanh(log_a)
    mult = jnp.sqrt(-2.0 * th / (1.0 - th))
    return a, mult, i


def _scan8(a8, u8, row):
    for s in (1, 2, 4):
        a_sh = pltpu.roll(a8, s, axis=0)
        u_sh = pltpu.roll(u8, s, axis=0)
        m = row >= s
        u8 = u8 + a8 * jnp.where(m, u_sh, 0.0)
        a8 = a8 * jnp.where(m, a_sh, 1.0)
    return a8, u8


def _rnn_prompt_kernel(x_ref, prev_ref, h0_ref, cw_ref, cb_ref, wbd_ref, ba_ref, bx_ref, lam_ref,
                       y_ref, cs_ref, hl_ref, ext_scr, a_scr, u_scr, h_scr, *, reset_first):
    c = pl.program_id(1)
    tc, d = x_ref.shape
    kw = cw_ref.shape[0]

    @pl.when(c == 0)
    def _():
        ext_scr[0:SUBLANES, :] = prev_ref[...]
        h_scr[...] = h0_ref[...]

    ext_scr[SUBLANES:SUBLANES + tc, :] = x_ref[...]
    w = cw_ref[...]
    xc = cb_ref[...]
    for j in range(kw):
        off = SUBLANES - (kw - 1) + j
        xc = xc + w[j:j + 1, :] * ext_scr[off:off + tc, :]
    ext_scr[0:SUBLANES, :] = ext_scr[tc:tc + SUBLANES, :]

    a, mult, i = _rglru_coeffs(xc, wbd_ref, ba_ref[...], bx_ref[...], lam_ref[...])
    if reset_first:
        row = lax.broadcasted_iota(I32, (tc, 1), 0)
        mult = jnp.where(row == jnp.where(c == 0, 0, -1), 1.0, mult)
    a_scr[...] = a
    u_scr[...] = mult * i * xc

    row8 = lax.broadcasted_iota(I32, (SUBLANES, d), 0)

    def body(g, h):
        r0 = pl.multiple_of(g * SUBLANES, SUBLANES)
        a8, u8 = _scan8(a_scr[pl.ds(r0, SUBLANES), :], u_scr[pl.ds(r0, SUBLANES), :], row8)
        h8 = u8 + a8 * h
        y_ref[pl.ds(r0, SUBLANES), :] = h8
        return h8[SUBLANES - 1:SUBLANES, :]

    h = lax.fori_loop(0, tc // SUBLANES, body, h_scr[...], unroll=4)
    h_scr[...] = h

    @pl.when(c == pl.num_programs(1) - 1)
    def _():
        hl_ref[...] = h
        cs_ref[...] = x_ref[tc - (kw - 1):tc, :]


def _rnn_prompt(x3, prev8, h0, cw, cb, wbd, ba, bx, lam, reset_first):
    b = prev8.shape[0]
    t, d = x3.shape[1], x3.shape[2]
    s = t // b
    kw = cw.shape[0]
    tc = min(256, s)
    nc = s // tc
    assert s % tc == 0 and kw - 1 <= SUBLANES and tc % SUBLANES == 0
    vec = pl.BlockSpec((1, d), lambda bi, ci: (0, 0))
    return pl.pallas_call(
        functools.partial(_rnn_prompt_kernel, reset_first=reset_first),
        grid=(b, nc),
        in_specs=[pl.BlockSpec((None, tc, d), lambda bi, ci: (0, bi * nc + ci, 0)),
                  pl.BlockSpec((None, SUBLANES, d), lambda bi, ci: (bi, 0, 0)),
                  pl.BlockSpec((None, 1, d), lambda bi, ci: (bi, 0, 0)),
                  pl.BlockSpec((kw, d), lambda bi, ci: (0, 0)), vec,
                  pl.BlockSpec(wbd.shape, lambda bi, ci: (0, 0, 0)), vec, vec, vec],
        out_specs=[pl.BlockSpec((tc, d), lambda bi, ci: (bi * nc + ci, 0)),
                   pl.BlockSpec((None, kw - 1, d), lambda bi, ci: (bi, 0, 0)),
                   pl.BlockSpec((None, 1, d), lambda bi, ci: (bi, 0, 0))],
        out_shape=[jax.ShapeDtypeStruct((t, d), F32),
                   jax.ShapeDtypeStruct((b, kw - 1, d), F32),
                   jax.ShapeDtypeStruct((b, 1, d), F32)],
        scratch_shapes=[pltpu.VMEM((tc + SUBLANES, d), F32), pltpu.VMEM((tc, d), F32),
                        pltpu.VMEM((tc, d), F32), pltpu.VMEM((1, d), F32)],
        compiler_params=_params(2),
        name="rnn_prompt",
    )(x3, prev8, h0, cw, cb, wbd, ba, bx, lam)


def _rnn_sample_kernel(x_ref, prev_ref, h0_ref, cw_ref, cb_ref, wbd_ref, ba_ref, bx_ref, lam_ref,
                       y_ref):
    r, d = x_ref.shape
    kw = cw_ref.shape[0]
    x = x_ref[...]
    p = prev_ref[...]
    step = lax.broadcasted_iota(I32, (r, 1), 0) & (SUBLANES - 1)
    w = cw_ref[...]
    xc = cb_ref[...]
    for j in range(kw):
        back = kw - 1 - j
        if back == 0:
            xs = x
        else:
            xs = jnp.where(step >= back, pltpu.roll(x, back, axis=0),
                           pltpu.roll(p, (r - SUBLANES + back) % r, axis=0))
        xc = xc + w[j:j + 1, :] * xs
    a, mult, i = _rglru_coeffs(xc, wbd_ref, ba_ref[...], bx_ref[...], lam_ref[...])
    u = mult * i * xc + a * h0_ref[...]
    for s in (1, 2, 4):
        m = step >= s
        a_sh = pltpu.roll(a, s, axis=0)
        u_sh = pltpu.roll(u, s, axis=0)
        u = u + a * jnp.where(m, u_sh, 0.0)
        a = a * jnp.where(m, a_sh, 1.0)
    y_ref[...] = u


def _rnn_sample(x3, prev8, h08, cw, cb, wbd, ba, bx, lam):
    t, d = x3.shape[1], x3.shape[2]
    nb_total = t // SUBLANES
    kw = cw.shape[0]
    nb = min(16, nb_total)
    r = nb * SUBLANES
    assert nb_total % nb == 0 and nb % SUBLANES == 0
    vec = pl.BlockSpec((1, d), lambda i: (0, 0))
    return pl.pallas_call(
        _rnn_sample_kernel,
        grid=(nb_total // nb,),
        in_specs=[pl.BlockSpec((None, r, d), lambda i: (0, i, 0)),
                  pl.BlockSpec((r, d), lambda i: (i, 0)),
                  pl.BlockSpec((r, d), lambda i: (i, 0)),
                  pl.BlockSpec((kw, d), lambda i: (0, 0)), vec,
                  pl.BlockSpec(wbd.shape, lambda i: (0, 0, 0)), vec, vec, vec],
        out_specs=pl.BlockSpec((r, d), lambda i: (i, 0)),
        out_shape=jax.ShapeDtypeStruct((t, d), F32),
        compiler_params=_params(1),
        name="rnn_sample",
    )(x3, prev8, h08, cw, cb, wbd, ba, bx, lam)


def _fox_prompt_kernel(q_ref, kt_ref, vt_ref, f_ref, o_ref, *, hd, scale):
    i = pl.program_id(2)
    tq = q_ref.shape[0]
    lane = lax.broadcasted_iota(I32, (1, 2 * hd), 1)
    q = q_ref[...] * scale
    qs = jnp.concatenate([jnp.where(lane < hd, q, 0.0), jnp.where(lane >= hd, q, 0.0)], axis=0).astype(BF16)

    def scores(kb):
        k0 = pl.multiple_of(kb * tq, tq)
        kt = kt_ref[:, pl.ds(k0, tq)].astype(BF16)
        fb = f_ref[:, pl.ds(k0, tq)]
        s = _dot(qs, kt)
        return jnp.concatenate([s[:tq] - fb[0:1, :], s[tq:] - fb[1:2, :]], axis=0)

    vrow = lax.broadcasted_iota(I32, (2 * hd, 1), 0)

    def update(kb, s, m, acc):
        k0 = pl.multiple_of(kb * tq, tq)
        vt = vt_ref[:, pl.ds(k0, tq)]
        vt0 = jnp.where(vrow < hd, vt, 1.0).astype(BF16)
        vt1 = jnp.where(vrow >= hd, vt, 1.0).astype(BF16)
        m_new = jnp.maximum(m, jnp.max(s, axis=-1, keepdims=True))
        p = jnp.exp(s - m_new).astype(BF16)
        pv = jnp.concatenate([_dot_nt(p[:tq], vt0), _dot_nt(p[tq:], vt1)], axis=0)
        return m_new, jnp.exp(m - m_new) * acc + pv

    def body(kb, carry):
        s, m, acc = carry
        return (scores(kb + 1),) + update(kb, s, m, acc)

    init = (scores(0), jnp.full((2 * tq, 1), NEG_BIG, F32), jnp.zeros((2 * tq, 2 * hd), F32))
    s, m, acc = lax.fori_loop(0, i, body, init)
    r = lax.broadcasted_iota(I32, (2 * tq, tq), 0) & (tq - 1)
    c = lax.broadcasted_iota(I32, (2 * tq, tq), 1)
    _, acc = update(i, jnp.where(c <= r, s, NEG_BIG), m, acc)
    o = acc / pltpu.roll(acc, hd, axis=1)
    o_ref[...] = jnp.where(lane < hd, o[:tq], o[tq:])


def _fox_prompt(z4, kt, vt, ft4, hd):
    _, b, s, d = z4.shape
    assert 2 * hd == LANES and d % LANES == 0
    npair = d // LANES
    tq = min(512, s)
    assert s % tq == 0 and tq & (tq - 1) == 0
    return pl.pallas_call(
        functools.partial(_fox_prompt_kernel, hd=hd, scale=hd ** -0.5),
        grid=(b, npair, s // tq),
        in_specs=[pl.BlockSpec((None, None, tq, LANES), lambda bi, j, i: (1, bi, i, j)),
                  pl.BlockSpec((None, LANES, s), lambda bi, j, i: (bi, j, 0)),
                  pl.BlockSpec((None, LANES, s), lambda bi, j, i: (bi, j, 0)),
                  pl.BlockSpec((None, None, 2, s), lambda bi, j, i: (bi, j, 0, 0))],
        out_specs=pl.BlockSpec((None, tq, LANES), lambda bi, j, i: (bi, i, j)),
        out_shape=jax.ShapeDtypeStruct((b, s, d), F32),
        compiler_params=_params(3),
        name="fox_prompt",
    )(z4, kt, vt, ft4)


def _fox_sample_kernel(pt_ref, q_ref, kn_ref, vn_ref, cn_ref, *rest, nh, hd, scale, pps, grp):
    del pt_ref
    k_refs, v_refs, lf_refs = rest[:pps], rest[pps:2 * pps], rest[2 * pps:3 * pps]
    o_ref, qx_scr, m_scr, l_scr, acc_scr, fcar_scr = rest[3 * pps:]
    j = pl.program_id(1)
    ds_, d = q_ref.shape
    page = lf_refs[0].shape[1]
    nr = nh * ds_

    def expand(x):
        return jnp.broadcast_to(x[:, None, :], (nh, ds_, x.shape[1])).reshape(nr, x.shape[1])

    def own_head():
        rh = lax.broadcasted_iota(I32, (nr, d), 0) // ds_
        ch = lax.broadcasted_iota(I32, (nr, d), 1) // hd
        return rh == ch

    @pl.when(j == 0)
    def _():
        q = q_ref[...] * scale
        qt = jnp.broadcast_to(q[None], (nh, ds_, d)).reshape(nr, d)
        qx_scr[...] = jnp.where(own_head(), qt, 0.0).astype(BF16)
        m_scr[...] = jnp.full(m_scr.shape, NEG_BIG, F32)
        l_scr[...] = jnp.zeros(l_scr.shape, F32)
        acc_scr[...] = jnp.zeros(acc_scr.shape, F32)
        fcar_scr[...] = jnp.zeros(fcar_scr.shape, F32)

    def update(s, pv_fn):
        m_old = m_scr[...]
        m_new = jnp.maximum(m_old, jnp.max(s, axis=-1, keepdims=True))
        p = jnp.exp(s - m_new)
        alpha = jnp.exp(m_old - m_new)
        l_scr[...] = alpha * l_scr[...] + jnp.sum(p, axis=-1, keepdims=True)
        m_scr[...] = m_new
        acc_scr[...] = alpha * acc_scr[...] + pv_fn(p.astype(BF16))

    r = lax.broadcasted_iota(I32, (page, page), 0)
    c = lax.broadcasted_iota(I32, (page, page), 1)
    u01 = jnp.where(r <= c, 1.0, 0.0).astype(BF16)
    qx = qx_scr[...]
    fcar = fcar_scr[...]
    for g0 in range(0, pps, grp):
        pages = range(g0, min(g0 + grp, pps))
        s_parts = []
        for pi in pages:
            ft = _dot_exact01(lf_refs[pi][...], u01)
            s_parts.append(_dot(qx, k_refs[pi][...].astype(BF16)) - expand(ft + fcar))
            fcar = fcar + ft[:, page - 1:page]
        s = s_parts[0] if len(s_parts) == 1 else jnp.concatenate(s_parts, axis=1)

        def pv_pages(pb, pages=pages):
            out = None
            for n, pi in enumerate(pages):
                term = _dot_nt(pb[:, n * page:(n + 1) * page], v_refs[pi][...].astype(BF16))
                out = term if out is None else out + term
            return out

        update(s, pv_pages)
    fcar_scr[...] = fcar

    @pl.when(j == pl.num_programs(1) - 1)
    def _():
        fn = expand(cn_ref[...] + fcar_scr[...])
        sn = _dot_nt(qx_scr[...], kn_ref[...].astype(BF16)) - fn
        qi = lax.broadcasted_iota(I32, (nr, ds_), 0) & (ds_ - 1)
        key = lax.broadcasted_iota(I32, (nr, ds_), 1)
        sn = jnp.where(key <= qi, sn, NEG_BIG)
        update(sn, lambda pb: _dot(pb, vn_ref[...].astype(BF16)))
        o = jnp.where(own_head(), acc_scr[...] / l_scr[...], 0.0)
        o_ref[...] = o.reshape(nh, ds_, d).sum(axis=0)


def _fox_sample(page_table, z3, slabs, cnt, ckt, cvt, clft, nh, hd):
    b, npages = page_table.shape
    t, d = z3.shape[1], z3.shape[2]
    ds_ = t // b
    page = clft.shape[2]
    pps = next(n for n in (8, 4, 2, 1) if npages % n == 0)
    grp = min(4, pps)
    nr = nh * ds_
    assert ds_ == SUBLANES and ckt.shape[1] == d
    tok = lambda slab: pl.BlockSpec((None, ds_, d), lambda bi, j, pt: (slab, bi, 0))

    def pg(rows, pi):
        return pl.BlockSpec((None, rows, page), lambda bi, j, pt: (pt[bi * npages + j * pps + pi], 0, 0))

    in_specs = [tok(slabs[0]), tok(slabs[1]), tok(slabs[2]),
                pl.BlockSpec((None, nh, ds_), lambda bi, j, pt: (bi, 0, 0))]
    in_specs += [pg(d, pi) for pi in range(pps)] * 2 + [pg(nh, pi) for pi in range(pps)]
    grid_spec = pltpu.PrefetchScalarGridSpec(
        num_scalar_prefetch=1,
        grid=(b, npages // pps),
        in_specs=in_specs,
        out_specs=pl.BlockSpec((ds_, d), lambda bi, j, pt: (bi, 0)),
        scratch_shapes=[pltpu.VMEM((nr, d), BF16), pltpu.VMEM((nr, 1), F32), pltpu.VMEM((nr, 1), F32),
                        pltpu.VMEM((nr, d), F32), pltpu.VMEM((nh, 1), F32)],
    )
    return pl.pallas_call(
        functools.partial(_fox_sample_kernel, nh=nh, hd=hd, scale=hd ** -0.5, pps=pps, grp=grp),
        grid_spec=grid_spec,
        out_shape=jax.ShapeDtypeStruct((t, d), F32),
        compiler_params=_params(2),
        name="fox_sample",
    )(page_table.reshape(-1), z3, z3, z3, cnt, *([ckt] * pps), *([cvt] * pps), *([clft] * pps))


def _post_kernel(yr_ref, ya_ref, ga_ref, gb_ref, x_ref, gt1_ref, sc2_ref, sh2_ref, wo_ref, gpost_ref,
                 gpre_ref, wr_ref, br_ref, x1_ref, h2_ref, te_ref, gate_ref):
    merged = jax.nn.sigmoid(ga_ref[...]) * yr_ref[...] + jax.nn.sigmoid(gb_ref[...]) * ya_ref[...]
    out = _dot(merged.astype(BF16), wo_ref[...])
    x1 = x_ref[...] + gt1_ref[...] * _rms(out, gpost_ref[...])
    x1_ref[...] = x1
    h2 = _rms(x1, gpre_ref[...]) * (1.0 + sc2_ref[...]) + sh2_ref[...]
    h2_ref[...] = h2
    logits = _dot3(h2, wr_ref[...]) + br_ref[...]
    ne = logits.shape[1]
    lane = lax.broadcasted_iota(I32, logits.shape, 1)
    vals, idxs = [], []
    for _ in range(TOP_K):
        m = jnp.max(logits, axis=1, keepdims=True)
        idx = jnp.min(jnp.where(logits == m, lane, ne), axis=1, keepdims=True)
        vals.append(m)
        idxs.append(idx)
        logits = jnp.where(lane == idx, -jnp.inf, logits)
    e = jnp.exp(jnp.concatenate(vals, axis=1) - vals[0])
    gate_ref[...] = e / jnp.sum(e, axis=1, keepdims=True)
    te_ref[...] = jnp.concatenate(idxs, axis=1)


def _post(y_rnn, y_attn, z3, x, gt1, sc2, sh2, wo, gpost, gpre, wr, br, rows_per_batch):
    t, d = x.shape
    ne = wr.shape[1]
    tm = min(256, rows_per_batch if gt1.ndim == 3 else t)
    assert t % tm == 0
    row = pl.BlockSpec((tm, d), lambda i: (i, 0))
    vec = pl.BlockSpec((1, d), lambda i: (0, 0))
    mod = lambda m: _mod_spec(m, tm, rows_per_batch)
    return pl.pallas_call(
        _post_kernel,
        grid=(t // tm,),
        in_specs=[row, row,
                  pl.BlockSpec((None, tm, d), lambda i: (2, i, 0)),
                  pl.BlockSpec((None, tm, d), lambda i: (3, i, 0)),
                  row, mod(gt1), mod(sc2), mod(sh2),
                  pl.BlockSpec((d, d), lambda i: (0, 0)), vec, vec,
                  pl.BlockSpec((d, ne), lambda i: (0, 0)),
                  pl.BlockSpec((1, ne), lambda i: (0, 0))],
        out_specs=[row, row,
                   pl.BlockSpec((tm, TOP_K), lambda i: (i, 0)),
                   pl.BlockSpec((tm, TOP_K), lambda i: (i, 0))],
        out_shape=[jax.ShapeDtypeStruct((t, d), F32), jax.ShapeDtypeStruct((t, d), F32),
                   jax.ShapeDtypeStruct((t, TOP_K), I32), jax.ShapeDtypeStruct((t, TOP_K), F32)],
        compiler_params=_params(1),
        name="post_mixer",
    )(y_rnn, y_attn, z3, z3, x, gt1, sc2, sh2, wo, gpost, gpre, wr, br)


def _multi_hot(te, ne):
    lane = lax.broadcasted_iota(I32, (te.shape[0], ne), 1)
    hot = jnp.zeros((te.shape[0], ne), F32)
    for k in range(te.shape[1]):
        hot = hot + jnp.where(lane == te[:, k:k + 1], 1.0, 0.0)
    return hot


def _expert_kernel(be_ref, nv_ref, x_ref, wgu_ref, bgu_ref, wd_ref, bd_ref, o_ref, wgu_bf, wd_bf, *, fc):
    i = pl.program_id(0)
    f = wd_ref.shape[0]

    @pl.when(jnp.logical_or(i == 0, be_ref[i] != be_ref[jnp.maximum(i - 1, 0)]))
    def _():
        for c in range(2 * f // fc):
            wgu_bf[:, c * fc:(c + 1) * fc] = wgu_ref[:, c * fc:(c + 1) * fc].astype(BF16)
        for c in range(f // fc):
            wd_bf[c * fc:(c + 1) * fc, :] = wd_ref[c * fc:(c + 1) * fc, :].astype(BF16)

    @pl.when(i < nv_ref[0])
    def _():
        xb = x_ref[...].astype(BF16)
        acc = jnp.zeros(o_ref.shape, F32)
        for c in range(f // fc):
            glu = _dot(xb, wgu_bf[:, c * fc:(c + 1) * fc]) + bgu_ref[:, c * fc:(c + 1) * fc]
            lin = _dot(xb, wgu_bf[:, f + c * fc:f + (c + 1) * fc]) + bgu_ref[:, f + c * fc:f + (c + 1) * fc]
            glu = jnp.minimum(glu, SWIGLU_LIMIT)
            lin = jnp.clip(lin, -SWIGLU_LIMIT, SWIGLU_LIMIT)
            act = glu * jax.nn.sigmoid(SWIGLU_ALPHA * glu) * (lin + 1.0)
            acc = acc + _dot(act.astype(BF16), wd_bf[c * fc:(c + 1) * fc, :])
        o_ref[...] = acc + bd_ref[...]

    @pl.when(i >= nv_ref[0])
    def _():
        o_ref[...] = jnp.zeros(o_ref.shape, F32)


def _experts(be, nv, xr, wgu, bgu, wd, bd, bm):
    nr, d = xr.shape
    ne, _, f2 = wgu.shape
    f = f2 // 2
    fc = min(512, f)
    grid_spec = pltpu.PrefetchScalarGridSpec(
        num_scalar_prefetch=2,
        grid=(nr // bm,),
        in_specs=[pl.BlockSpec((bm, d), lambda i, be, nv: (jnp.minimum(i, nv[0] - 1), 0)),
                  pl.BlockSpec((None, d, f2), lambda i, be, nv: (be[i], 0, 0)),
                  pl.BlockSpec((None, 1, f2), lambda i, be, nv: (be[i], 0, 0)),
                  pl.BlockSpec((None, f, d), lambda i, be, nv: (be[i], 0, 0)),
                  pl.BlockSpec((None, 1, d), lambda i, be, nv: (be[i], 0, 0))],
        out_specs=pl.BlockSpec((bm, d), lambda i, be, nv: (i, 0)),
        scratch_shapes=[pltpu.VMEM((d, f2), BF16), pltpu.VMEM((f, d), BF16)],
    )
    return pl.pallas_call(
        functools.partial(_expert_kernel, fc=fc),
        grid_spec=grid_spec,
        out_shape=jax.ShapeDtypeStruct((nr, d), F32),
        compiler_params=_params(1),
        name="moe_experts",
    )(be, nv, xr, wgu, bgu.reshape(ne, 1, f2), wd, bd.reshape(ne, 1, d))


def _run_plan_kernel(te_ref, lp_ref, n8_ref, soff8_ref, dpre8_ref, tot8_ref, carry_scr):
    i = pl.program_id(0)
    tt, ne = te_ref.shape[0], n8_ref.shape[1]

    @pl.when(i == 0)
    def _():
        carry_scr[...] = jnp.zeros(carry_scr.shape, F32)

    te = te_ref[...]
    hot = _multi_hot(te, ne)
    r = lax.broadcasted_iota(I32, (tt, tt), 0)
    c = lax.broadcasted_iota(I32, (tt, tt), 1)
    below = jnp.where(c < r, 1.0, 0.0).astype(BF16)
    rank = _dot(below, hot.astype(BF16))
    n8 = jnp.ceil(jnp.sum(hot, axis=0, keepdims=True) * (1.0 / SUBLANES))
    er = lax.broadcasted_iota(I32, (ne, ne), 0)
    ec = lax.broadcasted_iota(I32, (ne, ne), 1)
    before = jnp.where(er < ec, 1.0, 0.0).astype(BF16)
    soff8 = _dot_exact01(jnp.broadcast_to(n8, (SUBLANES, ne)), before)[0:1, :]
    lane = lax.broadcasted_iota(I32, (tt, ne), 1)
    pos = rank + soff8 * float(SUBLANES)
    cols = [jnp.sum(jnp.where(lane == te[:, k:k + 1], pos, 0.0), axis=1, keepdims=True) for k in range(te.shape[1])]
    lp_ref[...] = jnp.concatenate(cols, axis=1).astype(I32)
    n8_ref[...] = n8.astype(I32)
    soff8_ref[...] = soff8.astype(I32)
    dpre8_ref[...] = carry_scr[...].astype(I32)
    carry_scr[...] = carry_scr[...] + n8
    tot8_ref[...] = carry_scr[...]


def _block_plan_kernel(tot8_ref, dpre8_ref, dbase8_ref, tail8_ref, ntail8_ref, be_ref, nv_ref, *, bm):
    ne = tot8_ref.shape[1]
    per_blk = bm // SUBLANES
    nblk = jnp.ceil(tot8_ref[...] * (1.0 / per_blk))
    er = lax.broadcasted_iota(I32, (ne, ne), 0)
    ec = lax.broadcasted_iota(I32, (ne, ne), 1)
    before = jnp.where(er < ec, 1.0, 0.0).astype(BF16)
    start_blk = _dot_exact01(jnp.broadcast_to(nblk, (SUBLANES, ne)), before)[0:1, :]
    end_blk = start_blk + nblk
    dbase8_ref[...] = dpre8_ref[...] + (start_blk * float(per_blk)).astype(I32)
    tail8_ref[...] = (start_blk * float(per_blk) + tot8_ref[...]).astype(I32)
    ntail8_ref[...] = (nblk * float(per_blk) - tot8_ref[...]).astype(I32)
    blk = lax.broadcasted_iota(I32, be_ref.shape, 1).astype(F32)
    be = jnp.zeros(be_ref.shape, F32)
    for e in range(ne):
        be = be + jnp.where(blk >= end_blk[:, e:e + 1], 1.0, 0.0)
    be_ref[...] = jnp.minimum(be, float(ne - 1)).astype(I32)
    nv_ref[...] = jnp.broadcast_to(end_blk[:, ne - 1:ne], nv_ref.shape).astype(I32)


def _moe_tile(t):
    return 256 if t % 256 == 0 else LANES


def _plan_routes(te, ne, bm, nblocks, tt):
    t, k = te.shape
    assert t % tt == 0 and tt & (tt - 1) == 0
    ntiles = t // tt
    tab = pl.BlockSpec((None, 1, ne), lambda i: (i, 0, 0))
    tab_shape = jax.ShapeDtypeStruct((ntiles, 1, ne), I32)
    lp, n8, soff8, dpre8, tot8 = pl.pallas_call(
        _run_plan_kernel,
        grid=(ntiles,),
        in_specs=[pl.BlockSpec((tt, k), lambda i: (i, 0))],
        out_specs=[pl.BlockSpec((tt, k), lambda i: (i, 0)), tab, tab, tab,
                   pl.BlockSpec((1, ne), lambda i: (0, 0))],
        out_shape=[jax.ShapeDtypeStruct((t, k), I32), tab_shape, tab_shape, tab_shape,
                   jax.ShapeDtypeStruct((1, ne), F32)],
        scratch_shapes=[pltpu.VMEM((1, ne), F32)],
        compiler_params=_params(1),
        name="moe_run_plan",
    )(te)
    nbp = -(-nblocks // LANES) * LANES
    per_expert = jax.ShapeDtypeStruct((1, ne), I32)
    dbase8, tail8, ntail8, be, nv = pl.pallas_call(
        functools.partial(_block_plan_kernel, bm=bm),
        out_shape=[jax.ShapeDtypeStruct((ntiles, ne), I32), per_expert, per_expert,
                   jax.ShapeDtypeStruct((1, nbp), I32), jax.ShapeDtypeStruct((1, LANES), I32)],
        name="moe_block_plan",
    )(tot8, dpre8.reshape(ntiles, ne))
    runs = (n8.reshape(-1), soff8.reshape(-1), dbase8.reshape(-1))
    be, nv = be.reshape(-1), nv.reshape(-1)
    tails = (tail8.reshape(-1), ntail8.reshape(-1), nv)
    return lp, runs, tails, be, nv


def _for_each_run_chunk(tables, tile, ne, max8, fn):
    n8_ref, soff8_ref, dbase8_ref = tables

    def per_expert(e, carry):
        idx = tile * ne + e
        n, o, d = n8_ref[idx], soff8_ref[idx], dbase8_ref[idx]
        bit = max8
        while bit >= 1:
            @pl.when((n & bit) != 0)
            def _(o=o, d=d, bit=bit):
                fn(pl.multiple_of(o * SUBLANES, SUBLANES), pl.multiple_of(d * SUBLANES, SUBLANES), bit * SUBLANES)
            o = o + (n & bit)
            d = d + (n & bit)
            bit //= 2
        return carry

    lax.fori_loop(0, ne, per_expert, 0)


def _dispatch_runs_kernel(n8_ref, soff8_ref, dbase8_ref, tail8_ref, ntail8_ref, nv_ref, lpt_ref, *rest,
                          ne, group_tiles, max_tail8):
    src_refs = rest[:len(group_tiles)]
    xr_ref, xs, zb, sem = rest[len(group_tiles):]
    tile = pl.program_id(0)

    @pl.when(tile == 0)
    def _():
        zb[...] = jnp.zeros(zb.shape, F32)
        half = zb.shape[0]
        n_half = xr_ref.shape[0] // half

        def zero_unused(issue):
            def per_half(h, carry):
                issue(pltpu.make_async_copy(zb, xr_ref.at[pl.ds(pl.multiple_of(h * half, half), half)], sem))
                return carry
            lax.fori_loop(2 * nv_ref[0], n_half, per_half, 0)

        def zero_tails(issue):
            def per_expert(e, carry):
                n, d = ntail8_ref[e], tail8_ref[e]
                bit = max_tail8
                while bit >= 1:
                    @pl.when((n & bit) != 0)
                    def _(d=d, bit=bit):
                        issue(pltpu.make_async_copy(
                            zb.at[pl.ds(0, bit * SUBLANES)],
                            xr_ref.at[pl.ds(pl.multiple_of(d * SUBLANES, SUBLANES), bit * SUBLANES)], sem))
                    d = d + (n & bit)
                    bit //= 2
                return carry
            lax.fori_loop(0, ne, per_expert, 0)

        for fill in (zero_unused, zero_tails):
            fill(lambda cp: cp.start())
            fill(lambda cp: cp.wait())

    src = src_refs[-1][...]
    first_tile = sum(group_tiles)
    for g in range(len(group_tiles) - 2, -1, -1):
        first_tile -= group_tiles[g + 1]
        src = jnp.where(tile < first_tile, src_refs[g][...], src)
    tt = src.shape[0]
    rows = xs.shape[0]
    r = lax.broadcasted_iota(I32, (rows, tt), 0)
    lpt = lpt_ref[...]
    sel = jnp.zeros((rows, tt), F32)
    for k in range(lpt.shape[0]):
        sel = sel + jnp.where(r == lpt[k:k + 1, :], 1.0, 0.0)
    xs[...] = _dot(sel.astype(BF16), src.astype(BF16))

    def copy(o, d, n):
        return pltpu.make_async_copy(xs.at[pl.ds(o, n)], xr_ref.at[pl.ds(d, n)], sem)

    tables = (n8_ref, soff8_ref, dbase8_ref)
    _for_each_run_chunk(tables, tile, ne, tt // SUBLANES, lambda o, d, n: copy(o, d, n).start())
    _for_each_run_chunk(tables, tile, ne, tt // SUBLANES, lambda o, d, n: copy(o, d, n).wait())


def _dispatch_runs(runs, tails, lpt, srcs, ne, tt, xr_rows, bm):
    d = srcs[0].shape[1]
    k = lpt.shape[0]
    group_tiles = tuple(src.shape[0] // tt for src in srcs)
    assert all(src.shape[0] % tt == 0 for src in srcs) and sum(group_tiles) * tt == lpt.shape[1]
    max_tail8 = bm // SUBLANES // 2
    in_specs = [pl.BlockSpec((k, tt), lambda i, *_: (0, i))]
    first = 0
    for n in group_tiles:
        in_specs.append(pl.BlockSpec((tt, d), lambda i, *_, first=first, n=n: (jnp.clip(i - first, 0, n - 1), 0)))
        first += n
    grid_spec = pltpu.PrefetchScalarGridSpec(
        num_scalar_prefetch=len(runs) + len(tails),
        grid=(sum(group_tiles),),
        in_specs=in_specs,
        out_specs=pl.BlockSpec(memory_space=pl.ANY),
        scratch_shapes=[pltpu.VMEM((k * tt + SUBLANES * ne, d), F32),
                        pltpu.VMEM((max_tail8 * SUBLANES, d), F32), pltpu.SemaphoreType.DMA(())],
    )
    return pl.pallas_call(
        functools.partial(_dispatch_runs_kernel, ne=ne, group_tiles=group_tiles, max_tail8=max_tail8),
        grid_spec=grid_spec,
        out_shape=jax.ShapeDtypeStruct((xr_rows, d), F32),
        compiler_params=_params(1),
        name="moe_dispatch",
    )(*runs, *tails, lpt, *srcs)


def _combine_runs_kernel(n8_ref, soff8_ref, dbase8_ref, lp_ref, yr_ref, gate_ref, x1_ref, gt2_ref, g_ref,
                         o_ref, ys, sem, *, ne, tile_offset):
    tt = lp_ref.shape[0]
    rows = ys.shape[0]
    ys[rows - SUBLANES * ne:rows, :] = jnp.zeros((SUBLANES * ne, ys.shape[1]), F32)

    def copy(o, d, n):
        return pltpu.make_async_copy(yr_ref.at[pl.ds(d, n)], ys.at[pl.ds(o, n)], sem)

    tables = (n8_ref, soff8_ref, dbase8_ref)
    tile = pl.program_id(0) + tile_offset
    _for_each_run_chunk(tables, tile, ne, tt // SUBLANES, lambda o, d, n: copy(o, d, n).start())
    c = lax.broadcasted_iota(I32, (tt, rows), 1)
    lp = lp_ref[...]
    gate = gate_ref[...]
    mix = jnp.zeros((tt, rows), F32)
    for k in range(lp.shape[1]):
        mix = mix + jnp.where(c == lp[:, k:k + 1], gate[:, k:k + 1], 0.0)
    mix_hi, mix_lo = _split2(mix)
    _for_each_run_chunk(tables, tile, ne, tt // SUBLANES, lambda o, d, n: copy(o, d, n).wait())
    yb = ys[...].astype(BF16)
    f = _dot(mix_hi, yb) + _dot(mix_lo, yb)
    o_ref[...] = x1_ref[...] + gt2_ref[...] * _rms(f, g_ref[...])


def _combine_runs(tables, tile_offset, lp, yr, gate, x1, gt2, g, rows_per_batch, ne, tt):
    t, d = x1.shape
    k = lp.shape[1]
    assert t % tt == 0
    row = pl.BlockSpec((tt, d), lambda i, *_: (i, 0))
    grid_spec = pltpu.PrefetchScalarGridSpec(
        num_scalar_prefetch=3,
        grid=(t // tt,),
        in_specs=[pl.BlockSpec((tt, k), lambda i, *_: (i + tile_offset, 0)),
                  pl.BlockSpec(memory_space=pl.ANY),
                  pl.BlockSpec((tt, k), lambda i, *_: (i, 0)),
                  row, _mod_spec(gt2, tt, rows_per_batch),
                  pl.BlockSpec((1, d), lambda i, *_: (0, 0))],
        out_specs=row,
        scratch_shapes=[pltpu.VMEM((k * tt + SUBLANES * ne, d), F32), pltpu.SemaphoreType.DMA(())],
    )
    return pl.pallas_call(
        functools.partial(_combine_runs_kernel, ne=ne, tile_offset=tile_offset),
        grid_spec=grid_spec,
        out_shape=jax.ShapeDtypeStruct((t, d), F32),
        compiler_params=_params(1),
        name="moe_combine",
    )(*tables, lp, yr, gate, x1, gt2, g)


def _blockdiag_pairs(wa, wx, gw):
    nb, c, _ = wa.shape
    per = gw // c
    ng = nb // per
    eye = jnp.eye(per, dtype=wa.dtype)

    def bd(w):
        return jnp.einsum("gpcd,pq->gpcqd", w.reshape(ng, per, c, c), eye).reshape(ng, gw, gw)

    return jnp.concatenate([bd(wa), bd(wx)], axis=2).astype(BF16)


def _layer(xp, xs, ck, cv, clf, sconv, sh0, page_table, cp, cs, w):
    bp, s, d = xp.shape
    bs, ds_, _ = xs.shape
    nh = w["b_forget"].shape[0]
    hd = d // nh
    tp, ts = bp * s, bs * ds_
    kw = w["conv_w"].shape[0]
    ne = w["w_router"].shape[1]

    w_in = w["w_in"].astype(BF16)
    w_x, w_q, w_k, w_v = (w_in[:, n * d:(n + 1) * d] for n in range(4))
    w_ga, w_gb = w_in[:, 4 * d + nh:5 * d + nh], w_in[:, 5 * d + nh:]
    w4 = jnp.stack([w_x, w_q, w_ga, w_gb])
    wkvt = jnp.stack([w_k.T, w_v.T])
    w6 = jnp.stack([w_x, w_q, w_ga, w_gb, w_k, w_v])
    wft = w_in[:, 4 * d:4 * d + nh].T
    bfo = w["b_forget"].reshape(nh, 1)
    gw = min(2 * LANES, d)
    wbd = _blockdiag_pairs(w["rg_w_a"], w["rg_w_x"], gw)
    vec = lambda v: v.reshape(1, d)
    wo = w["w_out"].astype(BF16)
    wgu, wdn = w["w_gate_up"], w["w_down"]

    ada = _ada(jnp.concatenate([cp, cs], axis=0), w["w_ada"], w["b_ada"])
    mods_p = [m.reshape(bp, 1, d) for m in jnp.split(ada[:bp], 6, axis=-1)]
    mods_s = [jnp.repeat(m, ds_, axis=0) for m in jnp.split(ada[bp:], 6, axis=-1)]

    xpf, xsf = xp.reshape(tp, d), xs.reshape(ts, d)
    g_pre = vec(w["g_mix_pre"])
    zp, kt_p, vt_p, lft_p = _inproj(xpf, mods_p[1], mods_p[0], g_pre, w4, wkvt, wft, bfo, bp, s)
    zs, lft_s = _inproj(xsf, mods_s[1], mods_s[0], g_pre, w6, None, wft, bfo, 1, ds_)

    rnn_w = (w["conv_w"], vec(w["conv_b"]), wbd, vec(w["rg_b_a"]), vec(w["rg_b_x"]), vec(w["rg_lambda"]))
    zero_prev = jnp.zeros((bp, SUBLANES, d), F32)
    zero_h = jnp.zeros((bp, 1, d), F32)
    yr_p, conv_p, hl_p = _rnn_prompt(zp, zero_prev, zero_h, *rnn_w, reset_first=True)
    prev8 = jnp.pad(sconv, ((0, 0), (SUBLANES - (kw - 1), 0), (0, 0))).reshape(ts, d)
    h08 = jnp.pad(sh0[:, None, :], ((0, 0), (0, SUBLANES - 1), (0, 0))).reshape(ts, d)
    yr_s = _rnn_sample(zs, prev8, h08, *rnn_w)
    hl_s = yr_s.reshape(bs, ds_, d)[:, ds_ - 1]
    conv_s = zs[0].reshape(bs, ds_, d)[:, ds_ - (kw - 1):]

    ft_p = _cumsum_lanes(lft_p).reshape(bp, nh // 2, 2, s)
    ya_p = _fox_prompt(zp.reshape(4, bp, s, d), kt_p, vt_p, ft_p, hd).reshape(tp, d)
    cn_s = _cumsum_lanes(lft_s, seg=ds_).reshape(nh, bs, ds_).transpose(1, 0, 2)
    npool, page = ck.shape[0], ck.shape[1]
    ckt = ck.transpose(0, 2, 3, 1).reshape(npool, d, page)
    cvt = cv.transpose(0, 2, 3, 1).reshape(npool, d, page)
    ya_s = _fox_sample(page_table, zs, (1, 4, 5), cn_s, ckt, cvt, clf.transpose(0, 2, 1), nh, hd)

    post_w = (wo, vec(w["g_mix_post"]), vec(w["g_ffn_pre"]), w["w_router"], w["b_router"].reshape(1, ne))
    x1_p, h2_p, te_p, gate_p = _post(yr_p, ya_p, zp, xpf, mods_p[2], mods_p[4], mods_p[3], *post_w, s)
    x1_s, h2_s, te_s, gate_s = _post(yr_s, ya_s, zs, xsf, mods_s[2], mods_s[4], mods_s[3], *post_w, ds_)

    bm = 512
    t_all = tp + ts
    tt = _moe_tile(t_all)
    assert tp % tt == 0 and ts % tt == 0
    run_pad = (SUBLANES - 1) * (t_all // tt) * ne
    nblocks = -(-(t_all * TOP_K + run_pad) // bm) + ne
    lp, tables, tails, be, nv = _plan_routes(jnp.concatenate([te_p, te_s], axis=0), ne, bm, nblocks, tt)
    lpt = lp.T
    xr = _dispatch_runs(tables, tails, lpt, (h2_p, h2_s), ne, tt, nblocks * bm, bm)
    yr = _experts(be, nv, xr, wgu, w["b_gate_up"], wdn, w["b_down"], bm)
    g_post = vec(w["g_ffn_post"])
    y_p = _combine_runs(tables, 0, lp, yr, gate_p, x1_p, mods_p[5], g_post, s, ne, tt)
    y_s = _combine_runs(tables, tp // tt, lp, yr, gate_s, x1_s, mods_s[5], g_post, ds_, ne, tt)

    heads_t = lambda zt_: zt_.reshape(bp, nh, hd, s).transpose(0, 3, 1, 2)
    out_p = (y_p.reshape(bp, s, d), heads_t(kt_p), heads_t(vt_p),
             lft_p.transpose(0, 2, 1), conv_p, hl_p.reshape(bp, d))
    out_s = (y_s.reshape(bs, ds_, d), zs[4].reshape(bs, ds_, nh, hd), zs[5].reshape(bs, ds_, nh, hd),
             lft_s[0].T.reshape(bs, ds_, nh), conv_s, hl_s)
    return out_p, out_s


def kernel(x_prompt, x_sample, cache_k, cache_v, cache_logf, state_conv, state_h, page_table, c_prompt, c_sample, w_ada, b_ada, g_mix_pre, g_mix_post, w_in, b_forget, conv_w, conv_b, rg_w_a, rg_b_a, rg_w_x, rg_b_x, rg_lambda, w_out, g_ffn_pre, g_ffn_post, w_router, b_router, w_gate_up, b_gate_up, w_down, b_down):
    weights = dict(w_ada=w_ada, b_ada=b_ada, g_mix_pre=g_mix_pre, g_mix_post=g_mix_post, w_in=w_in,
                   b_forget=b_forget, conv_w=conv_w, conv_b=conv_b, rg_w_a=rg_w_a, rg_b_a=rg_b_a,
                   rg_w_x=rg_w_x, rg_b_x=rg_b_x, rg_lambda=rg_lambda, w_out=w_out, g_ffn_pre=g_ffn_pre,
                   g_ffn_post=g_ffn_post, w_router=w_router, b_router=b_router, w_gate_up=w_gate_up,
                   b_gate_up=b_gate_up, w_down=w_down, b_down=b_down)
    depth = w_ada.shape[0]
    yp, ys = x_prompt, x_sample
    per_layer = []
    for l in range(depth):
        wl = {k: v[l] for k, v in weights.items()}
        out_p, out_s = _layer(yp, ys, cache_k[l], cache_v[l], cache_logf[l], state_conv[l], state_h[l],
                              page_table, c_prompt, c_sample, wl)
        yp, ys = out_p[0], out_s[0]
        per_layer.append(out_p[1:] + out_s[1:])
    stacked = [jnp.stack(leaf) for leaf in zip(*per_layer)]
    return (yp, ys, *stacked)
```

```python
import functools

import jax
import jax.numpy as jnp
from jax import lax
from jax.experimental import pallas as pl
from jax.experimental.pallas import tpu as pltpu

F32 = jnp.float32
BF16 = jnp.bfloat16
I32 = jnp.int32

EPS = 1e-6
LRU_C = 8.0
TOP_K = 4
SWIGLU_LIMIT = 7.0
SWIGLU_ALPHA = 1.702
NEG_BIG = -1e30

LANES = 128
SUBLANES = 8
VMEM_LIMIT = 56 * 1024 * 1024

ADA_COLS = 1536
IN_PROJ_TILE = 1024
RNN_CHUNK = 512
RNN_SAMPLE_ROWS = 16
ATTN_BLOCK = 512
PAGES_PER_STEP = 16
PAGES_PER_SOFTMAX = 4
POST_TILE = 512
MOE_TILE = 256
EXPERT_BLOCK = 512
EXPERT_FF_CHUNK = 512


def _params(n_grid_dims):
    return pltpu.CompilerParams(
        dimension_semantics=("arbitrary",) * n_grid_dims, vmem_limit_bytes=VMEM_LIMIT)


def _dot(a, b):
    return jnp.dot(a, b, preferred_element_type=F32)


def _dot_nt(a, b):
    return lax.dot_general(a, b, (((1,), (1,)), ((), ())), preferred_element_type=F32)


def _split2(x):
    hi = x.astype(BF16)
    lo = (x - hi.astype(F32)).astype(BF16)
    return hi, lo


def _split3(x):
    p1 = x.astype(BF16)
    r1 = x - p1.astype(F32)
    p2 = r1.astype(BF16)
    p3 = (r1 - p2.astype(F32)).astype(BF16)
    return p1, p2, p3


def _dot3(a, b):
    ah, al = _split2(a)
    bh, bl = _split2(b)
    return _dot(ah, bh) + _dot(ah, bl) + _dot(al, bh)


def _dot_exact01(x, u01):
    p1, p2, p3 = _split3(x)
    return _dot(p1, u01) + _dot(p2, u01) + _dot(p3, u01)


def _rms(x, g):
    ms = jnp.mean(x * x, axis=-1, keepdims=True)
    return x * lax.rsqrt(ms + EPS) * g


def _softplus(x):
    return jnp.maximum(x, 0.0) + jnp.log1p(jnp.exp(-jnp.abs(x)))


def _log_sigmoid(x):
    return jnp.minimum(x, 0.0) - jnp.log1p(jnp.exp(-jnp.abs(x)))


def _ada_kernel(c_ref, w_ref, b_ref, o_ref):
    c = c_ref[...]
    o_ref[...] = _dot3(c * jax.nn.sigmoid(c), w_ref[...]) + b_ref[...]


def _ada(c, w, b):
    n, d = c.shape
    nout = w.shape[1]
    tn = min(nout, ADA_COLS)
    assert nout % tn == 0
    return pl.pallas_call(
        _ada_kernel,
        grid=(nout // tn,),
        in_specs=[pl.BlockSpec((n, d), lambda j: (0, 0)),
                  pl.BlockSpec((d, tn), lambda j: (0, j)),
                  pl.BlockSpec((1, tn), lambda j: (0, j))],
        out_specs=pl.BlockSpec((n, tn), lambda j: (0, j)),
        out_shape=jax.ShapeDtypeStruct((n, nout), F32),
        compiler_params=_params(1),
        name="ada",
    )(c, w, b.reshape(1, nout))


def _inproj_kernel(x_ref, sc_ref, sh_ref, g_ref, w_ref, wft_ref, bf_ref, *rest, nz, nt):
    if nt:
        wt_ref, z_ref = rest[:2]
        zt_refs = rest[2:2 + nt]
    else:
        z_ref = rest[0]
    lft_ref, h_scr = rest[-2:]
    j = pl.program_id(1)

    @pl.when(j == 0)
    def _():
        h = _rms(x_ref[...], g_ref[...]) * (1.0 + sc_ref[...]) + sh_ref[...]
        hb = h.astype(BF16)
        h_scr[...] = hb
        lft_ref[...] = _log_sigmoid(_dot_nt(wft_ref[...], hb) + bf_ref[...])

    if nt:
        @pl.when(j < nz)
        def _():
            z_ref[...] = _dot(h_scr[...], w_ref[jnp.minimum(j, nz - 1)])

        for n in range(nt):
            @pl.when(j == nz + n)
            def _(n=n):
                zt_refs[n][...] = _dot_nt(wt_ref[n], h_scr[...])
    else:
        z_ref[...] = _dot(h_scr[...], w_ref[j])


def _mod_spec(mod, tm, rows_per_batch):
    if mod.ndim == 3:
        tiles_per_batch = rows_per_batch // tm
        return pl.BlockSpec((None, 1, mod.shape[-1]), lambda i, *_: (i // tiles_per_batch, 0, 0))
    return pl.BlockSpec((tm, mod.shape[-1]), lambda i, *_: (i, 0))


def _inproj(x, sc, sh, g, w, wt, wft, bfo, nbatch, rows_per_batch):
    t, d = x.shape
    nh = wft.shape[0]
    nz = w.shape[0]
    nt = 0 if wt is None else wt.shape[0]
    s = t // nbatch
    tm = min(IN_PROJ_TILE, s)
    assert s % tm == 0 and (sc.ndim == 2 or rows_per_batch % tm == 0)
    tpb = s // tm
    resident = lambda n: pl.BlockSpec((n, d, d), lambda i, j: (0, 0, 0), pipeline_mode=pl.Buffered(1))
    in_specs = [pl.BlockSpec((tm, d), lambda i, j: (i, 0)),
                _mod_spec(sc, tm, rows_per_batch), _mod_spec(sh, tm, rows_per_batch),
                pl.BlockSpec((1, d), lambda i, j: (0, 0)),
                resident(nz),
                pl.BlockSpec((nh, d), lambda i, j: (0, 0)),
                pl.BlockSpec((nh, 1), lambda i, j: (0, 0))]
    out_specs = [pl.BlockSpec((None, tm, d), lambda i, j: (jnp.minimum(j, nz - 1), i, 0))]
    out_shape = [jax.ShapeDtypeStruct((nz, t, d), F32)]
    args = [x, sc, sh, g, w, wft, bfo]
    if nt:
        in_specs.append(resident(nt))
        args.append(wt)
        for _ in range(nt):
            out_specs.append(pl.BlockSpec((None, d, tm), lambda i, j: (i // tpb, 0, i % tpb)))
            out_shape.append(jax.ShapeDtypeStruct((nbatch, d, s), F32))
    out_specs.append(pl.BlockSpec((None, nh, tm), lambda i, j: (i // tpb, 0, i % tpb)))
    out_shape.append(jax.ShapeDtypeStruct((nbatch, nh, s), F32))
    return pl.pallas_call(
        functools.partial(_inproj_kernel, nz=nz, nt=nt),
        grid=(t // tm, nz + nt),
        in_specs=in_specs,
        out_specs=out_specs,
        out_shape=out_shape,
        scratch_shapes=[pltpu.VMEM((tm, d), BF16)],
        compiler_params=_params(2),
        name="inproj",
    )(*args)


def _cumsum_kernel(lf_ref, o_ref, *, cw, seg):
    n = lf_ref.shape[1]
    r = lax.broadcasted_iota(I32, (cw, cw), 0)
    c = lax.broadcasted_iota(I32, (cw, cw), 1)
    keep = r <= c
    if seg is not None:
        keep = jnp.logical_and(keep, (r // seg) == (c // seg))
    u01 = jnp.where(keep, 1.0, 0.0).astype(BF16)
    carry = jnp.zeros((lf_ref.shape[0], 1), F32)
    for i in range(n // cw):
        f = _dot_exact01(lf_ref[:, i * cw:(i + 1) * cw], u01)
        if seg is None:
            f = f + carry
            carry = f[:, cw - 1:cw]
        o_ref[:, i * cw:(i + 1) * cw] = f


def _cumsum_lanes(lft, seg=None):
    nb, nh, s = lft.shape
    block = s if seg is None else min(s, 2 * LANES)
    cw = min(2 * LANES, block)
    assert s % block == 0 and block % cw == 0 and (seg is None or cw % seg == 0)
    spec = pl.BlockSpec((None, nh, block), lambda b, n: (b, 0, n))
    return pl.pallas_call(
        functools.partial(_cumsum_kernel, cw=cw, seg=seg),
        grid=(nb, s // block),
        in_specs=[spec],
        out_specs=spec,
        out_shape=jax.ShapeDtypeStruct((nb, nh, s), F32),
        compiler_params=_params(2),
        name="cumsum_logf",
    )(lft)


def _rglru_coeffs(xc, wbd_ref, ba, bx, lam):
    ng, gw = wbd_ref.shape[0], wbd_ref.shape[1]
    xb = xc.astype(BF16)
    ra, ri = [], []
    for g in range(ng):
        o = _dot(xb[:, g * gw:(g + 1) * gw], wbd_ref[g])
        ra.append(o[:, :gw])
        ri.append(o[:, gw:])
    r = jax.nn.sigmoid((ra[0] if ng == 1 else jnp.concatenate(ra, axis=1)) + ba)
    i = jax.nn.sigmoid((ri[0] if ng == 1 else jnp.concatenate(ri, axis=1)) + bx)
    log_a = -LRU_C * r * _softplus(-lam)
    a = jnp.exp(log_a)
    th = jnp.tanh(log_a)
    mult = jnp.sqrt(-2.0 * th / (1.0 - th))
    return a, mult, i


def _scan8(a8, u8, row):
    for s in (1, 2, 4):
        a_sh = pltpu.roll(a8, s, axis=0)
        u_sh = pltpu.roll(u8, s, axis=0)
        m = row >= s
        u8 = u8 + a8 * jnp.where(m, u_sh, 0.0)
        a8 = a8 * jnp.where(m, a_sh, 1.0)
    return a8, u8


def _rnn_prompt_kernel(x_ref, prev_ref, h0_ref, cw_ref, cb_ref, wbd_ref, ba_ref, bx_ref, lam_ref,
                       y_ref, cs_ref, hl_ref, ext_scr, a_scr, u_scr, h_scr, *, reset_first):
    c = pl.program_id(1)
    tc, d = x_ref.shape
    kw = cw_ref.shape[0]

    @pl.when(c == 0)
    def _():
        ext_scr[0:SUBLANES, :] = prev_ref[...]
        h_scr[...] = h0_ref[...]

    ext_scr[SUBLANES:SUBLANES + tc, :] = x_ref[...]
    w = cw_ref[...]
    xc = cb_ref[...]
    for j in range(kw):
        off = SUBLANES - (kw - 1) + j
        xc = xc + w[j:j + 1, :] * ext_scr[off:off + tc, :]
    ext_scr[0:SUBLANES, :] = ext_scr[tc:tc + SUBLANES, :]

    a, mult, i = _rglru_coeffs(xc, wbd_ref, ba_ref[...], bx_ref[...], lam_ref[...])
    if reset_first:
        row = lax.broadcasted_iota(I32, (tc, 1), 0)
        mult = jnp.where(row == jnp.where(c == 0, 0, -1), 1.0, mult)
    a_scr[...] = a
    u_scr[...] = mult * i * xc

    row8 = lax.broadcasted_iota(I32, (SUBLANES, d), 0)

    def body(g, h):
        r0 = pl.multiple_of(g * SUBLANES, SUBLANES)
        a8, u8 = _scan8(a_scr[pl.ds(r0, SUBLANES), :], u_scr[pl.ds(r0, SUBLANES), :], row8)
        h8 = u8 + a8 * h
        y_ref[pl.ds(r0, SUBLANES), :] = h8
        return h8[SUBLANES - 1:SUBLANES, :]

    h = lax.fori_loop(0, tc // SUBLANES, body, h_scr[...], unroll=4)
    h_scr[...] = h

    @pl.when(c == pl.num_programs(1) - 1)
    def _():
        hl_ref[...] = h
        cs_ref[...] = x_ref[tc - (kw - 1):tc, :]


def _rnn_prompt(x3, prev8, h0, cw, cb, wbd, ba, bx, lam, reset_first):
    b = prev8.shape[0]
    t, d = x3.shape[1], x3.shape[2]
    s = t // b
    kw = cw.shape[0]
    tc = min(RNN_CHUNK, s)
    nc = s // tc
    assert s % tc == 0 and kw - 1 <= SUBLANES and tc % SUBLANES == 0
    vec = pl.BlockSpec((1, d), lambda bi, ci: (0, 0))
    return pl.pallas_call(
        functools.partial(_rnn_prompt_kernel, reset_first=reset_first),
        grid=(b, nc),
        in_specs=[pl.BlockSpec((None, tc, d), lambda bi, ci: (0, bi * nc + ci, 0)),
                  pl.BlockSpec((None, SUBLANES, d), lambda bi, ci: (bi, 0, 0)),
                  pl.BlockSpec((None, 1, d), lambda bi, ci: (bi, 0, 0)),
                  pl.BlockSpec((kw, d), lambda bi, ci: (0, 0)), vec,
                  pl.BlockSpec(wbd.shape, lambda bi, ci: (0, 0, 0)), vec, vec, vec],
        out_specs=[pl.BlockSpec((tc, d), lambda bi, ci: (bi * nc + ci, 0)),
                   pl.BlockSpec((None, kw - 1, d), lambda bi, ci: (bi, 0, 0)),
                   pl.BlockSpec((None, 1, d), lambda bi, ci: (bi, 0, 0))],
        out_shape=[jax.ShapeDtypeStruct((t, d), F32),
                   jax.ShapeDtypeStruct((b, kw - 1, d), F32),
                   jax.ShapeDtypeStruct((b, 1, d), F32)],
        scratch_shapes=[pltpu.VMEM((tc + SUBLANES, d), F32), pltpu.VMEM((tc, d), F32),
                        pltpu.VMEM((tc, d), F32), pltpu.VMEM((1, d), F32)],
        compiler_params=_params(2),
        name="rnn_prompt",
    )(x3, prev8, h0, cw, cb, wbd, ba, bx, lam)


def _rnn_sample_kernel(x_ref, prev_ref, h0_ref, cw_ref, cb_ref, wbd_ref, ba_ref, bx_ref, lam_ref,
                       y_ref):
    r, d = x_ref.shape
    kw = cw_ref.shape[0]
    x = x_ref[...]
    p = prev_ref[...]
    step = lax.broadcasted_iota(I32, (r, 1), 0) & (SUBLANES - 1)
    w = cw_ref[...]
    xc = cb_ref[...]
    for j in range(kw):
        back = kw - 1 - j
        if back == 0:
            xs = x
        else:
            xs = jnp.where(step >= back, pltpu.roll(x, back, axis=0),
                           pltpu.roll(p, (r - SUBLANES + back) % r, axis=0))
        xc = xc + w[j:j + 1, :] * xs
    a, mult, i = _rglru_coeffs(xc, wbd_ref, ba_ref[...], bx_ref[...], lam_ref[...])
    u = mult * i * xc + a * h0_ref[...]
    for s in (1, 2, 4):
        m = step >= s
        a_sh = pltpu.roll(a, s, axis=0)
        u_sh = pltpu.roll(u, s, axis=0)
        u = u + a * jnp.where(m, u_sh, 0.0)
        a = a * jnp.where(m, a_sh, 1.0)
    y_ref[...] = u


def _rnn_sample(x3, prev8, h08, cw, cb, wbd, ba, bx, lam):
    t, d = x3.shape[1], x3.shape[2]
    nb_total = t // SUBLANES
    kw = cw.shape[0]
    nb = min(RNN_SAMPLE_ROWS, nb_total)
    r = nb * SUBLANES
    assert nb_total % nb == 0 and nb % SUBLANES == 0
    vec = pl.BlockSpec((1, d), lambda i: (0, 0))
    return pl.pallas_call(
        _rnn_sample_kernel,
        grid=(nb_total // nb,),
        in_specs=[pl.BlockSpec((None, r, d), lambda i: (0, i, 0)),
                  pl.BlockSpec((r, d), lambda i: (i, 0)),
                  pl.BlockSpec((r, d), lambda i: (i, 0)),
                  pl.BlockSpec((kw, d), lambda i: (0, 0)), vec,
                  pl.BlockSpec(wbd.shape, lambda i: (0, 0, 0)), vec, vec, vec],
        out_specs=pl.BlockSpec((r, d), lambda i: (i, 0)),
        out_shape=jax.ShapeDtypeStruct((t, d), F32),
        compiler_params=_params(1),
        name="rnn_sample",
    )(x3, prev8, h08, cw, cb, wbd, ba, bx, lam)


def _fox_prompt_kernel(q_ref, kt_ref, vt_ref, f_ref, o_ref, *, hd, scale):
    i = pl.program_id(2)
    tq = q_ref.shape[0]
    lane = lax.broadcasted_iota(I32, (1, 2 * hd), 1)
    q = q_ref[...] * scale
    qs = jnp.concatenate([jnp.where(lane < hd, q, 0.0), jnp.where(lane >= hd, q, 0.0)], axis=0).astype(BF16)

    def scores(kb):
        k0 = pl.multiple_of(kb * tq, tq)
        kt = kt_ref[:, pl.ds(k0, tq)].astype(BF16)
        fb = f_ref[:, pl.ds(k0, tq)]
        s = _dot(qs, kt)
        return jnp.concatenate([s[:tq] - fb[0:1, :], s[tq:] - fb[1:2, :]], axis=0)

    vrow = lax.broadcasted_iota(I32, (2 * hd, 1), 0)

    def update(kb, s, m, acc):
        k0 = pl.multiple_of(kb * tq, tq)
        vt = vt_ref[:, pl.ds(k0, tq)]
        vt0 = jnp.where(vrow < hd, vt, 1.0).astype(BF16)
        vt1 = jnp.where(vrow >= hd, vt, 1.0).astype(BF16)
        m_new = jnp.maximum(m, jnp.max(s, axis=-1, keepdims=True))
        p = jnp.exp(s - m_new).astype(BF16)
        pv = jnp.concatenate([_dot_nt(p[:tq], vt0), _dot_nt(p[tq:], vt1)], axis=0)
        return m_new, jnp.exp(m - m_new) * acc + pv

    def body(kb, carry):
        s, m, acc = carry
        return (scores(kb + 1),) + update(kb, s, m, acc)

    init = (scores(0), jnp.full((2 * tq, 1), NEG_BIG, F32), jnp.zeros((2 * tq, 2 * hd), F32))
    s, m, acc = lax.fori_loop(0, i, body, init)
    r = lax.broadcasted_iota(I32, (2 * tq, tq), 0) & (tq - 1)
    c = lax.broadcasted_iota(I32, (2 * tq, tq), 1)
    _, acc = update(i, jnp.where(c <= r, s, NEG_BIG), m, acc)
    o = acc / pltpu.roll(acc, hd, axis=1)
    o_ref[...] = jnp.where(lane < hd, o[:tq], o[tq:])


def _fox_prompt(z4, kt, vt, ft4, hd):
    _, b, s, d = z4.shape
    assert 2 * hd == LANES and d % LANES == 0
    npair = d // LANES
    tq = min(ATTN_BLOCK, s)
    assert s % tq == 0 and tq & (tq - 1) == 0
    return pl.pallas_call(
        functools.partial(_fox_prompt_kernel, hd=hd, scale=hd ** -0.5),
        grid=(b, npair, s // tq),
        in_specs=[pl.BlockSpec((None, None, tq, LANES), lambda bi, j, i: (1, bi, i, j)),
                  pl.BlockSpec((None, LANES, s), lambda bi, j, i: (bi, j, 0)),
                  pl.BlockSpec((None, LANES, s), lambda bi, j, i: (bi, j, 0)),
                  pl.BlockSpec((None, None, 2, s), lambda bi, j, i: (bi, j, 0, 0))],
        out_specs=pl.BlockSpec((None, tq, LANES), lambda bi, j, i: (bi, i, j)),
        out_shape=jax.ShapeDtypeStruct((b, s, d), F32),
        compiler_params=_params(3),
        name="fox_prompt",
    )(z4, kt, vt, ft4)


def _fox_sample_kernel(pt_ref, q_ref, kn_ref, vn_ref, cn_ref, *rest, nh, hd, scale, pps, grp):
    del pt_ref
    k_refs, v_refs, lf_refs = rest[:pps], rest[pps:2 * pps], rest[2 * pps:3 * pps]
    o_ref, qx_scr, m_scr, l_scr, acc_scr, fcar_scr = rest[3 * pps:]
    j = pl.program_id(1)
    ds_, d = q_ref.shape
    page = lf_refs[0].shape[1]
    nr = nh * ds_

    def expand(x):
        return jnp.broadcast_to(x[:, None, :], (nh, ds_, x.shape[1])).reshape(nr, x.shape[1])

    def own_head():
        rh = lax.broadcasted_iota(I32, (nr, d), 0) // ds_
        ch = lax.broadcasted_iota(I32, (nr, d), 1) // hd
        return rh == ch

    @pl.when(j == 0)
    def _():
        q = q_ref[...] * scale
        qt = jnp.broadcast_to(q[None], (nh, ds_, d)).reshape(nr, d)
        qx_scr[...] = jnp.where(own_head(), qt, 0.0).astype(BF16)
        m_scr[...] = jnp.full(m_scr.shape, NEG_BIG, F32)
        l_scr[...] = jnp.zeros(l_scr.shape, F32)
        acc_scr[...] = jnp.zeros(acc_scr.shape, F32)
        fcar_scr[...] = jnp.zeros(fcar_scr.shape, F32)

    def update(s, pv_fn):
        m_old = m_scr[...]
        m_new = jnp.maximum(m_old, jnp.max(s, axis=-1, keepdims=True))
        p = jnp.exp(s - m_new)
        alpha = jnp.exp(m_old - m_new)
        l_scr[...] = alpha * l_scr[...] + jnp.sum(p, axis=-1, keepdims=True)
        m_scr[...] = m_new
        acc_scr[...] = alpha * acc_scr[...] + pv_fn(p.astype(BF16))

    r = lax.broadcasted_iota(I32, (page, page), 0)
    c = lax.broadcasted_iota(I32, (page, page), 1)
    u01 = jnp.where(r <= c, 1.0, 0.0).astype(BF16)
    qx = qx_scr[...]
    fcar = fcar_scr[...]
    for g0 in range(0, pps, grp):
        pages = range(g0, min(g0 + grp, pps))
        s_parts = []
        for pi in pages:
            ft = _dot_exact01(lf_refs[pi][...], u01)
            s_parts.append(_dot(qx, k_refs[pi][...].astype(BF16)) - expand(ft + fcar))
            fcar = fcar + ft[:, page - 1:page]
        s = s_parts[0] if len(s_parts) == 1 else jnp.concatenate(s_parts, axis=1)

        def pv_pages(pb, pages=pages):
            out = None
            for n, pi in enumerate(pages):
                term = _dot_nt(pb[:, n * page:(n + 1) * page], v_refs[pi][...].astype(BF16))
                out = term if out is None else out + term
            return out

        update(s, pv_pages)
    fcar_scr[...] = fcar

    @pl.when(j == pl.num_programs(1) - 1)
    def _():
        fn = expand(cn_ref[...] + fcar_scr[...])
        sn = _dot_nt(qx_scr[...], kn_ref[...].astype(BF16)) - fn
        qi = lax.broadcasted_iota(I32, (nr, ds_), 0) & (ds_ - 1)
        key = lax.broadcasted_iota(I32, (nr, ds_), 1)
        sn = jnp.where(key <= qi, sn, NEG_BIG)
        update(sn, lambda pb: _dot(pb, vn_ref[...].astype(BF16)))
        o = jnp.where(own_head(), acc_scr[...] / l_scr[...], 0.0)
        o_ref[...] = o.reshape(nh, ds_, d).sum(axis=0)


def _fox_sample(page_table, z3, slabs, cnt, ckt, cvt, clft, nh, hd):
    b, npages = page_table.shape
    t, d = z3.shape[1], z3.shape[2]
    ds_ = t // b
    page = clft.shape[2]
    pps = PAGES_PER_STEP
    while npages % pps:
        pps //= 2
    grp = min(PAGES_PER_SOFTMAX, pps)
    nr = nh * ds_
    assert ds_ == SUBLANES and ckt.shape[1] == d
    tok = lambda slab: pl.BlockSpec((None, ds_, d), lambda bi, j, pt: (slab, bi, 0))

    def pg(rows, pi):
        return pl.BlockSpec((None, rows, page), lambda bi, j, pt: (pt[bi * npages + j * pps + pi], 0, 0))

    in_specs = [tok(slabs[0]), tok(slabs[1]), tok(slabs[2]),
                pl.BlockSpec((None, nh, ds_), lambda bi, j, pt: (bi, 0, 0))]
    in_specs += [pg(d, pi) for pi in range(pps)] * 2 + [pg(nh, pi) for pi in range(pps)]
    grid_spec = pltpu.PrefetchScalarGridSpec(
        num_scalar_prefetch=1,
        grid=(b, npages // pps),
        in_specs=in_specs,
        out_specs=pl.BlockSpec((ds_, d), lambda bi, j, pt: (bi, 0)),
        scratch_shapes=[pltpu.VMEM((nr, d), BF16), pltpu.VMEM((nr, 1), F32), pltpu.VMEM((nr, 1), F32),
                        pltpu.VMEM((nr, d), F32), pltpu.VMEM((nh, 1), F32)],
    )
    return pl.pallas_call(
        functools.partial(_fox_sample_kernel, nh=nh, hd=hd, scale=hd ** -0.5, pps=pps, grp=grp),
        grid_spec=grid_spec,
        out_shape=jax.ShapeDtypeStruct((t, d), F32),
        compiler_params=_params(2),
        name="fox_sample",
    )(page_table.reshape(-1), z3, z3, z3, cnt, *([ckt] * pps), *([cvt] * pps), *([clft] * pps))


def _post_kernel(yr_ref, ya_ref, ga_ref, gb_ref, x_ref, gt1_ref, sc2_ref, sh2_ref, wo_ref, gpost_ref,
                 gpre_ref, wr_ref, br_ref, x1_ref, h2_ref, te_ref, gate_ref):
    merged = jax.nn.sigmoid(ga_ref[...]) * yr_ref[...] + jax.nn.sigmoid(gb_ref[...]) * ya_ref[...]
    out = _dot(merged.astype(BF16), wo_ref[...])
    x1 = x_ref[...] + gt1_ref[...] * _rms(out, gpost_ref[...])
    x1_ref[...] = x1
    h2 = _rms(x1, gpre_ref[...]) * (1.0 + sc2_ref[...]) + sh2_ref[...]
    h2_ref[...] = h2
    logits = _dot3(h2, wr_ref[...]) + br_ref[...]
    ne = logits.shape[1]
    lane = lax.broadcasted_iota(I32, logits.shape, 1)
    vals, idxs = [], []
    for _ in range(TOP_K):
        m = jnp.max(logits, axis=1, keepdims=True)
        idx = jnp.min(jnp.where(logits == m, lane, ne), axis=1, keepdims=True)
        vals.append(m)
        idxs.append(idx)
        logits = jnp.where(lane == idx, -jnp.inf, logits)
    e = jnp.exp(jnp.concatenate(vals, axis=1) - vals[0])
    gate_ref[...] = e / jnp.sum(e, axis=1, keepdims=True)
    te_ref[...] = jnp.concatenate(idxs, axis=1)


def _post(y_rnn, y_attn, z3, x, gt1, sc2, sh2, wo, gpost, gpre, wr, br, rows_per_batch):
    t, d = x.shape
    ne = wr.shape[1]
    tm = min(POST_TILE, rows_per_batch if gt1.ndim == 3 else t)
    assert t % tm == 0
    row = pl.BlockSpec((tm, d), lambda i: (i, 0))
    vec = pl.BlockSpec((1, d), lambda i: (0, 0))
    mod = lambda m: _mod_spec(m, tm, rows_per_batch)
    return pl.pallas_call(
        _post_kernel,
        grid=(t // tm,),
        in_specs=[row, row,
                  pl.BlockSpec((None, tm, d), lambda i: (2, i, 0)),
                  pl.BlockSpec((None, tm, d), lambda i: (3, i, 0)),
                  row, mod(gt1), mod(sc2), mod(sh2),
                  pl.BlockSpec((d, d), lambda i: (0, 0)), vec, vec,
                  pl.BlockSpec((d, ne), lambda i: (0, 0)),
                  pl.BlockSpec((1, ne), lambda i: (0, 0))],
        out_specs=[row, row,
                   pl.BlockSpec((tm, TOP_K), lambda i: (i, 0)),
                   pl.BlockSpec((tm, TOP_K), lambda i: (i, 0))],
        out_shape=[jax.ShapeDtypeStruct((t, d), F32), jax.ShapeDtypeStruct((t, d), F32),
                   jax.ShapeDtypeStruct((t, TOP_K), I32), jax.ShapeDtypeStruct((t, TOP_K), F32)],
        compiler_params=_params(1),
        name="post_mixer",
    )(y_rnn, y_attn, z3, z3, x, gt1, sc2, sh2, wo, gpost, gpre, wr, br)


def _multi_hot(te, ne):
    lane = lax.broadcasted_iota(I32, (te.shape[0], ne), 1)
    hot = jnp.zeros((te.shape[0], ne), F32)
    for k in range(te.shape[1]):
        hot = hot + jnp.where(lane == te[:, k:k + 1], 1.0, 0.0)
    return hot


def _expert_kernel(be_ref, nv_ref, x_ref, wgu_ref, bgu_ref, wd_ref, bd_ref, o_ref, wgu_bf, wd_bf, *, fc):
    i = pl.program_id(0)
    f = wd_ref.shape[0]

    @pl.when(jnp.logical_or(i == 0, be_ref[i] != be_ref[jnp.maximum(i - 1, 0)]))
    def _():
        for c in range(2 * f // fc):
            wgu_bf[:, c * fc:(c + 1) * fc] = wgu_ref[:, c * fc:(c + 1) * fc].astype(BF16)
        for c in range(f // fc):
            wd_bf[c * fc:(c + 1) * fc, :] = wd_ref[c * fc:(c + 1) * fc, :].astype(BF16)

    @pl.when(i < nv_ref[0])
    def _():
        xb = x_ref[...].astype(BF16)
        acc = jnp.zeros(o_ref.shape, F32)
        for c in range(f // fc):
            glu = _dot(xb, wgu_bf[:, c * fc:(c + 1) * fc]) + bgu_ref[:, c * fc:(c + 1) * fc]
            lin = _dot(xb, wgu_bf[:, f + c * fc:f + (c + 1) * fc]) + bgu_ref[:, f + c * fc:f + (c + 1) * fc]
            glu = jnp.minimum(glu, SWIGLU_LIMIT)
            lin = jnp.clip(lin, -SWIGLU_LIMIT, SWIGLU_LIMIT)
            act = glu * jax.nn.sigmoid(SWIGLU_ALPHA * glu) * (lin + 1.0)
            acc = acc + _dot(act.astype(BF16), wd_bf[c * fc:(c + 1) * fc, :])
        o_ref[...] = acc + bd_ref[...]

    @pl.when(i >= nv_ref[0])
    def _():
        o_ref[...] = jnp.zeros(o_ref.shape, F32)


def _experts(be, nv, xr, wgu, bgu, wd, bd, bm):
    nr, d = xr.shape
    ne, _, f2 = wgu.shape
    f = f2 // 2
    fc = min(EXPERT_FF_CHUNK, f)
    grid_spec = pltpu.PrefetchScalarGridSpec(
        num_scalar_prefetch=2,
        grid=(nr // bm,),
        in_specs=[pl.BlockSpec((bm, d), lambda i, be, nv: (jnp.minimum(i, nv[0] - 1), 0)),
                  pl.BlockSpec((None, d, f2), lambda i, be, nv: (be[i], 0, 0)),
                  pl.BlockSpec((None, 1, f2), lambda i, be, nv: (be[i], 0, 0)),
                  pl.BlockSpec((None, f, d), lambda i, be, nv: (be[i], 0, 0)),
                  pl.BlockSpec((None, 1, d), lambda i, be, nv: (be[i], 0, 0))],
        out_specs=pl.BlockSpec((bm, d), lambda i, be, nv: (i, 0)),
        scratch_shapes=[pltpu.VMEM((d, f2), BF16), pltpu.VMEM((f, d), BF16)],
    )
    return pl.pallas_call(
        functools.partial(_expert_kernel, fc=fc),
        grid_spec=grid_spec,
        out_shape=jax.ShapeDtypeStruct((nr, d), F32),
        compiler_params=_params(1),
        name="moe_experts",
    )(be, nv, xr, wgu, bgu.reshape(ne, 1, f2), wd, bd.reshape(ne, 1, d))


def _run_plan_kernel(te_ref, lp_ref, n8_ref, soff8_ref, dpre8_ref, tot8_ref, carry_scr):
    i = pl.program_id(0)
    tt, ne = te_ref.shape[0], n8_ref.shape[1]

    @pl.when(i == 0)
    def _():
        carry_scr[...] = jnp.zeros(carry_scr.shape, F32)

    te = te_ref[...]
    hot = _multi_hot(te, ne)
    r = lax.broadcasted_iota(I32, (tt, tt), 0)
    c = lax.broadcasted_iota(I32, (tt, tt), 1)
    below = jnp.where(c < r, 1.0, 0.0).astype(BF16)
    rank = _dot(below, hot.astype(BF16))
    n8 = jnp.ceil(jnp.sum(hot, axis=0, keepdims=True) * (1.0 / SUBLANES))
    er = lax.broadcasted_iota(I32, (ne, ne), 0)
    ec = lax.broadcasted_iota(I32, (ne, ne), 1)
    before = jnp.where(er < ec, 1.0, 0.0).astype(BF16)
    soff8 = _dot_exact01(jnp.broadcast_to(n8, (SUBLANES, ne)), before)[0:1, :]
    lane = lax.broadcasted_iota(I32, (tt, ne), 1)
    pos = rank + soff8 * float(SUBLANES)
    cols = [jnp.sum(jnp.where(lane == te[:, k:k + 1], pos, 0.0), axis=1, keepdims=True) for k in range(te.shape[1])]
    lp_ref[...] = jnp.concatenate(cols, axis=1).astype(I32)
    n8_ref[...] = n8.astype(I32)
    soff8_ref[...] = soff8.astype(I32)
    dpre8_ref[...] = carry_scr[...].astype(I32)
    carry_scr[...] = carry_scr[...] + n8
    tot8_ref[...] = carry_scr[...]


def _block_plan_kernel(tot8_ref, dpre8_ref, dbase8_ref, tail8_ref, ntail8_ref, be_ref, nv_ref, *, bm):
    ne = tot8_ref.shape[1]
    per_blk = bm // SUBLANES
    nblk = jnp.ceil(tot8_ref[...] * (1.0 / per_blk))
    er = lax.broadcasted_iota(I32, (ne, ne), 0)
    ec = lax.broadcasted_iota(I32, (ne, ne), 1)
    before = jnp.where(er < ec, 1.0, 0.0).astype(BF16)
    start_blk = _dot_exact01(jnp.broadcast_to(nblk, (SUBLANES, ne)), before)[0:1, :]
    end_blk = start_blk + nblk
    dbase8_ref[...] = dpre8_ref[...] + (start_blk * float(per_blk)).astype(I32)
    tail8_ref[...] = (start_blk * float(per_blk) + tot8_ref[...]).astype(I32)
    ntail8_ref[...] = (nblk * float(per_blk) - tot8_ref[...]).astype(I32)
    blk = lax.broadcasted_iota(I32, be_ref.shape, 1).astype(F32)
    be = jnp.zeros(be_ref.shape, F32)
    for e in range(ne):
        be = be + jnp.where(blk >= end_blk[:, e:e + 1], 1.0, 0.0)
    be_ref[...] = jnp.minimum(be, float(ne - 1)).astype(I32)
    nv_ref[...] = jnp.broadcast_to(end_blk[:, ne - 1:ne], nv_ref.shape).astype(I32)


def _moe_tile(t):
    return MOE_TILE if t % MOE_TILE == 0 else LANES


def _plan_routes(te, ne, bm, nblocks, tt):
    t, k = te.shape
    assert t % tt == 0 and tt & (tt - 1) == 0
    ntiles = t // tt
    tab = pl.BlockSpec((None, 1, ne), lambda i: (i, 0, 0))
    tab_shape = jax.ShapeDtypeStruct((ntiles, 1, ne), I32)
    lp, n8, soff8, dpre8, tot8 = pl.pallas_call(
        _run_plan_kernel,
        grid=(ntiles,),
        in_specs=[pl.BlockSpec((tt, k), lambda i: (i, 0))],
        out_specs=[pl.BlockSpec((tt, k), lambda i: (i, 0)), tab, tab, tab,
                   pl.BlockSpec((1, ne), lambda i: (0, 0))],
        out_shape=[jax.ShapeDtypeStruct((t, k), I32), tab_shape, tab_shape, tab_shape,
                   jax.ShapeDtypeStruct((1, ne), F32)],
        scratch_shapes=[pltpu.VMEM((1, ne), F32)],
        compiler_params=_params(1),
        name="moe_run_plan",
    )(te)
    nbp = -(-nblocks // LANES) * LANES
    per_expert = jax.ShapeDtypeStruct((1, ne), I32)
    dbase8, tail8, ntail8, be, nv = pl.pallas_call(
        functools.partial(_block_plan_kernel, bm=bm),
        out_shape=[jax.ShapeDtypeStruct((ntiles, ne), I32), per_expert, per_expert,
                   jax.ShapeDtypeStruct((1, nbp), I32), jax.ShapeDtypeStruct((1, LANES), I32)],
        name="moe_block_plan",
    )(tot8, dpre8.reshape(ntiles, ne))
    runs = (n8.reshape(-1), soff8.reshape(-1), dbase8.reshape(-1))
    be, nv = be.reshape(-1), nv.reshape(-1)
    tails = (tail8.reshape(-1), ntail8.reshape(-1), nv)
    return lp, runs, tails, be, nv


def _for_each_run_chunk(tables, tile, ne, max8, fn):
    n8_ref, soff8_ref, dbase8_ref = tables

    def per_expert(e, carry):
        idx = tile * ne + e
        n, o, d = n8_ref[idx], soff8_ref[idx], dbase8_ref[idx]
        bit = max8
        while bit >= 1:
            @pl.when((n & bit) != 0)
            def _(o=o, d=d, bit=bit):
                fn(pl.multiple_of(o * SUBLANES, SUBLANES), pl.multiple_of(d * SUBLANES, SUBLANES), bit * SUBLANES)
            o = o + (n & bit)
            d = d + (n & bit)
            bit //= 2
        return carry

    lax.fori_loop(0, ne, per_expert, 0)


def _dispatch_runs_kernel(n8_ref, soff8_ref, dbase8_ref, tail8_ref, ntail8_ref, nv_ref, lpt_ref, *rest,
                          ne, group_tiles, max_tail8):
    src_refs = rest[:len(group_tiles)]
    xr_ref, xs, zb, sem = rest[len(group_tiles):]
    tile = pl.program_id(0)

    @pl.when(tile == 0)
    def _():
        zb[...] = jnp.zeros(zb.shape, F32)
        half = zb.shape[0]
        n_half = xr_ref.shape[0] // half

        def zero_unused(issue):
            def per_half(h, carry):
                issue(pltpu.make_async_copy(zb, xr_ref.at[pl.ds(pl.multiple_of(h * half, half), half)], sem))
                return carry
            lax.fori_loop(2 * nv_ref[0], n_half, per_half, 0)

        def zero_tails(issue):
            def per_expert(e, carry):
                n, d = ntail8_ref[e], tail8_ref[e]
                bit = max_tail8
                while bit >= 1:
                    @pl.when((n & bit) != 0)
                    def _(d=d, bit=bit):
                        issue(pltpu.make_async_copy(
                            zb.at[pl.ds(0, bit * SUBLANES)],
                            xr_ref.at[pl.ds(pl.multiple_of(d * SUBLANES, SUBLANES), bit * SUBLANES)], sem))
                    d = d + (n & bit)
                    bit //= 2
                return carry
            lax.fori_loop(0, ne, per_expert, 0)

        for fill in (zero_unused, zero_tails):
            fill(lambda cp: cp.start())
            fill(lambda cp: cp.wait())

    src = src_refs[-1][...]
    first_tile = sum(group_tiles)
    for g in range(len(group_tiles) - 2, -1, -1):
        first_tile -= group_tiles[g + 1]
        src = jnp.where(tile < first_tile, src_refs[g][...], src)
    tt = src.shape[0]
    rows = xs.shape[0]
    r = lax.broadcasted_iota(I32, (rows, tt), 0)
    lpt = lpt_ref[...]
    sel = jnp.zeros((rows, tt), F32)
    for k in range(lpt.shape[0]):
        sel = sel + jnp.where(r == lpt[k:k + 1, :], 1.0, 0.0)
    xs[...] = _dot(sel.astype(BF16), src.astype(BF16))

    def copy(o, d, n):
        return pltpu.make_async_copy(xs.at[pl.ds(o, n)], xr_ref.at[pl.ds(d, n)], sem)

    tables = (n8_ref, soff8_ref, dbase8_ref)
    _for_each_run_chunk(tables, tile, ne, tt // SUBLANES, lambda o, d, n: copy(o, d, n).start())
    _for_each_run_chunk(tables, tile, ne, tt // SUBLANES, lambda o, d, n: copy(o, d, n).wait())


def _dispatch_runs(runs, tails, lpt, srcs, ne, tt, xr_rows, bm):
    d = srcs[0].shape[1]
    k = lpt.shape[0]
    group_tiles = tuple(src.shape[0] // tt for src in srcs)
    assert all(src.shape[0] % tt == 0 for src in srcs) and sum(group_tiles) * tt == lpt.shape[1]
    max_tail8 = bm // SUBLANES // 2
    in_specs = [pl.BlockSpec((k, tt), lambda i, *_: (0, i))]
    first = 0
    for n in group_tiles:
        in_specs.append(pl.BlockSpec((tt, d), lambda i, *_, first=first, n=n: (jnp.clip(i - first, 0, n - 1), 0)))
        first += n
    grid_spec = pltpu.PrefetchScalarGridSpec(
        num_scalar_prefetch=len(runs) + len(tails),
        grid=(sum(group_tiles),),
        in_specs=in_specs,
        out_specs=pl.BlockSpec(memory_space=pl.ANY),
        scratch_shapes=[pltpu.VMEM((k * tt + SUBLANES * ne, d), F32),
                        pltpu.VMEM((max_tail8 * SUBLANES, d), F32), pltpu.SemaphoreType.DMA(())],
    )
    return pl.pallas_call(
        functools.partial(_dispatch_runs_kernel, ne=ne, group_tiles=group_tiles, max_tail8=max_tail8),
        grid_spec=grid_spec,
        out_shape=jax.ShapeDtypeStruct((xr_rows, d), F32),
        compiler_params=_params(1),
        name="moe_dispatch",
    )(*runs, *tails, lpt, *srcs)


def _combine_runs_kernel(n8_ref, soff8_ref, dbase8_ref, lp_ref, yr_ref, gate_ref, x1_ref, gt2_ref, g_ref,
                         o_ref, ys, sem, *, ne, tile_offset):
    tt = lp_ref.shape[0]
    rows = ys.shape[0]
    ys[rows - SUBLANES * ne:rows, :] = jnp.zeros((SUBLANES * ne, ys.shape[1]), F32)

    def copy(o, d, n):
        return pltpu.make_async_copy(yr_ref.at[pl.ds(d, n)], ys.at[pl.ds(o, n)], sem)

    tables = (n8_ref, soff8_ref, dbase8_ref)
    tile = pl.program_id(0) + tile_offset
    _for_each_run_chunk(tables, tile, ne, tt // SUBLANES, lambda o, d, n: copy(o, d, n).start())
    c = lax.broadcasted_iota(I32, (tt, rows), 1)
    lp = lp_ref[...]
    gate = gate_ref[...]
    mix = jnp.zeros((tt, rows), F32)
    for k in range(lp.shape[1]):
        mix = mix + jnp.where(c == lp[:, k:k + 1], gate[:, k:k + 1], 0.0)
    mix_hi, mix_lo = _split2(mix)
    _for_each_run_chunk(tables, tile, ne, tt // SUBLANES, lambda o, d, n: copy(o, d, n).wait())
    yb = ys[...].astype(BF16)
    f = _dot(mix_hi, yb) + _dot(mix_lo, yb)
    o_ref[...] = x1_ref[...] + gt2_ref[...] * _rms(f, g_ref[...])


def _combine_runs(tables, tile_offset, lp, yr, gate, x1, gt2, g, rows_per_batch, ne, tt):
    t, d = x1.shape
    k = lp.shape[1]
    assert t % tt == 0
    row = pl.BlockSpec((tt, d), lambda i, *_: (i, 0))
    grid_spec = pltpu.PrefetchScalarGridSpec(
        num_scalar_prefetch=3,
        grid=(t // tt,),
        in_specs=[pl.BlockSpec((tt, k), lambda i, *_: (i + tile_offset, 0)),
                  pl.BlockSpec(memory_space=pl.ANY),
                  pl.BlockSpec((tt, k), lambda i, *_: (i, 0)),
                  row, _mod_spec(gt2, tt, rows_per_batch),
                  pl.BlockSpec((1, d), lambda i, *_: (0, 0))],
        out_specs=row,
        scratch_shapes=[pltpu.VMEM((k * tt + SUBLANES * ne, d), F32), pltpu.SemaphoreType.DMA(())],
    )
    return pl.pallas_call(
        functools.partial(_combine_runs_kernel, ne=ne, tile_offset=tile_offset),
        grid_spec=grid_spec,
        out_shape=jax.ShapeDtypeStruct((t, d), F32),
        compiler_params=_params(1),
        name="moe_combine",
    )(*tables, lp, yr, gate, x1, gt2, g)


def _blockdiag_pairs(wa, wx, gw):
    nb, c, _ = wa.shape
    per = gw // c
    ng = nb // per
    eye = jnp.eye(per, dtype=wa.dtype)

    def bd(w):
        return jnp.einsum("gpcd,pq->gpcqd", w.reshape(ng, per, c, c), eye).reshape(ng, gw, gw)

    return jnp.concatenate([bd(wa), bd(wx)], axis=2).astype(BF16)


def _layer(xp, xs, ck, cv, clf, sconv, sh0, page_table, cp, cs, w):
    bp, s, d = xp.shape
    bs, ds_, _ = xs.shape
    nh = w["b_forget"].shape[0]
    hd = d // nh
    tp, ts = bp * s, bs * ds_
    kw = w["conv_w"].shape[0]
    ne = w["w_router"].shape[1]

    w_in = w["w_in"].astype(BF16)
    w_x, w_q, w_k, w_v = (w_in[:, n * d:(n + 1) * d] for n in range(4))
    w_ga, w_gb = w_in[:, 4 * d + nh:5 * d + nh], w_in[:, 5 * d + nh:]
    w4 = jnp.stack([w_x, w_q, w_ga, w_gb])
    wkvt = jnp.stack([w_k.T, w_v.T])
    w6 = jnp.stack([w_x, w_q, w_ga, w_gb, w_k, w_v])
    wft = w_in[:, 4 * d:4 * d + nh].T
    bfo = w["b_forget"].reshape(nh, 1)
    gw = min(2 * LANES, d)
    wbd = _blockdiag_pairs(w["rg_w_a"], w["rg_w_x"], gw)
    vec = lambda v: v.reshape(1, d)
    wo = w["w_out"].astype(BF16)
    wgu, wdn = w["w_gate_up"], w["w_down"]

    ada = _ada(jnp.concatenate([cp, cs], axis=0), w["w_ada"], w["b_ada"])
    mods_p = [m.reshape(bp, 1, d) for m in jnp.split(ada[:bp], 6, axis=-1)]
    mods_s = [jnp.repeat(m, ds_, axis=0) for m in jnp.split(ada[bp:], 6, axis=-1)]

    xpf, xsf = xp.reshape(tp, d), xs.reshape(ts, d)
    g_pre = vec(w["g_mix_pre"])
    zp, kt_p, vt_p, lft_p = _inproj(xpf, mods_p[1], mods_p[0], g_pre, w4, wkvt, wft, bfo, bp, s)
    zs, lft_s = _inproj(xsf, mods_s[1], mods_s[0], g_pre, w6, None, wft, bfo, 1, ds_)

    rnn_w = (w["conv_w"], vec(w["conv_b"]), wbd, vec(w["rg_b_a"]), vec(w["rg_b_x"]), vec(w["rg_lambda"]))
    zero_prev = jnp.zeros((bp, SUBLANES, d), F32)
    zero_h = jnp.zeros((bp, 1, d), F32)
    yr_p, conv_p, hl_p = _rnn_prompt(zp, zero_prev, zero_h, *rnn_w, reset_first=True)
    prev8 = jnp.pad(sconv, ((0, 0), (SUBLANES - (kw - 1), 0), (0, 0))).reshape(ts, d)
    h08 = jnp.pad(sh0[:, None, :], ((0, 0), (0, SUBLANES - 1), (0, 0))).reshape(ts, d)
    yr_s = _rnn_sample(zs, prev8, h08, *rnn_w)
    hl_s = yr_s.reshape(bs, ds_, d)[:, ds_ - 1]
    conv_s = zs[0].reshape(bs, ds_, d)[:, ds_ - (kw - 1):]

    ft_p = _cumsum_lanes(lft_p).reshape(bp, nh // 2, 2, s)
    ya_p = _fox_prompt(zp.reshape(4, bp, s, d), kt_p, vt_p, ft_p, hd).reshape(tp, d)
    cn_s = _cumsum_lanes(lft_s, seg=ds_).reshape(nh, bs, ds_).transpose(1, 0, 2)
    npool, page = ck.shape[0], ck.shape[1]
    ckt = ck.transpose(0, 2, 3, 1).reshape(npool, d, page)
    cvt = cv.transpose(0, 2, 3, 1).reshape(npool, d, page)
    ya_s = _fox_sample(page_table, zs, (1, 4, 5), cn_s, ckt, cvt, clf.transpose(0, 2, 1), nh, hd)

    post_w = (wo, vec(w["g_mix_post"]), vec(w["g_ffn_pre"]), w["w_router"], w["b_router"].reshape(1, ne))
    x1_p, h2_p, te_p, gate_p = _post(yr_p, ya_p, zp, xpf, mods_p[2], mods_p[4], mods_p[3], *post_w, s)
    x1_s, h2_s, te_s, gate_s = _post(yr_s, ya_s, zs, xsf, mods_s[2], mods_s[4], mods_s[3], *post_w, ds_)

    bm = EXPERT_BLOCK
    t_all = tp + ts
    tt = _moe_tile(t_all)
    assert tp % tt == 0 and ts % tt == 0
    run_pad = (SUBLANES - 1) * (t_all // tt) * ne
    nblocks = -(-(t_all * TOP_K + run_pad) // bm) + ne
    lp, tables, tails, be, nv = _plan_routes(jnp.concatenate([te_p, te_s], axis=0), ne, bm, nblocks, tt)
    lpt = lp.T
    xr = _dispatch_runs(tables, tails, lpt, (h2_p, h2_s), ne, tt, nblocks * bm, bm)
    yr = _experts(be, nv, xr, wgu, w["b_gate_up"], wdn, w["b_down"], bm)
    g_post = vec(w["g_ffn_post"])
    y_p = _combine_runs(tables, 0, lp, yr, gate_p, x1_p, mods_p[5], g_post, s, ne, tt)
    y_s = _combine_runs(tables, tp // tt, lp, yr, gate_s, x1_s, mods_s[5], g_post, ds_, ne, tt)

    heads_t = lambda zt_: zt_.reshape(bp, nh, hd, s).transpose(0, 3, 1, 2)
    out_p = (y_p.reshape(bp, s, d), heads_t(kt_p), heads_t(vt_p),
             lft_p.transpose(0, 2, 1), conv_p, hl_p.reshape(bp, d))
    out_s = (y_s.reshape(bs, ds_, d), zs[4].reshape(bs, ds_, nh, hd), zs[5].reshape(bs, ds_, nh, hd),
             lft_s[0].T.reshape(bs, ds_, nh), conv_s, hl_s)
    return out_p, out_s


def kernel(x_prompt, x_sample, cache_k, cache_v, cache_logf, state_conv, state_h, page_table, c_prompt, c_sample, w_ada, b_ada, g_mix_pre, g_mix_post, w_in, b_forget, conv_w, conv_b, rg_w_a, rg_b_a, rg_w_x, rg_b_x, rg_lambda, w_out, g_ffn_pre, g_ffn_post, w_router, b_router, w_gate_up, b_gate_up, w_down, b_down):
    weights = dict(w_ada=w_ada, b_ada=b_ada, g_mix_pre=g_mix_pre, g_mix_post=g_mix_post, w_in=w_in,
                   b_forget=b_forget, conv_w=conv_w, conv_b=conv_b, rg_w_a=rg_w_a, rg_b_a=rg_b_a,
                   rg_w_x=rg_w_x, rg_b_x=rg_b_x, rg_lambda=rg_lambda, w_out=w_out, g_ffn_pre=g_ffn_pre,
                   g_ffn_post=g_ffn_post, w_router=w_router, b_router=b_router, w_gate_up=w_gate_up,
                   b_gate_up=b_gate_up, w_down=w_down, b_down=b_down)
    depth = w_ada.shape[0]
    yp, ys = x_prompt, x_sample
    per_layer = []
    for l in range(depth):
        wl = {k: v[l] for k, v in weights.items()}
        out_p, out_s = _layer(yp, ys, cache_k[l], cache_v[l], cache_logf[l], state_conv[l], state_h[l],
                              page_table, c_prompt, c_sample, wl)
        yp, ys = out_p[0], out_s[0]
        per_layer.append(out_p[1:] + out_s[1:])
    stacked = [jnp.stack(leaf) for leaf in zip(*per_layer)]
    return (yp, ys, *stacked)
```

```python
import functools

import jax
import jax.numpy as jnp
from jax import lax
from jax.experimental import pallas as pl
from jax.experimental.pallas import tpu as pltpu

F32 = jnp.float32
BF16 = jnp.bfloat16
I32 = jnp.int32

EPS = 1e-6
LRU_C = 8.0
TOP_K = 4
SWIGLU_LIMIT = 7.0
SWIGLU_ALPHA = 1.702
NEG_BIG = -1e30

LANES = 128
SUBLANES = 8
VMEM_LIMIT = 56 * 1024 * 1024

ADA_COLS = 1536
IN_PROJ_TILE = 1024
RNN_CHUNK = 512
RNN_SAMPLE_ROWS = 16
ATTN_BLOCK = 512
PAGES_PER_STEP = 16
PAGES_PER_SOFTMAX = 16
POST_TILE = 512
MOE_TILE = 256
EXPERT_BLOCK = 512
EXPERT_FF_CHUNK = 512


def _params(n_grid_dims):
    return pltpu.CompilerParams(
        dimension_semantics=("arbitrary",) * n_grid_dims, vmem_limit_bytes=VMEM_LIMIT)


def _dot(a, b):
    return jnp.dot(a, b, preferred_element_type=F32)


def _dot_nt(a, b):
    return lax.dot_general(a, b, (((1,), (1,)), ((), ())), preferred_element_type=F32)


def _split2(x):
    hi = x.astype(BF16)
    lo = (x - hi.astype(F32)).astype(BF16)
    return hi, lo


def _split3(x):
    p1 = x.astype(BF16)
    r1 = x - p1.astype(F32)
    p2 = r1.astype(BF16)
    p3 = (r1 - p2.astype(F32)).astype(BF16)
    return p1, p2, p3


def _dot3(a, b):
    ah, al = _split2(a)
    bh, bl = _split2(b)
    return _dot(ah, bh) + _dot(ah, bl) + _dot(al, bh)


def _dot_exact01(x, u01):
    p1, p2, p3 = _split3(x)
    return _dot(p1, u01) + _dot(p2, u01) + _dot(p3, u01)


def _rms(x, g):
    ms = jnp.mean(x * x, axis=-1, keepdims=True)
    return x * lax.rsqrt(ms + EPS) * g


def _softplus(x):
    return jnp.maximum(x, 0.0) + jnp.log1p(jnp.exp(-jnp.abs(x)))


def _log_sigmoid(x):
    return jnp.minimum(x, 0.0) - jnp.log1p(jnp.exp(-jnp.abs(x)))


def _ada_kernel(c_ref, w_ref, b_ref, o_ref):
    c = c_ref[...]
    o_ref[...] = _dot3(c * jax.nn.sigmoid(c), w_ref[...]) + b_ref[...]


def _ada(c, w, b):
    n, d = c.shape
    nout = w.shape[1]
    tn = min(nout, ADA_COLS)
    assert nout % tn == 0
    return pl.pallas_call(
        _ada_kernel,
        grid=(nout // tn,),
        in_specs=[pl.BlockSpec((n, d), lambda j: (0, 0)),
                  pl.BlockSpec((d, tn), lambda j: (0, j)),
                  pl.BlockSpec((1, tn), lambda j: (0, j))],
        out_specs=pl.BlockSpec((n, tn), lambda j: (0, j)),
        out_shape=jax.ShapeDtypeStruct((n, nout), F32),
        compiler_params=_params(1),
        name="ada",
    )(c, w, b.reshape(1, nout))


def _inproj_kernel(x_ref, sc_ref, sh_ref, g_ref, w_ref, wft_ref, bf_ref, *rest, nz, nt):
    if nt:
        wt_ref, z_ref = rest[:2]
        zt_refs = rest[2:2 + nt]
    else:
        z_ref = rest[0]
    lft_ref, h_scr = rest[-2:]
    j = pl.program_id(1)

    @pl.when(j == 0)
    def _():
        h = _rms(x_ref[...], g_ref[...]) * (1.0 + sc_ref[...]) + sh_ref[...]
        hb = h.astype(BF16)
        h_scr[...] = hb
        lft_ref[...] = _log_sigmoid(_dot_nt(wft_ref[...], hb) + bf_ref[...])

    if nt:
        @pl.when(j < nz)
        def _():
            z_ref[...] = _dot(h_scr[...], w_ref[jnp.minimum(j, nz - 1)])

        for n in range(nt):
            @pl.when(j == nz + n)
            def _(n=n):
                zt_refs[n][...] = _dot_nt(wt_ref[n], h_scr[...])
    else:
        z_ref[...] = _dot(h_scr[...], w_ref[j])


def _mod_spec(mod, tm, rows_per_batch):
    if mod.ndim == 3:
        tiles_per_batch = rows_per_batch // tm
        return pl.BlockSpec((None, 1, mod.shape[-1]), lambda i, *_: (i // tiles_per_batch, 0, 0))
    return pl.BlockSpec((tm, mod.shape[-1]), lambda i, *_: (i, 0))


def _inproj(x, sc, sh, g, w, wt, wft, bfo, nbatch, rows_per_batch):
    t, d = x.shape
    nh = wft.shape[0]
    nz = w.shape[0]
    nt = 0 if wt is None else wt.shape[0]
    s = t // nbatch
    tm = min(IN_PROJ_TILE, s)
    assert s % tm == 0 and (sc.ndim == 2 or rows_per_batch % tm == 0)
    tpb = s // tm
    resident = lambda n: pl.BlockSpec((n, d, d), lambda i, j: (0, 0, 0), pipeline_mode=pl.Buffered(1))
    in_specs = [pl.BlockSpec((tm, d), lambda i, j: (i, 0)),
                _mod_spec(sc, tm, rows_per_batch), _mod_spec(sh, tm, rows_per_batch),
                pl.BlockSpec((1, d), lambda i, j: (0, 0)),
                resident(nz),
                pl.BlockSpec((nh, d), lambda i, j: (0, 0)),
                pl.BlockSpec((nh, 1), lambda i, j: (0, 0))]
    out_specs = [pl.BlockSpec((None, tm, d), lambda i, j: (jnp.minimum(j, nz - 1), i, 0))]
    out_shape = [jax.ShapeDtypeStruct((nz, t, d), F32)]
    args = [x, sc, sh, g, w, wft, bfo]
    if nt:
        in_specs.append(resident(nt))
        args.append(wt)
        for _ in range(nt):
            out_specs.append(pl.BlockSpec((None, d, tm), lambda i, j: (i // tpb, 0, i % tpb)))
            out_shape.append(jax.ShapeDtypeStruct((nbatch, d, s), F32))
    out_specs.append(pl.BlockSpec((None, nh, tm), lambda i, j: (i // tpb, 0, i % tpb)))
    out_shape.append(jax.ShapeDtypeStruct((nbatch, nh, s), F32))
    return pl.pallas_call(
        functools.partial(_inproj_kernel, nz=nz, nt=nt),
        grid=(t // tm, nz + nt),
        in_specs=in_specs,
        out_specs=out_specs,
        out_shape=out_shape,
        scratch_shapes=[pltpu.VMEM((tm, d), BF16)],
        compiler_params=_params(2),
        name="inproj",
    )(*args)


def _cumsum_kernel(lf_ref, o_ref, *, cw, seg):
    n = lf_ref.shape[1]
    r = lax.broadcasted_iota(I32, (cw, cw), 0)
    c = lax.broadcasted_iota(I32, (cw, cw), 1)
    keep = r <= c
    if seg is not None:
        keep = jnp.logical_and(keep, (r // seg) == (c // seg))
    u01 = jnp.where(keep, 1.0, 0.0).astype(BF16)
    carry = jnp.zeros((lf_ref.shape[0], 1), F32)
    for i in range(n // cw):
        f = _dot_exact01(lf_ref[:, i * cw:(i + 1) * cw], u01)
        if seg is None:
            f = f + carry
            carry = f[:, cw - 1:cw]
        o_ref[:, i * cw:(i + 1) * cw] = f


def _cumsum_lanes(lft, seg=None):
    nb, nh, s = lft.shape
    block = s if seg is None else min(s, 2 * LANES)
    cw = min(2 * LANES, block)
    assert s % block == 0 and block % cw == 0 and (seg is None or cw % seg == 0)
    spec = pl.BlockSpec((None, nh, block), lambda b, n: (b, 0, n))
    return pl.pallas_call(
        functools.partial(_cumsum_kernel, cw=cw, seg=seg),
        grid=(nb, s // block),
        in_specs=[spec],
        out_specs=spec,
        out_shape=jax.ShapeDtypeStruct((nb, nh, s), F32),
        compiler_params=_params(2),
        name="cumsum_logf",
    )(lft)


def _rglru_coeffs(xc, wbd_ref, ba, bx, lam):
    ng, gw = wbd_ref.shape[0], wbd_ref.shape[1]
    xb = xc.astype(BF16)
    ra, ri = [], []
    for g in range(ng):
        o = _dot(xb[:, g * gw:(g + 1) * gw], wbd_ref[g])
        ra.append(o[:, :gw])
        ri.append(o[:, gw:])
    r = jax.nn.sigmoid((ra[0] if ng == 1 else jnp.concatenate(ra, axis=1)) + ba)
    i = jax.nn.sigmoid((ri[0] if ng == 1 else jnp.concatenate(ri, axis=1)) + bx)
    log_a = -LRU_C * r * _softplus(-lam)
    a = jnp.exp(log_a)
    th = jnp.tanh(log_a)
    mult = jnp.sqrt(-2.0 * th / (1.0 - th))
    return a, mult, i


def _scan8(a8, u8, row):
    for s in (1, 2, 4):
        a_sh = pltpu.roll(a8, s, axis=0)
        u_sh = pltpu.roll(u8, s, axis=0)
        m = row >= s
        u8 = u8 + a8 * jnp.where(m, u_sh, 0.0)
        a8 = a8 * jnp.where(m, a_sh, 1.0)
    return a8, u8


def _rnn_prompt_kernel(x_ref, prev_ref, h0_ref, cw_ref, cb_ref, wbd_ref, ba_ref, bx_ref, lam_ref,
                       y_ref, cs_ref, hl_ref, ext_scr, a_scr, u_scr, h_scr, *, reset_first):
    c = pl.program_id(1)
    tc, d = x_ref.shape
    kw = cw_ref.shape[0]

    @pl.when(c == 0)
    def _():
        ext_scr[0:SUBLANES, :] = prev_ref[...]
        h_scr[...] = h0_ref[...]

    ext_scr[SUBLANES:SUBLANES + tc, :] = x_ref[...]
    w = cw_ref[...]
    xc = cb_ref[...]
    for j in range(kw):
        off = SUBLANES - (kw - 1) + j
        xc = xc + w[j:j + 1, :] * ext_scr[off:off + tc, :]
    ext_scr[0:SUBLANES, :] = ext_scr[tc:tc + SUBLANES, :]

    a, mult, i = _rglru_coeffs(xc, wbd_ref, ba_ref[...], bx_ref[...], lam_ref[...])
    if reset_first:
        row = lax.broadcasted_iota(I32, (tc, 1), 0)
        mult = jnp.where(row == jnp.where(c == 0, 0, -1), 1.0, mult)
    a_scr[...] = a
    u_scr[...] = mult * i * xc

    row8 = lax.broadcasted_iota(I32, (SUBLANES, d), 0)

    def body(g, h):
        r0 = pl.multiple_of(g * SUBLANES, SUBLANES)
        a8, u8 = _scan8(a_scr[pl.ds(r0, SUBLANES), :], u_scr[pl.ds(r0, SUBLANES), :], row8)
        h8 = u8 + a8 * h
        y_ref[pl.ds(r0, SUBLANES), :] = h8
        return h8[SUBLANES - 1:SUBLANES, :]

    h = lax.fori_loop(0, tc // SUBLANES, body, h_scr[...], unroll=4)
    h_scr[...] = h

    @pl.when(c == pl.num_programs(1) - 1)
    def _():
        hl_ref[...] = h
        cs_ref[...] = x_ref[tc - (kw - 1):tc, :]


def _rnn_prompt(x3, prev8, h0, cw, cb, wbd, ba, bx, lam, reset_first):
    b = prev8.shape[0]
    t, d = x3.shape[1], x3.shape[2]
    s = t // b
    kw = cw.shape[0]
    tc = min(RNN_CHUNK, s)
    nc = s // tc
    assert s % tc == 0 and kw - 1 <= SUBLANES and tc % SUBLANES == 0
    vec = pl.BlockSpec((1, d), lambda bi, ci: (0, 0))
    return pl.pallas_call(
        functools.partial(_rnn_prompt_kernel, reset_first=reset_first),
        grid=(b, nc),
        in_specs=[pl.BlockSpec((None, tc, d), lambda bi, ci: (0, bi * nc + ci, 0)),
                  pl.BlockSpec((None, SUBLANES, d), lambda bi, ci: (bi, 0, 0)),
                  pl.BlockSpec((None, 1, d), lambda bi, ci: (bi, 0, 0)),
                  pl.BlockSpec((kw, d), lambda bi, ci: (0, 0)), vec,
                  pl.BlockSpec(wbd.shape, lambda bi, ci: (0, 0, 0)), vec, vec, vec],
        out_specs=[pl.BlockSpec((tc, d), lambda bi, ci: (bi * nc + ci, 0)),
                   pl.BlockSpec((None, kw - 1, d), lambda bi, ci: (bi, 0, 0)),
                   pl.BlockSpec((None, 1, d), lambda bi, ci: (bi, 0, 0))],
        out_shape=[jax.ShapeDtypeStruct((t, d), F32),
                   jax.ShapeDtypeStruct((b, kw - 1, d), F32),
                   jax.ShapeDtypeStruct((b, 1, d), F32)],
        scratch_shapes=[pltpu.VMEM((tc + SUBLANES, d), F32), pltpu.VMEM((tc, d), F32),
                        pltpu.VMEM((tc, d), F32), pltpu.VMEM((1, d), F32)],
        compiler_params=_params(2),
        name="rnn_prompt",
    )(x3, prev8, h0, cw, cb, wbd, ba, bx, lam)


def _rnn_sample_kernel(x_ref, prev_ref, h0_ref, cw_ref, cb_ref, wbd_ref, ba_ref, bx_ref, lam_ref,
                       y_ref):
    r, d = x_ref.shape
    kw = cw_ref.shape[0]
    x = x_ref[...]
    p = prev_ref[...]
    step = lax.broadcasted_iota(I32, (r, 1), 0) & (SUBLANES - 1)
    w = cw_ref[...]
    xc = cb_ref[...]
    for j in range(kw):
        back = kw - 1 - j
        if back == 0:
            xs = x
        else:
            xs = jnp.where(step >= back, pltpu.roll(x, back, axis=0),
                           pltpu.roll(p, (r - SUBLANES + back) % r, axis=0))
        xc = xc + w[j:j + 1, :] * xs
    a, mult, i = _rglru_coeffs(xc, wbd_ref, ba_ref[...], bx_ref[...], lam_ref[...])
    u = mult * i * xc + a * h0_ref[...]
    for s in (1, 2, 4):
        m = step >= s
        a_sh = pltpu.roll(a, s, axis=0)
        u_sh = pltpu.roll(u, s, axis=0)
        u = u + a * jnp.where(m, u_sh, 0.0)
        a = a * jnp.where(m, a_sh, 1.0)
    y_ref[...] = u


def _rnn_sample(x3, prev8, h08, cw, cb, wbd, ba, bx, lam):
    t, d = x3.shape[1], x3.shape[2]
    nb_total = t // SUBLANES
    kw = cw.shape[0]
    nb = min(RNN_SAMPLE_ROWS, nb_total)
    r = nb * SUBLANES
    assert nb_total % nb == 0 and nb % SUBLANES == 0
    vec = pl.BlockSpec((1, d), lambda i: (0, 0))
    return pl.pallas_call(
        _rnn_sample_kernel,
        grid=(nb_total // nb,),
        in_specs=[pl.BlockSpec((None, r, d), lambda i: (0, i, 0)),
                  pl.BlockSpec((r, d), lambda i: (i, 0)),
                  pl.BlockSpec((r, d), lambda i: (i, 0)),
                  pl.BlockSpec((kw, d), lambda i: (0, 0)), vec,
                  pl.BlockSpec(wbd.shape, lambda i: (0, 0, 0)), vec, vec, vec],
        out_specs=pl.BlockSpec((r, d), lambda i: (i, 0)),
        out_shape=jax.ShapeDtypeStruct((t, d), F32),
        compiler_params=_params(1),
        name="rnn_sample",
    )(x3, prev8, h08, cw, cb, wbd, ba, bx, lam)


def _fox_prompt_kernel(q_ref, kt_ref, vt_ref, f_ref, o_ref, *, hd, scale):
    i = pl.program_id(2)
    tq = q_ref.shape[0]
    lane = lax.broadcasted_iota(I32, (1, 2 * hd), 1)
    q = q_ref[...] * scale
    qs = jnp.concatenate([jnp.where(lane < hd, q, 0.0), jnp.where(lane >= hd, q, 0.0)], axis=0).astype(BF16)

    def scores(kb):
        k0 = pl.multiple_of(kb * tq, tq)
        kt = kt_ref[:, pl.ds(k0, tq)].astype(BF16)
        fb = f_ref[:, pl.ds(k0, tq)]
        s = _dot(qs, kt)
        return jnp.concatenate([s[:tq] - fb[0:1, :], s[tq:] - fb[1:2, :]], axis=0)

    vrow = lax.broadcasted_iota(I32, (2 * hd, 1), 0)

    def update(kb, s, m, acc):
        k0 = pl.multiple_of(kb * tq, tq)
        vt = vt_ref[:, pl.ds(k0, tq)]
        vt0 = jnp.where(vrow < hd, vt, 1.0).astype(BF16)
        vt1 = jnp.where(vrow >= hd, vt, 1.0).astype(BF16)
        m_new = jnp.maximum(m, jnp.max(s, axis=-1, keepdims=True))
        p = jnp.exp(s - m_new).astype(BF16)
        pv = jnp.concatenate([_dot_nt(p[:tq], vt0), _dot_nt(p[tq:], vt1)], axis=0)
        return m_new, jnp.exp(m - m_new) * acc + pv

    def body(kb, carry):
        s, m, acc = carry
        return (scores(kb + 1),) + update(kb, s, m, acc)

    init = (scores(0), jnp.full((2 * tq, 1), NEG_BIG, F32), jnp.zeros((2 * tq, 2 * hd), F32))
    s, m, acc = lax.fori_loop(0, i, body, init)
    r = lax.broadcasted_iota(I32, (2 * tq, tq), 0) & (tq - 1)
    c = lax.broadcasted_iota(I32, (2 * tq, tq), 1)
    _, acc = update(i, jnp.where(c <= r, s, NEG_BIG), m, acc)
    o = acc / pltpu.roll(acc, hd, axis=1)
    o_ref[...] = jnp.where(lane < hd, o[:tq], o[tq:])


def _fox_prompt(z4, kt, vt, ft4, hd):
    _, b, s, d = z4.shape
    assert 2 * hd == LANES and d % LANES == 0
    npair = d // LANES
    tq = min(ATTN_BLOCK, s)
    assert s % tq == 0 and tq & (tq - 1) == 0
    return pl.pallas_call(
        functools.partial(_fox_prompt_kernel, hd=hd, scale=hd ** -0.5),
        grid=(b, npair, s // tq),
        in_specs=[pl.BlockSpec((None, None, tq, LANES), lambda bi, j, i: (1, bi, i, j)),
                  pl.BlockSpec((None, LANES, s), lambda bi, j, i: (bi, j, 0)),
                  pl.BlockSpec((None, LANES, s), lambda bi, j, i: (bi, j, 0)),
                  pl.BlockSpec((None, None, 2, s), lambda bi, j, i: (bi, j, 0, 0))],
        out_specs=pl.BlockSpec((None, tq, LANES), lambda bi, j, i: (bi, i, j)),
        out_shape=jax.ShapeDtypeStruct((b, s, d), F32),
        compiler_params=_params(3),
        name="fox_prompt",
    )(z4, kt, vt, ft4)


def _fox_sample_kernel(pt_ref, q_ref, kn_ref, vn_ref, cn_ref, *rest, nh, hd, scale, pps, grp):
    del pt_ref
    k_refs, v_refs, lf_refs = rest[:pps], rest[pps:2 * pps], rest[2 * pps:3 * pps]
    o_ref, qx_scr, m_scr, l_scr, acc_scr, fcar_scr = rest[3 * pps:]
    j = pl.program_id(1)
    ds_, d = q_ref.shape
    page = lf_refs[0].shape[1]
    nr = nh * ds_

    def expand(x):
        return jnp.broadcast_to(x[:, None, :], (nh, ds_, x.shape[1])).reshape(nr, x.shape[1])

    def own_head():
        rh = lax.broadcasted_iota(I32, (nr, d), 0) // ds_
        ch = lax.broadcasted_iota(I32, (nr, d), 1) // hd
        return rh == ch

    @pl.when(j == 0)
    def _():
        q = q_ref[...] * scale
        qt = jnp.broadcast_to(q[None], (nh, ds_, d)).reshape(nr, d)
        qx_scr[...] = jnp.where(own_head(), qt, 0.0).astype(BF16)
        m_scr[...] = jnp.full(m_scr.shape, NEG_BIG, F32)
        l_scr[...] = jnp.zeros(l_scr.shape, F32)
        acc_scr[...] = jnp.zeros(acc_scr.shape, F32)
        fcar_scr[...] = jnp.zeros(fcar_scr.shape, F32)

    def update(s, pv_fn):
        m_old = m_scr[...]
        m_new = jnp.maximum(m_old, jnp.max(s, axis=-1, keepdims=True))
        p = jnp.exp(s - m_new)
        alpha = jnp.exp(m_old - m_new)
        l_scr[...] = alpha * l_scr[...] + jnp.sum(p, axis=-1, keepdims=True)
        m_scr[...] = m_new
        acc_scr[...] = alpha * acc_scr[...] + pv_fn(p.astype(BF16))

    r = lax.broadcasted_iota(I32, (page, page), 0)
    c = lax.broadcasted_iota(I32, (page, page), 1)
    u01 = jnp.where(r <= c, 1.0, 0.0).astype(BF16)
    qx = qx_scr[...]
    fcar = fcar_scr[...]
    for g0 in range(0, pps, grp):
        pages = range(g0, min(g0 + grp, pps))
        s_parts = []
        for pi in pages:
            ft = _dot_exact01(lf_refs[pi][...], u01)
            s_parts.append(_dot(qx, k_refs[pi][...].astype(BF16)) - expand(ft + fcar))
            fcar = fcar + ft[:, page - 1:page]
        s = s_parts[0] if len(s_parts) == 1 else jnp.concatenate(s_parts, axis=1)

        def pv_pages(pb, pages=pages):
            out = None
            for n, pi in enumerate(pages):
                term = _dot_nt(pb[:, n * page:(n + 1) * page], v_refs[pi][...].astype(BF16))
                out = term if out is None else out + term
            return out

        update(s, pv_pages)
    fcar_scr[...] = fcar

    @pl.when(j == pl.num_programs(1) - 1)
    def _():
        fn = expand(cn_ref[...] + fcar_scr[...])
        sn = _dot_nt(qx_scr[...], kn_ref[...].astype(BF16)) - fn
        qi = lax.broadcasted_iota(I32, (nr, ds_), 0) & (ds_ - 1)
        key = lax.broadcasted_iota(I32, (nr, ds_), 1)
        sn = jnp.where(key <= qi, sn, NEG_BIG)
        update(sn, lambda pb: _dot(pb, vn_ref[...].astype(BF16)))
        o = jnp.where(own_head(), acc_scr[...] / l_scr[...], 0.0)
        o_ref[...] = o.reshape(nh, ds_, d).sum(axis=0)


def _fox_sample(page_table, z3, slabs, cnt, ckt, cvt, clft, nh, hd):
    b, npages = page_table.shape
    t, d = z3.shape[1], z3.shape[2]
    ds_ = t // b
    page = clft.shape[2]
    pps = PAGES_PER_STEP
    while npages % pps:
        pps //= 2
    grp = min(PAGES_PER_SOFTMAX, pps)
    nr = nh * ds_
    assert ds_ == SUBLANES and ckt.shape[1] == d
    tok = lambda slab: pl.BlockSpec((None, ds_, d), lambda bi, j, pt: (slab, bi, 0))

    def pg(rows, pi):
        return pl.BlockSpec((None, rows, page), lambda bi, j, pt: (pt[bi * npages + j * pps + pi], 0, 0))

    in_specs = [tok(slabs[0]), tok(slabs[1]), tok(slabs[2]),
                pl.BlockSpec((None, nh, ds_), lambda bi, j, pt: (bi, 0, 0))]
    in_specs += [pg(d, pi) for pi in range(pps)] * 2 + [pg(nh, pi) for pi in range(pps)]
    grid_spec = pltpu.PrefetchScalarGridSpec(
        num_scalar_prefetch=1,
        grid=(b, npages // pps),
        in_specs=in_specs,
        out_specs=pl.BlockSpec((ds_, d), lambda bi, j, pt: (bi, 0)),
        scratch_shapes=[pltpu.VMEM((nr, d), BF16), pltpu.VMEM((nr, 1), F32), pltpu.VMEM((nr, 1), F32),
                        pltpu.VMEM((nr, d), F32), pltpu.VMEM((nh, 1), F32)],
    )
    return pl.pallas_call(
        functools.partial(_fox_sample_kernel, nh=nh, hd=hd, scale=hd ** -0.5, pps=pps, grp=grp),
        grid_spec=grid_spec,
        out_shape=jax.ShapeDtypeStruct((t, d), F32),
        compiler_params=_params(2),
        name="fox_sample",
    )(page_table.reshape(-1), z3, z3, z3, cnt, *([ckt] * pps), *([cvt] * pps), *([clft] * pps))


def _post_kernel(yr_ref, ya_ref, ga_ref, gb_ref, x_ref, gt1_ref, sc2_ref, sh2_ref, wo_ref, gpost_ref,
                 gpre_ref, wr_ref, br_ref, x1_ref, h2_ref, te_ref, gate_ref):
    merged = jax.nn.sigmoid(ga_ref[...]) * yr_ref[...] + jax.nn.sigmoid(gb_ref[...]) * ya_ref[...]
    out = _dot(merged.astype(BF16), wo_ref[...])
    x1 = x_ref[...] + gt1_ref[...] * _rms(out, gpost_ref[...])
    x1_ref[...] = x1
    h2 = _rms(x1, gpre_ref[...]) * (1.0 + sc2_ref[...]) + sh2_ref[...]
    h2_ref[...] = h2
    logits = _dot3(h2, wr_ref[...]) + br_ref[...]
    ne = logits.shape[1]
    lane = lax.broadcasted_iota(I32, logits.shape, 1)
    vals, idxs = [], []
    for _ in range(TOP_K):
        m = jnp.max(logits, axis=1, keepdims=True)
        idx = jnp.min(jnp.where(logits == m, lane, ne), axis=1, keepdims=True)
        vals.append(m)
        idxs.append(idx)
        logits = jnp.where(lane == idx, -jnp.inf, logits)
    e = jnp.exp(jnp.concatenate(vals, axis=1) - vals[0])
    gate_ref[...] = e / jnp.sum(e, axis=1, keepdims=True)
    te_ref[...] = jnp.concatenate(idxs, axis=1)


def _post(y_rnn, y_attn, z3, x, gt1, sc2, sh2, wo, gpost, gpre, wr, br, rows_per_batch):
    t, d = x.shape
    ne = wr.shape[1]
    tm = min(POST_TILE, rows_per_batch if gt1.ndim == 3 else t)
    assert t % tm == 0
    row = pl.BlockSpec((tm, d), lambda i: (i, 0))
    vec = pl.BlockSpec((1, d), lambda i: (0, 0))
    mod = lambda m: _mod_spec(m, tm, rows_per_batch)
    return pl.pallas_call(
        _post_kernel,
        grid=(t // tm,),
        in_specs=[row, row,
                  pl.BlockSpec((None, tm, d), lambda i: (2, i, 0)),
                  pl.BlockSpec((None, tm, d), lambda i: (3, i, 0)),
                  row, mod(gt1), mod(sc2), mod(sh2),
                  pl.BlockSpec((d, d), lambda i: (0, 0)), vec, vec,
                  pl.BlockSpec((d, ne), lambda i: (0, 0)),
                  pl.BlockSpec((1, ne), lambda i: (0, 0))],
        out_specs=[row, row,
                   pl.BlockSpec((tm, TOP_K), lambda i: (i, 0)),
                   pl.BlockSpec((tm, TOP_K), lambda i: (i, 0))],
        out_shape=[jax.ShapeDtypeStruct((t, d), F32), jax.ShapeDtypeStruct((t, d), F32),
                   jax.ShapeDtypeStruct((t, TOP_K), I32), jax.ShapeDtypeStruct((t, TOP_K), F32)],
        compiler_params=_params(1),
        name="post_mixer",
    )(y_rnn, y_attn, z3, z3, x, gt1, sc2, sh2, wo, gpost, gpre, wr, br)


def _multi_hot(te, ne):
    lane = lax.broadcasted_iota(I32, (te.shape[0], ne), 1)
    hot = jnp.zeros((te.shape[0], ne), F32)
    for k in range(te.shape[1]):
        hot = hot + jnp.where(lane == te[:, k:k + 1], 1.0, 0.0)
    return hot


def _expert_kernel(be_ref, nv_ref, x_ref, wgu_ref, bgu_ref, wd_ref, bd_ref, o_ref, wgu_bf, wd_bf, *, fc):
    i = pl.program_id(0)
    f = wd_ref.shape[0]

    @pl.when(jnp.logical_or(i == 0, be_ref[i] != be_ref[jnp.maximum(i - 1, 0)]))
    def _():
        for c in range(2 * f // fc):
            wgu_bf[:, c * fc:(c + 1) * fc] = wgu_ref[:, c * fc:(c + 1) * fc].astype(BF16)
        for c in range(f // fc):
            wd_bf[c * fc:(c + 1) * fc, :] = wd_ref[c * fc:(c + 1) * fc, :].astype(BF16)

    @pl.when(i < nv_ref[0])
    def _():
        xb = x_ref[...].astype(BF16)
        acc = jnp.zeros(o_ref.shape, F32)
        for c in range(f // fc):
            glu = _dot(xb, wgu_bf[:, c * fc:(c + 1) * fc]) + bgu_ref[:, c * fc:(c + 1) * fc]
            lin = _dot(xb, wgu_bf[:, f + c * fc:f + (c + 1) * fc]) + bgu_ref[:, f + c * fc:f + (c + 1) * fc]
            glu = jnp.minimum(glu, SWIGLU_LIMIT)
            lin = jnp.clip(lin, -SWIGLU_LIMIT, SWIGLU_LIMIT)
            act = glu * jax.nn.sigmoid(SWIGLU_ALPHA * glu) * (lin + 1.0)
            acc = acc + _dot(act.astype(BF16), wd_bf[c * fc:(c + 1) * fc, :])
        o_ref[...] = acc + bd_ref[...]

    @pl.when(i >= nv_ref[0])
    def _():
        o_ref[...] = jnp.zeros(o_ref.shape, F32)


def _experts(be, nv, xr, wgu, bgu, wd, bd, bm):
    nr, d = xr.shape
    ne, _, f2 = wgu.shape
    f = f2 // 2
    fc = min(EXPERT_FF_CHUNK, f)
    grid_spec = pltpu.PrefetchScalarGridSpec(
        num_scalar_prefetch=2,
        grid=(nr // bm,),
        in_specs=[pl.BlockSpec((bm, d), lambda i, be, nv: (jnp.minimum(i, nv[0] - 1), 0)),
                  pl.BlockSpec((None, d, f2), lambda i, be, nv: (be[i], 0, 0)),
                  pl.BlockSpec((None, 1, f2), lambda i, be, nv: (be[i], 0, 0)),
                  pl.BlockSpec((None, f, d), lambda i, be, nv: (be[i], 0, 0)),
                  pl.BlockSpec((None, 1, d), lambda i, be, nv: (be[i], 0, 0))],
        out_specs=pl.BlockSpec((bm, d), lambda i, be, nv: (i, 0)),
        scratch_shapes=[pltpu.VMEM((d, f2), BF16), pltpu.VMEM((f, d), BF16)],
    )
    return pl.pallas_call(
        functools.partial(_expert_kernel, fc=fc),
        grid_spec=grid_spec,
        out_shape=jax.ShapeDtypeStruct((nr, d), F32),
        compiler_params=_params(1),
        name="moe_experts",
    )(be, nv, xr, wgu, bgu.reshape(ne, 1, f2), wd, bd.reshape(ne, 1, d))


def _run_plan_kernel(te_ref, lp_ref, n8_ref, soff8_ref, dpre8_ref, tot8_ref, carry_scr):
    i = pl.program_id(0)
    tt, ne = te_ref.shape[0], n8_ref.shape[1]

    @pl.when(i == 0)
    def _():
        carry_scr[...] = jnp.zeros(carry_scr.shape, F32)

    te = te_ref[...]
    hot = _multi_hot(te, ne)
    r = lax.broadcasted_iota(I32, (tt, tt), 0)
    c = lax.broadcasted_iota(I32, (tt, tt), 1)
    below = jnp.where(c < r, 1.0, 0.0).astype(BF16)
    rank = _dot(below, hot.astype(BF16))
    n8 = jnp.ceil(jnp.sum(hot, axis=0, keepdims=True) * (1.0 / SUBLANES))
    er = lax.broadcasted_iota(I32, (ne, ne), 0)
    ec = lax.broadcasted_iota(I32, (ne, ne), 1)
    before = jnp.where(er < ec, 1.0, 0.0).astype(BF16)
    soff8 = _dot_exact01(jnp.broadcast_to(n8, (SUBLANES, ne)), before)[0:1, :]
    lane = lax.broadcasted_iota(I32, (tt, ne), 1)
    pos = rank + soff8 * float(SUBLANES)
    cols = [jnp.sum(jnp.where(lane == te[:, k:k + 1], pos, 0.0), axis=1, keepdims=True) for k in range(te.shape[1])]
    lp_ref[...] = jnp.concatenate(cols, axis=1).astype(I32)
    n8_ref[...] = n8.astype(I32)
    soff8_ref[...] = soff8.astype(I32)
    dpre8_ref[...] = carry_scr[...].astype(I32)
    carry_scr[...] = carry_scr[...] + n8
    tot8_ref[...] = carry_scr[...]


def _block_plan_kernel(tot8_ref, dpre8_ref, dbase8_ref, tail8_ref, ntail8_ref, be_ref, nv_ref, *, bm):
    ne = tot8_ref.shape[1]
    per_blk = bm // SUBLANES
    nblk = jnp.ceil(tot8_ref[...] * (1.0 / per_blk))
    er = lax.broadcasted_iota(I32, (ne, ne), 0)
    ec = lax.broadcasted_iota(I32, (ne, ne), 1)
    before = jnp.where(er < ec, 1.0, 0.0).astype(BF16)
    start_blk = _dot_exact01(jnp.broadcast_to(nblk, (SUBLANES, ne)), before)[0:1, :]
    end_blk = start_blk + nblk
    dbase8_ref[...] = dpre8_ref[...] + (start_blk * float(per_blk)).astype(I32)
    tail8_ref[...] = (start_blk * float(per_blk) + tot8_ref[...]).astype(I32)
    ntail8_ref[...] = (nblk * float(per_blk) - tot8_ref[...]).astype(I32)
    blk = lax.broadcasted_iota(I32, be_ref.shape, 1).astype(F32)
    be = jnp.zeros(be_ref.shape, F32)
    for e in range(ne):
        be = be + jnp.where(blk >= end_blk[:, e:e + 1], 1.0, 0.0)
    be_ref[...] = jnp.minimum(be, float(ne - 1)).astype(I32)
    nv_ref[...] = jnp.broadcast_to(end_blk[:, ne - 1:ne], nv_ref.shape).astype(I32)


def _moe_tile(t):
    return MOE_TILE if t % MOE_TILE == 0 else LANES


def _plan_routes(te, ne, bm, nblocks, tt):
    t, k = te.shape
    assert t % tt == 0 and tt & (tt - 1) == 0
    ntiles = t // tt
    tab = pl.BlockSpec((None, 1, ne), lambda i: (i, 0, 0))
    tab_shape = jax.ShapeDtypeStruct((ntiles, 1, ne), I32)
    lp, n8, soff8, dpre8, tot8 = pl.pallas_call(
        _run_plan_kernel,
        grid=(ntiles,),
        in_specs=[pl.BlockSpec((tt, k), lambda i: (i, 0))],
        out_specs=[pl.BlockSpec((tt, k), lambda i: (i, 0)), tab, tab, tab,
                   pl.BlockSpec((1, ne), lambda i: (0, 0))],
        out_shape=[jax.ShapeDtypeStruct((t, k), I32), tab_shape, tab_shape, tab_shape,
                   jax.ShapeDtypeStruct((1, ne), F32)],
        scratch_shapes=[pltpu.VMEM((1, ne), F32)],
        compiler_params=_params(1),
        name="moe_run_plan",
    )(te)
    nbp = -(-nblocks // LANES) * LANES
    per_expert = jax.ShapeDtypeStruct((1, ne), I32)
    dbase8, tail8, ntail8, be, nv = pl.pallas_call(
        functools.partial(_block_plan_kernel, bm=bm),
        out_shape=[jax.ShapeDtypeStruct((ntiles, ne), I32), per_expert, per_expert,
                   jax.ShapeDtypeStruct((1, nbp), I32), jax.ShapeDtypeStruct((1, LANES), I32)],
        name="moe_block_plan",
    )(tot8, dpre8.reshape(ntiles, ne))
    runs = (n8.reshape(-1), soff8.reshape(-1), dbase8.reshape(-1))
    be, nv = be.reshape(-1), nv.reshape(-1)
    tails = (tail8.reshape(-1), ntail8.reshape(-1), nv)
    return lp, runs, tails, be, nv


def _for_each_run_chunk(tables, tile, ne, max8, fn):
    n8_ref, soff8_ref, dbase8_ref = tables

    def per_expert(e, carry):
        idx = tile * ne + e
        n, o, d = n8_ref[idx], soff8_ref[idx], dbase8_ref[idx]
        bit = max8
        while bit >= 1:
            @pl.when((n & bit) != 0)
            def _(o=o, d=d, bit=bit):
                fn(pl.multiple_of(o * SUBLANES, SUBLANES), pl.multiple_of(d * SUBLANES, SUBLANES), bit * SUBLANES)
            o = o + (n & bit)
            d = d + (n & bit)
            bit //= 2
        return carry

    lax.fori_loop(0, ne, per_expert, 0)


def _dispatch_runs_kernel(n8_ref, soff8_ref, dbase8_ref, tail8_ref, ntail8_ref, nv_ref, lpt_ref, *rest,
                          ne, group_tiles, max_tail8):
    src_refs = rest[:len(group_tiles)]
    xr_ref, xs, zb, sem = rest[len(group_tiles):]
    tile = pl.program_id(0)

    @pl.when(tile == 0)
    def _():
        zb[...] = jnp.zeros(zb.shape, F32)
        half = zb.shape[0]
        n_half = xr_ref.shape[0] // half

        def zero_unused(issue):
            def per_half(h, carry):
                issue(pltpu.make_async_copy(zb, xr_ref.at[pl.ds(pl.multiple_of(h * half, half), half)], sem))
                return carry
            lax.fori_loop(2 * nv_ref[0], n_half, per_half, 0)

        def zero_tails(issue):
            def per_expert(e, carry):
                n, d = ntail8_ref[e], tail8_ref[e]
                bit = max_tail8
                while bit >= 1:
                    @pl.when((n & bit) != 0)
                    def _(d=d, bit=bit):
                        issue(pltpu.make_async_copy(
                            zb.at[pl.ds(0, bit * SUBLANES)],
                            xr_ref.at[pl.ds(pl.multiple_of(d * SUBLANES, SUBLANES), bit * SUBLANES)], sem))
                    d = d + (n & bit)
                    bit //= 2
                return carry
            lax.fori_loop(0, ne, per_expert, 0)

        for fill in (zero_unused, zero_tails):
            fill(lambda cp: cp.start())
            fill(lambda cp: cp.wait())

    src = src_refs[-1][...]
    first_tile = sum(group_tiles)
    for g in range(len(group_tiles) - 2, -1, -1):
        first_tile -= group_tiles[g + 1]
        src = jnp.where(tile < first_tile, src_refs[g][...], src)
    tt = src.shape[0]
    rows = xs.shape[0]
    r = lax.broadcasted_iota(I32, (rows, tt), 0)
    lpt = lpt_ref[...]
    sel = jnp.zeros((rows, tt), F32)
    for k in range(lpt.shape[0]):
        sel = sel + jnp.where(r == lpt[k:k + 1, :], 1.0, 0.0)
    xs[...] = _dot(sel.astype(BF16), src.astype(BF16))

    def copy(o, d, n):
        return pltpu.make_async_copy(xs.at[pl.ds(o, n)], xr_ref.at[pl.ds(d, n)], sem)

    tables = (n8_ref, soff8_ref, dbase8_ref)
    _for_each_run_chunk(tables, tile, ne, tt // SUBLANES, lambda o, d, n: copy(o, d, n).start())
    _for_each_run_chunk(tables, tile, ne, tt // SUBLANES, lambda o, d, n: copy(o, d, n).wait())


def _dispatch_runs(runs, tails, lpt, srcs, ne, tt, xr_rows, bm):
    d = srcs[0].shape[1]
    k = lpt.shape[0]
    group_tiles = tuple(src.shape[0] // tt for src in srcs)
    assert all(src.shape[0] % tt == 0 for src in srcs) and sum(group_tiles) * tt == lpt.shape[1]
    max_tail8 = bm // SUBLANES // 2
    in_specs = [pl.BlockSpec((k, tt), lambda i, *_: (0, i))]
    first = 0
    for n in group_tiles:
        in_specs.append(pl.BlockSpec((tt, d), lambda i, *_, first=first, n=n: (jnp.clip(i - first, 0, n - 1), 0)))
        first += n
    grid_spec = pltpu.PrefetchScalarGridSpec(
        num_scalar_prefetch=len(runs) + len(tails),
        grid=(sum(group_tiles),),
        in_specs=in_specs,
        out_specs=pl.BlockSpec(memory_space=pl.ANY),
        scratch_shapes=[pltpu.VMEM((k * tt + SUBLANES * ne, d), F32),
                        pltpu.VMEM((max_tail8 * SUBLANES, d), F32), pltpu.SemaphoreType.DMA(())],
    )
    return pl.pallas_call(
        functools.partial(_dispatch_runs_kernel, ne=ne, group_tiles=group_tiles, max_tail8=max_tail8),
        grid_spec=grid_spec,
        out_shape=jax.ShapeDtypeStruct((xr_rows, d), F32),
        compiler_params=_params(1),
        name="moe_dispatch",
    )(*runs, *tails, lpt, *srcs)


def _combine_runs_kernel(n8_ref, soff8_ref, dbase8_ref, lp_ref, yr_ref, gate_ref, x1_ref, gt2_ref, g_ref,
                         o_ref, ys, sem, *, ne, tile_offset):
    tt = lp_ref.shape[0]
    rows = ys.shape[0]
    ys[rows - SUBLANES * ne:rows, :] = jnp.zeros((SUBLANES * ne, ys.shape[1]), F32)

    def copy(o, d, n):
        return pltpu.make_async_copy(yr_ref.at[pl.ds(d, n)], ys.at[pl.ds(o, n)], sem)

    tables = (n8_ref, soff8_ref, dbase8_ref)
    tile = pl.program_id(0) + tile_offset
    _for_each_run_chunk(tables, tile, ne, tt // SUBLANES, lambda o, d, n: copy(o, d, n).start())
    c = lax.broadcasted_iota(I32, (tt, rows), 1)
    lp = lp_ref[...]
    gate = gate_ref[...]
    mix = jnp.zeros((tt, rows), F32)
    for k in range(lp.shape[1]):
        mix = mix + jnp.where(c == lp[:, k:k + 1], gate[:, k:k + 1], 0.0)
    mix_hi, mix_lo = _split2(mix)
    _for_each_run_chunk(tables, tile, ne, tt // SUBLANES, lambda o, d, n: copy(o, d, n).wait())
    yb = ys[...].astype(BF16)
    f = _dot(mix_hi, yb) + _dot(mix_lo, yb)
    o_ref[...] = x1_ref[...] + gt2_ref[...] * _rms(f, g_ref[...])


def _combine_runs(tables, tile_offset, lp, yr, gate, x1, gt2, g, rows_per_batch, ne, tt):
    t, d = x1.shape
    k = lp.shape[1]
    assert t % tt == 0
    row = pl.BlockSpec((tt, d), lambda i, *_: (i, 0))
    grid_spec = pltpu.PrefetchScalarGridSpec(
        num_scalar_prefetch=3,
        grid=(t // tt,),
        in_specs=[pl.BlockSpec((tt, k), lambda i, *_: (i + tile_offset, 0)),
                  pl.BlockSpec(memory_space=pl.ANY),
                  pl.BlockSpec((tt, k), lambda i, *_: (i, 0)),
                  row, _mod_spec(gt2, tt, rows_per_batch),
                  pl.BlockSpec((1, d), lambda i, *_: (0, 0))],
        out_specs=row,
        scratch_shapes=[pltpu.VMEM((k * tt + SUBLANES * ne, d), F32), pltpu.SemaphoreType.DMA(())],
    )
    return pl.pallas_call(
        functools.partial(_combine_runs_kernel, ne=ne, tile_offset=tile_offset),
        grid_spec=grid_spec,
        out_shape=jax.ShapeDtypeStruct((t, d), F32),
        compiler_params=_params(1),
        name="moe_combine",
    )(*tables, lp, yr, gate, x1, gt2, g)


def _blockdiag_pairs(wa, wx, gw):
    nb, c, _ = wa.shape
    per = gw // c
    ng = nb // per
    eye = jnp.eye(per, dtype=wa.dtype)

    def bd(w):
        return jnp.einsum("gpcd,pq->gpcqd", w.reshape(ng, per, c, c), eye).reshape(ng, gw, gw)

    return jnp.concatenate([bd(wa), bd(wx)], axis=2).astype(BF16)


def _layer(xp, xs, ck, cv, clf, sconv, sh0, page_table, cp, cs, w):
    bp, s, d = xp.shape
    bs, ds_, _ = xs.shape
    nh = w["b_forget"].shape[0]
    hd = d // nh
    tp, ts = bp * s, bs * ds_
    kw = w["conv_w"].shape[0]
    ne = w["w_router"].shape[1]

    w_in = w["w_in"].astype(BF16)
    w_x, w_q, w_k, w_v = (w_in[:, n * d:(n + 1) * d] for n in range(4))
    w_ga, w_gb = w_in[:, 4 * d + nh:5 * d + nh], w_in[:, 5 * d + nh:]
    w4 = jnp.stack([w_x, w_q, w_ga, w_gb])
    wkvt = jnp.stack([w_k.T, w_v.T])
    w6 = jnp.stack([w_x, w_q, w_ga, w_gb, w_k, w_v])
    wft = w_in[:, 4 * d:4 * d + nh].T
    bfo = w["b_forget"].reshape(nh, 1)
    gw = min(2 * LANES, d)
    wbd = _blockdiag_pairs(w["rg_w_a"], w["rg_w_x"], gw)
    vec = lambda v: v.reshape(1, d)
    wo = w["w_out"].astype(BF16)
    wgu, wdn = w["w_gate_up"], w["w_down"]

    ada = _ada(jnp.concatenate([cp, cs], axis=0), w["w_ada"], w["b_ada"])
    mods_p = [m.reshape(bp, 1, d) for m in jnp.split(ada[:bp], 6, axis=-1)]
    mods_s = [jnp.repeat(m, ds_, axis=0) for m in jnp.split(ada[bp:], 6, axis=-1)]

    xpf, xsf = xp.reshape(tp, d), xs.reshape(ts, d)
    g_pre = vec(w["g_mix_pre"])
    zp, kt_p, vt_p, lft_p = _inproj(xpf, mods_p[1], mods_p[0], g_pre, w4, wkvt, wft, bfo, bp, s)
    zs, lft_s = _inproj(xsf, mods_s[1], mods_s[0], g_pre, w6, None, wft, bfo, 1, ds_)

    rnn_w = (w["conv_w"], vec(w["conv_b"]), wbd, vec(w["rg_b_a"]), vec(w["rg_b_x"]), vec(w["rg_lambda"]))
    zero_prev = jnp.zeros((bp, SUBLANES, d), F32)
    zero_h = jnp.zeros((bp, 1, d), F32)
    yr_p, conv_p, hl_p = _rnn_prompt(zp, zero_prev, zero_h, *rnn_w, reset_first=True)
    prev8 = jnp.pad(sconv, ((0, 0), (SUBLANES - (kw - 1), 0), (0, 0))).reshape(ts, d)
    h08 = jnp.pad(sh0[:, None, :], ((0, 0), (0, SUBLANES - 1), (0, 0))).reshape(ts, d)
    yr_s = _rnn_sample(zs, prev8, h08, *rnn_w)
    hl_s = yr_s.reshape(bs, ds_, d)[:, ds_ - 1]
    conv_s = zs[0].reshape(bs, ds_, d)[:, ds_ - (kw - 1):]

    ft_p = _cumsum_lanes(lft_p).reshape(bp, nh // 2, 2, s)
    ya_p = _fox_prompt(zp.reshape(4, bp, s, d), kt_p, vt_p, ft_p, hd).reshape(tp, d)
    cn_s = _cumsum_lanes(lft_s, seg=ds_).reshape(nh, bs, ds_).transpose(1, 0, 2)
    npool, page = ck.shape[0], ck.shape[1]
    ckt = ck.transpose(0, 2, 3, 1).reshape(npool, d, page)
    cvt = cv.transpose(0, 2, 3, 1).reshape(npool, d, page)
    ya_s = _fox_sample(page_table, zs, (1, 4, 5), cn_s, ckt, cvt, clf.transpose(0, 2, 1), nh, hd)

    post_w = (wo, vec(w["g_mix_post"]), vec(w["g_ffn_pre"]), w["w_router"], w["b_router"].reshape(1, ne))
    x1_p, h2_p, te_p, gate_p = _post(yr_p, ya_p, zp, xpf, mods_p[2], mods_p[4], mods_p[3], *post_w, s)
    x1_s, h2_s, te_s, gate_s = _post(yr_s, ya_s, zs, xsf, mods_s[2], mods_s[4], mods_s[3], *post_w, ds_)

    bm = EXPERT_BLOCK
    t_all = tp + ts
    tt = _moe_tile(t_all)
    assert tp % tt == 0 and ts % tt == 0
    run_pad = (SUBLANES - 1) * (t_all // tt) * ne
    nblocks = -(-(t_all * TOP_K + run_pad) // bm) + ne
    lp, tables, tails, be, nv = _plan_routes(jnp.concatenate([te_p, te_s], axis=0), ne, bm, nblocks, tt)
    lpt = lp.T
    xr = _dispatch_runs(tables, tails, lpt, (h2_p, h2_s), ne, tt, nblocks * bm, bm)
    yr = _experts(be, nv, xr, wgu, w["b_gate_up"], wdn, w["b_down"], bm)
    g_post = vec(w["g_ffn_post"])
    y_p = _combine_runs(tables, 0, lp, yr, gate_p, x1_p, mods_p[5], g_post, s, ne, tt)
    y_s = _combine_runs(tables, tp // tt, lp, yr, gate_s, x1_s, mods_s[5], g_post, ds_, ne, tt)

    heads_t = lambda zt_: zt_.reshape(bp, nh, hd, s).transpose(0, 3, 1, 2)
    out_p = (y_p.reshape(bp, s, d), heads_t(kt_p), heads_t(vt_p),
             lft_p.transpose(0, 2, 1), conv_p, hl_p.reshape(bp, d))
    out_s = (y_s.reshape(bs, ds_, d), zs[4].reshape(bs, ds_, nh, hd), zs[5].reshape(bs, ds_, nh, hd),
             lft_s[0].T.reshape(bs, ds_, nh), conv_s, hl_s)
    return out_p, out_s


def kernel(x_prompt, x_sample, cache_k, cache_v, cache_logf, state_conv, state_h, page_table, c_prompt, c_sample, w_ada, b_ada, g_mix_pre, g_mix_post, w_in, b_forget, conv_w, conv_b, rg_w_a, rg_b_a, rg_w_x, rg_b_x, rg_lambda, w_out, g_ffn_pre, g_ffn_post, w_router, b_router, w_gate_up, b_gate_up, w_down, b_down):
    weights = dict(w_ada=w_ada, b_ada=b_ada, g_mix_pre=g_mix_pre, g_mix_post=g_mix_post, w_in=w_in,
                   b_forget=b_forget, conv_w=conv_w, conv_b=conv_b, rg_w_a=rg_w_a, rg_b_a=rg_b_a,
                   rg_w_x=rg_w_x, rg_b_x=rg_b_x, rg_lambda=rg_lambda, w_out=w_out, g_ffn_pre=g_ffn_pre,
                   g_ffn_post=g_ffn_post, w_router=w_router, b_router=b_router, w_gate_up=w_gate_up,
                   b_gate_up=b_gate_up, w_down=w_down, b_down=b_down)
    depth = w_ada.shape[0]
    yp, ys = x_prompt, x_sample
    per_layer = []
    for l in range(depth):
        wl = {k: v[l] for k, v in weights.items()}
        out_p, out_s = _layer(yp, ys, cache_k[l], cache_v[l], cache_logf[l], state_conv[l], state_h[l],
                              page_table, c_prompt, c_sample, wl)
        yp, ys = out_p[0], out_s[0]
        per_layer.append(out_p[1:] + out_s[1:])
    stacked = [jnp.stack(leaf) for leaf in zip(*per_layer)]
    return (yp, ys, *stacked)
```
